```python
import math
import jax, jax.numpy as jnp
from jax import lax
import numpy as np

D_MODEL = 2048
BATCH = 8
SEQ = 2048
DEPTH = 2
DEC_BATCH = 128
DEC_SEQ = 8
PAST_LEN = 8192
PAGE_SIZE = 128

H_A = 4
DK_A = 128
DV_A = 256
W_A = H_A * DV_A
RET_CHUNK = 128
ROPE_BASE = 10000.0
N_POOL_GROUPS = 4
W_B = 1024
GW_B = W_B // N_POOL_GROUPS
POOL_WINDOWS = (2, 4, 8, 16)
POOL_BUF = 15
H_C = 16
KV_C = 4
G_C = H_C // KV_C
HD_C = 64
W_C = H_C * HD_C
WINDOW = 128
ATT_BLOCK = 128
NUM_BUCKETS = 32
MAX_DISTANCE = 128
SPLITS = (H_A * DK_A, H_A * DK_A, W_A, W_A,
          W_B, W_B,
          H_C * HD_C, KV_C * HD_C, KV_C * HD_C, W_C,
          D_MODEL, D_MODEL, D_MODEL)
N_IN = 13824
LN_EPS = 1e-5
RMS_EPS = 1e-6

kernel_name = 'hybrid_retention_pool_swa_deepnorm_step'


def layer_norm(x, g, b):
    xf = x.astype(jnp.float32)
    mu = jnp.mean(xf, -1, keepdims=True)
    var = jnp.mean(jnp.square(xf - mu), -1, keepdims=True)
    return ((xf - mu) * lax.rsqrt(var + LN_EPS) * g.astype(jnp.float32) + b.astype(jnp.float32)).astype(x.dtype)


def rotary(x, pos):
    half = x.shape[-1] // 2
    inv = ROPE_BASE ** (-jnp.arange(half, dtype=jnp.float32) / half)
    ang = pos[:, None] * inv[None, :]
    cos = jnp.cos(ang)[None, :, None, :]
    sin = jnp.sin(ang)[None, :, None, :]
    x1, x2 = x[..., :half], x[..., half:]
    return jnp.concatenate([x1 * cos - x2 * sin, x1 * sin + x2 * cos], -1)


def retention(q, k, v, s0, pos0):
    B, T = q.shape[0], q.shape[1]
    C = min(RET_CHUNK, T)
    n = T // C
    pos = pos0 + jnp.arange(T, dtype=jnp.float32)
    q = rotary(q, pos)
    k = rotary(k, pos) * (DK_A ** -0.5)
    lg = jnp.log1p(-jnp.exp2(-5.0 - jnp.arange(H_A, dtype=jnp.float32)))
    idx = jnp.arange(C, dtype=jnp.float32)
    diff = idx[:, None] - idx[None, :]
    dmask = jnp.where(diff >= 0, jnp.exp(lg[:, None, None] * jnp.maximum(diff, 0.0)), 0.0)
    q_decay = jnp.exp(lg[None, :] * (idx[:, None] + 1.0))[None, :, :, None]
    k_decay = jnp.exp(lg[None, :] * (C - 1.0 - idx[:, None]))[None, :, :, None]
    chunk_decay = jnp.exp(lg * C)[None, :, None, None]

    def to_chunks(a):
        return a.reshape(B, n, C, *a.shape[2:]).swapaxes(0, 1)

    def step(s, inp):
        qc, kc, vc = inp
        sc = jnp.einsum('bihd,bjhd->bhij', qc, kc) * dmask
        o = jnp.einsum('bhij,bjhv->bihv', sc, vc) + jnp.einsum('bihd,bhdv->bihv', qc, s) * q_decay
        s = s * chunk_decay + jnp.einsum('bjhd,bjhv->bhdv', kc * k_decay, vc)
        return s, o

    s, o = lax.scan(step, s0, (to_chunks(q), to_chunks(k), to_chunks(v)))
    o = o.swapaxes(0, 1).reshape(B, T, H_A, DV_A)
    return o, s


def multi_scale_pool(u_ext, pos0, T):
    c = jnp.cumsum(u_ext, axis=1)
    c = jnp.concatenate([jnp.zeros_like(c[:, :1]), c], axis=1)
    pos = pos0 + jnp.arange(T)
    end = c[:, POOL_BUF + 1:]
    outs = []
    for g, w in enumerate(POOL_WINDOWS):
        lo, hi = g * GW_B, (g + 1) * GW_B
        start = c[:, POOL_BUF + 1 - w: POOL_BUF + 1 - w + T, lo:hi]
        cnt = jnp.minimum(pos + 1, w).astype(jnp.float32)
        outs.append((end[:, :, lo:hi] - start) / cnt[None, :, None])
    return jnp.concatenate(outs, -1) - u_ext[:, POOL_BUF:]


def t5_bucket(dist):
    max_exact = NUM_BUCKETS // 2
    d = dist.astype(jnp.float32)
    large = max_exact + (jnp.log(jnp.maximum(d, 1.0) / max_exact) / math.log(MAX_DISTANCE / max_exact)
                         * (NUM_BUCKETS - max_exact)).astype(jnp.int32)
    large = jnp.minimum(large, NUM_BUCKETS - 1)
    return jnp.where(dist < max_exact, dist, large)


def window_attention(q, k_ext, v_ext, pos0, sinks, rel_bias):
    B, T = q.shape[0], q.shape[1]
    bq = min(ATT_BLOCK, T)
    nb = T // bq
    span = bq + WINDOW
    kidx = (jnp.arange(nb) * bq)[:, None] + jnp.arange(span)[None, :]
    kb = k_ext[:, kidx]
    vb = v_ext[:, kidx]
    qb = q.reshape(B, nb, bq, KV_C, G_C, HD_C)
    s = jnp.einsum('bnqkgd,bnskd->bnkgqs', qb, kb).astype(jnp.float32) * (HD_C ** -0.5)
    dist = jnp.arange(bq)[:, None] + WINDOW - jnp.arange(span)[None, :]
    bias = rel_bias.astype(jnp.float32)[t5_bucket(jnp.maximum(dist, 0))]
    bias = bias.transpose(2, 0, 1).reshape(KV_C, G_C, bq, span)
    kpos = pos0 - WINDOW + kidx
    valid = ((dist >= 0) & (dist < WINDOW))[None] & (kpos >= 0)[:, None, :]
    s = jnp.where(valid[None, :, None, None], s + bias, jnp.finfo(jnp.float32).min)
    sink = sinks.astype(jnp.float32).reshape(KV_C, G_C)[None, None, :, :, None, None]
    m = jnp.maximum(jnp.max(s, -1, keepdims=True), sink)
    p = jnp.exp(s - m)
    p = p / (jnp.sum(p, -1, keepdims=True) + jnp.exp(sink - m))
    o = jnp.einsum('bnkgqs,bnskd->bnqkgd', p.astype(vb.dtype), vb)
    return o.reshape(B, T, W_C)


def mixer_layer(x, pos0, s_ret, win_k, win_v, pool_buf, w_in, w_ret_o, w_pool_map, pool_scale,
                w_pool_o, sinks, w_att_o, w_out, ln_g, ln_b, rel_bias, alpha):
    B, T, _ = x.shape
    dt = x.dtype
    f32 = jnp.float32
    h = x @ w_in
    offs = [int(o) for o in np.cumsum(SPLITS)[:-1]]
    qa, ka, va, ga, ub, gb, qc, kc, vc, gc, ma, mb, mc = jnp.split(h, offs, axis=-1)
    oa, s_new = retention(qa.reshape(B, T, H_A, DK_A).astype(f32), ka.reshape(B, T, H_A, DK_A).astype(f32),
                          va.reshape(B, T, H_A, DV_A).astype(f32), s_ret.astype(f32), pos0)
    oa = oa * lax.rsqrt(jnp.mean(oa * oa, -1, keepdims=True) + RMS_EPS)
    ya = (oa.reshape(B, T, W_A).astype(dt) * jax.nn.silu(ga)) @ w_ret_o
    u_ext = jnp.concatenate([pool_buf.astype(dt), ub], axis=1)
    p = multi_scale_pool(u_ext.astype(f32), pos0, T)
    p = jnp.einsum('btgc,gcd->btgd', p.reshape(B, T, N_POOL_GROUPS, GW_B), w_pool_map.astype(f32))
    p = p.reshape(B, T, W_B) * pool_scale.astype(f32)
    yb = (p.astype(dt) * jax.nn.silu(gb)) @ w_pool_o
    k_ext = jnp.concatenate([win_k.astype(dt), kc.reshape(B, T, KV_C, HD_C)], axis=1)
    v_ext = jnp.concatenate([win_v.astype(dt), vc.reshape(B, T, KV_C, HD_C)], axis=1)
    oc = window_attention(qc.reshape(B, T, H_C, HD_C), k_ext, v_ext, pos0, sinks, rel_bias)
    yc = (oc.astype(dt) * jax.nn.silu(gc)) @ w_att_o
    merged = jax.nn.sigmoid(ma) * ya + jax.nn.sigmoid(mb) * yb + jax.nn.sigmoid(mc) * yc
    y = merged @ w_out
    x_new = layer_norm(alpha * x + y, ln_g, ln_b)
    return x_new, s_new.astype(dt), k_ext[:, -WINDOW:], v_ext[:, -WINDOW:], u_ext[:, -POOL_BUF:]


def setup_inputs(seed: int = 0) -> dict:
    key = jax.random.key(seed)
    ks = jax.random.split(key, 20)
    beta = (8.0 * DEPTH) ** -0.25
    nrm = lambda k, shape: jax.random.normal(k, shape, dtype=jnp.float32)
    win = min(WINDOW, PAST_LEN)
    return {
        'x_prompt': nrm(ks[0], (BATCH, SEQ, D_MODEL)),
        'x_sample': nrm(ks[1], (DEC_BATCH, DEC_SEQ, D_MODEL)),
        'state_ret': nrm(ks[2], (DEPTH, DEC_BATCH, H_A, DK_A, DV_A)),
        'cache_win_k': nrm(ks[3], (DEPTH, DEC_BATCH, win, KV_C, HD_C)),
        'cache_win_v': nrm(ks[4], (DEPTH, DEC_BATCH, win, KV_C, HD_C)),
        'state_pool': nrm(ks[5], (DEPTH, DEC_BATCH, POOL_BUF, W_B)),
        'w_in': nrm(ks[6], (DEPTH, D_MODEL, N_IN)) * D_MODEL ** -0.5,
        'w_ret_o': nrm(ks[7], (DEPTH, W_A, D_MODEL)) * (W_A ** -0.5 * beta),
        'w_pool_map': nrm(ks[8], (DEPTH, N_POOL_GROUPS, GW_B, GW_B)) * GW_B ** -0.5,
        'pool_scale': 1.0 + 0.1 * nrm(ks[9], (DEPTH, W_B)),
        'w_pool_o': nrm(ks[10], (DEPTH, W_B, D_MODEL)) * (W_B ** -0.5 * beta),
        'attn_sinks': 0.5 * nrm(ks[11], (DEPTH, H_C)),
        'w_att_o': nrm(ks[12], (DEPTH, W_C, D_MODEL)) * (W_C ** -0.5 * beta),
        'w_out': nrm(ks[13], (DEPTH, D_MODEL, D_MODEL)) * (D_MODEL ** -0.5 * beta),
        'ln_g': 1.0 + 0.02 * nrm(ks[14], (DEPTH, D_MODEL)),
        'ln_b': 0.02 * nrm(ks[15], (DEPTH, D_MODEL)),
        'rel_bias': 0.5 * nrm(ks[16], (NUM_BUCKETS, H_C)),
    }


def reference(x_prompt, x_sample, state_ret, cache_win_k, cache_win_v, state_pool, w_in, w_ret_o,
              w_pool_map, pool_scale, w_pool_o, attn_sinks, w_att_o, w_out, ln_g, ln_b, rel_bias):
    alpha = (2.0 * DEPTH) ** 0.25
    dt = x_prompt.dtype
    bp = x_prompt.shape[0]
    zero_ret = jnp.zeros((bp, H_A, DK_A, DV_A), dt)
    zero_win = jnp.zeros((bp, WINDOW, KV_C, HD_C), dt)
    zero_pool = jnp.zeros((bp, POOL_BUF, W_B), dt)
    xp, xs = x_prompt, x_sample
    ret_p, ret_s, kp_l, ks_l, vp_l, vs_l, pp_l, ps_l = [], [], [], [], [], [], [], []
    for l in range(DEPTH):
        xp, r1, k1, v1, p1 = mixer_layer(xp, 0, zero_ret, zero_win, zero_win, zero_pool,
                                         w_in[l], w_ret_o[l], w_pool_map[l], pool_scale[l], w_pool_o[l],
                                         attn_sinks[l], w_att_o[l], w_out[l], ln_g[l], ln_b[l], rel_bias, alpha)
        xs, r2, k2, v2, p2 = mixer_layer(xs, PAST_LEN, state_ret[l], cache_win_k[l], cache_win_v[l], state_pool[l],
                                         w_in[l], w_ret_o[l], w_pool_map[l], pool_scale[l], w_pool_o[l],
                                         attn_sinks[l], w_att_o[l], w_out[l], ln_g[l], ln_b[l], rel_bias, alpha)
        ret_p.append(r1); ret_s.append(r2)
        kp_l.append(k1); ks_l.append(k2)
        vp_l.append(v1); vs_l.append(v2)
        pp_l.append(p1); ps_l.append(p2)
    return (xp, xs, jnp.stack(ret_p), jnp.stack(ret_s), jnp.stack(kp_l), jnp.stack(ks_l),
            jnp.stack(vp_l), jnp.stack(vs_l), jnp.stack(pp_l), jnp.stack(ps_l))
```

```python
import functools
import math

import jax
import jax.numpy as jnp
import numpy as np
from jax import lax
from jax.experimental import pallas as pl
from jax.experimental.pallas import tpu as pltpu

D_MODEL = 2048
DEPTH = 2
PAST_LEN = 8192
H_A, DK_A, DV_A = 4, 128, 256
W_A = H_A * DV_A
RET_CHUNK = 128
ROPE_BASE = 10000.0
N_POOL_GROUPS = 4
W_B = 1024
GW_B = W_B // N_POOL_GROUPS
POOL_WINDOWS = (2, 4, 8, 16)
POOL_BUF = 15
POOL_HALO = 16
H_C, KV_C, HD_C = 16, 4, 64
G_C = H_C // KV_C
W_C = H_C * HD_C
KVW = KV_C * HD_C
WINDOW = 128
ATT_BLOCK = 128
NUM_BUCKETS = 32
MAX_DISTANCE = 128
LN_EPS = 1e-5
RMS_EPS = 1e-6

OFF_QA, OFF_KA, OFF_VA, OFF_GA = 0, 512, 1024, 2048
OFF_UB, OFF_GB = 3072, 4096
OFF_QC, OFF_KC, OFF_VC, OFF_GC = 5120, 6144, 6400, 6656
N_MIX = 7680
N_GATE = 3 * D_MODEL

F32 = jnp.float32
BF16 = jnp.bfloat16
VMEM_LIMIT = 56 * 1024 * 1024


def _sigmoid(x):
    return 1.0 / (1.0 + jnp.exp(-x))


def _silu(x):
    return x * _sigmoid(x)


def _params(*sem):
    return pltpu.CompilerParams(dimension_semantics=sem, vmem_limit_bytes=VMEM_LIMIT)


def _matmul_kernel(x_ref, w_ref, o_ref):
    o_ref[...] = jnp.dot(x_ref[...], w_ref[...], preferred_element_type=F32)


def _in_proj(x, w, *, tm=512, tn=1536, name):
    m, k = x.shape
    n = w.shape[1]
    return pl.pallas_call(
        _matmul_kernel,
        out_shape=jax.ShapeDtypeStruct((m, n), F32),
        grid=(n // tn, m // tm),
        in_specs=[pl.BlockSpec((tm, k), lambda j, i: (i, 0)),
                  pl.BlockSpec((k, tn), lambda j, i: (0, j))],
        out_specs=pl.BlockSpec((tm, tn), lambda j, i: (i, j)),
        compiler_params=_params("parallel", "parallel"),
        name=name,
    )(x, w)


def _retention_kernel(*refs, chunk, has_state):
    if has_state:
        (q_ref, k_ref, v_ref, g_ref, cos_ref, sin_ref, dmask_ref, qdec_ref, kdec_ref, cdec_ref,
         s0_ref, z_ref, sout_ref, s_scr) = refs
    else:
        (q_ref, k_ref, v_ref, g_ref, cos_ref, sin_ref, dmask_ref, qdec_ref, kdec_ref, cdec_ref,
         z_ref, sout_ref, s_scr) = refs
    c = pl.program_id(1)

    @pl.when(c == 0)
    def _():
        if has_state:
            s_scr[...] = s0_ref[0]
        else:
            s_scr[...] = jnp.zeros_like(s_scr)

    cos = cos_ref[...]
    sin = sin_ref[...]
    for hd in range(H_A):
        q = q_ref[:, hd * DK_A:(hd + 1) * DK_A]
        k = k_ref[:, hd * DK_A:(hd + 1) * DK_A]
        v = v_ref[:, hd * DV_A:(hd + 1) * DV_A].astype(BF16)
        qr = q * cos + pltpu.roll(q, DK_A // 2, 1) * sin
        kr = (k * cos + pltpu.roll(k, DK_A // 2, 1) * sin) * (DK_A ** -0.5)
        qb = qr.astype(BF16)
        s = s_scr[hd]
        sc = lax.dot_general(qb, kr.astype(BF16), (((1,), (1,)), ((), ())),
                             preferred_element_type=F32) * dmask_ref[hd]
        o = (jnp.dot(sc.astype(BF16), v, preferred_element_type=F32)
             + jnp.dot(qb, s.astype(BF16), preferred_element_type=F32) * qdec_ref[hd])
        kd = (kr * kdec_ref[hd]).astype(BF16)
        s_scr[hd] = s * cdec_ref[hd] + lax.dot_general(
            kd, v, (((0,), (0,)), ((), ())), preferred_element_type=F32)
        o = o * lax.rsqrt(jnp.mean(o * o, axis=-1, keepdims=True) + RMS_EPS)
        g = g_ref[:, hd * DV_A:(hd + 1) * DV_A]
        z_ref[:, hd * DV_A:(hd + 1) * DV_A] = (o * _silu(g)).astype(z_ref.dtype)

    @pl.when(c == pl.num_programs(1) - 1)
    def _():
        sout_ref[0] = s_scr[...]


def _retention_tables(pos0, t, chunk):
    half = DK_A // 2
    inv = ROPE_BASE ** (-jnp.arange(half, dtype=F32) / half)
    pos = pos0 + jnp.arange(t, dtype=F32)
    ang = pos[:, None] * inv[None, :]
    cos, sin = jnp.cos(ang), jnp.sin(ang)
    cos2 = jnp.concatenate([cos, cos], -1)
    sin2 = jnp.concatenate([-sin, sin], -1)
    lg = jnp.log1p(-jnp.exp2(-5.0 - jnp.arange(H_A, dtype=F32)))
    idx = jnp.arange(chunk, dtype=F32)
    diff = idx[:, None] - idx[None, :]
    dmask = jnp.where(diff >= 0, jnp.exp(lg[:, None, None] * jnp.maximum(diff, 0.0)), 0.0)
    qdec = jnp.exp(lg[:, None] * (idx[None, :] + 1.0))
    kdec = jnp.exp(lg[:, None] * (chunk - 1.0 - idx[None, :]))
    cdec = jnp.exp(lg * chunk)
    qdec = jnp.broadcast_to(qdec[:, :, None], (H_A, chunk, DV_A))
    kdec = jnp.broadcast_to(kdec[:, :, None], (H_A, chunk, DK_A))
    cdec = jnp.broadcast_to(cdec[:, None, None], (H_A, 1, DV_A))
    return cos2, sin2, dmask, qdec, kdec, cdec


def _retention(h_mix, batch, t, pos0, state, layer, *, name):
    chunk = min(RET_CHUNK, t)
    n = t // chunk
    has_state = state is not None
    tables = _retention_tables(pos0, t, chunk)
    row = lambda b, c: b * n + c
    in_specs = [
        pl.BlockSpec((chunk, H_A * DK_A), lambda b, c: (row(b, c), OFF_QA // (H_A * DK_A))),
        pl.BlockSpec((chunk, H_A * DK_A), lambda b, c: (row(b, c), OFF_KA // (H_A * DK_A))),
        pl.BlockSpec((chunk, W_A), lambda b, c: (row(b, c), OFF_VA // W_A)),
        pl.BlockSpec((chunk, W_A), lambda b, c: (row(b, c), OFF_GA // W_A)),
        pl.BlockSpec((chunk, DK_A), lambda b, c: (c, 0)),
        pl.BlockSpec((chunk, DK_A), lambda b, c: (c, 0)),
        pl.BlockSpec((H_A, chunk, chunk), lambda b, c: (0, 0, 0)),
        pl.BlockSpec((H_A, chunk, DV_A), lambda b, c: (0, 0, 0)),
        pl.BlockSpec((H_A, chunk, DK_A), lambda b, c: (0, 0, 0)),
        pl.BlockSpec((H_A, 1, DV_A), lambda b, c: (0, 0, 0)),
    ]
    args = [h_mix, h_mix, h_mix, h_mix, *tables]
    if has_state:
        in_specs.append(pl.BlockSpec((None, 1, H_A, DK_A, DV_A), lambda b, c: (layer, b, 0, 0, 0)))
        args.append(state)
    z, s_out = pl.pallas_call(
        functools.partial(_retention_kernel, chunk=chunk, has_state=has_state),
        out_shape=(jax.ShapeDtypeStruct((batch * t, W_A), BF16),
                   jax.ShapeDtypeStruct((batch, H_A, DK_A, DV_A), F32)),
        grid=(batch, n),
        in_specs=in_specs,
        out_specs=(pl.BlockSpec((chunk, W_A), lambda b, c: (row(b, c), 0)),
                   pl.BlockSpec((1, H_A, DK_A, DV_A), lambda b, c: (b, 0, 0, 0))),
        scratch_shapes=[pltpu.VMEM((H_A, DK_A, DV_A), F32)],
        compiler_params=_params("parallel", "arbitrary"),
        name=name,
    )(*args)
    return z, s_out


def _pool_kernel(*refs, nb, tb, pos0, has_state):
    if has_state:
        u_ref, g_ref, st_ref, wmap_ref, scale_ref, z_ref, pout_ref, ext_scr, p_scr = refs
    else:
        u_ref, g_ref, halo_ref, wmap_ref, scale_ref, z_ref, pout_ref, ext_scr, p_scr = refs
    ti = pl.program_id(1)
    for bi in range(nb):
        rows = pl.ds(bi * tb, tb)
        if has_state:
            ext_scr[0:1, :] = jnp.zeros((1, W_B), F32)
            ext_scr[1:POOL_HALO, :] = st_ref[bi]
        else:
            ext_scr[0:POOL_HALO, :] = jnp.where(ti == 0, 0.0, halo_ref[...])
        ext_scr[POOL_HALO:POOL_HALO + tb, :] = u_ref[rows, :]
        t_idx = pos0 + ti * tb + lax.broadcasted_iota(jnp.int32, (tb, GW_B), 0)
        for gi, w in enumerate(POOL_WINDOWS):
            cols = slice(gi * GW_B, (gi + 1) * GW_B)
            acc = ext_scr[POOL_HALO:POOL_HALO + tb, cols]
            u = acc
            for i in range(1, w):
                acc = acc + ext_scr[POOL_HALO - i:POOL_HALO - i + tb, cols]
            cnt = jnp.minimum(t_idx + 1, w).astype(F32)
            p_scr[rows, cols] = (acc / cnt - u).astype(p_scr.dtype)

        @pl.when(ti == pl.num_programs(1) - 1)
        def _():
            pout_ref[bi] = ext_scr[tb + 1:tb + POOL_HALO, :]

    for gi in range(N_POOL_GROUPS):
        cols = slice(gi * GW_B, (gi + 1) * GW_B)
        pm = jnp.dot(p_scr[:, cols].astype(BF16), wmap_ref[gi], preferred_element_type=F32)
        pm = pm * scale_ref[:, cols]
        z_ref[:, cols] = (pm * _silu(g_ref[:, cols])).astype(z_ref.dtype)


def _pool(h_mix, batch, t, pos0, state, layer, wmap, scale, *, nb, tb, name):
    has_state = state is not None
    nt = t // tb
    assert nb == 1 or nt == 1
    rows = nb * tb
    rblk = lambda b, i: b * nt + i
    in_specs = [pl.BlockSpec((rows, W_B), lambda b, i: (rblk(b, i), OFF_UB // W_B)),
                pl.BlockSpec((rows, W_B), lambda b, i: (rblk(b, i), OFF_GB // W_B))]
    args = [h_mix, h_mix]
    if has_state:
        in_specs.append(pl.BlockSpec((None, nb, POOL_BUF, W_B), lambda b, i: (layer, b, 0, 0)))
        args.append(state)
    else:
        per = tb // POOL_HALO
        in_specs.append(pl.BlockSpec(
            (POOL_HALO, W_B), lambda b, i: (jnp.maximum(rblk(b, i) * per - 1, 0), OFF_UB // W_B)))
        args.append(h_mix)
    in_specs += [pl.BlockSpec((N_POOL_GROUPS, GW_B, GW_B), lambda b, i: (0, 0, 0)),
                 pl.BlockSpec((1, W_B), lambda b, i: (0, 0))]
    args += [wmap, scale]
    z, p_out = pl.pallas_call(
        functools.partial(_pool_kernel, nb=nb, tb=tb, pos0=pos0, has_state=has_state),
        out_shape=(jax.ShapeDtypeStruct((batch * t, W_B), BF16),
                   jax.ShapeDtypeStruct((batch, POOL_BUF, W_B), F32)),
        grid=(batch // nb, nt),
        in_specs=in_specs,
        out_specs=(pl.BlockSpec((rows, W_B), lambda b, i: (rblk(b, i), 0)),
                   pl.BlockSpec((nb, POOL_BUF, W_B), lambda b, i: (b, 0, 0))),
        scratch_shapes=[pltpu.VMEM((POOL_HALO + tb, W_B), F32),
                        pltpu.VMEM((rows, W_B), F32)],
        compiler_params=_params("parallel", "arbitrary"),
        name=name,
    )(*args)
    return z, p_out


def _t5_bucket_table(bq):
    span = WINDOW + bq
    dist = np.arange(bq)[:, None] + WINDOW - np.arange(span)[None, :]
    max_exact = NUM_BUCKETS // 2
    d = np.maximum(dist, 0).astype(np.float32)
    large = max_exact + (np.log(np.maximum(d, np.float32(1.0)) / np.float32(max_exact))
                         / np.float32(math.log(MAX_DISTANCE / max_exact))
                         * np.float32(NUM_BUCKETS - max_exact)).astype(np.int32)
    large = np.minimum(large, NUM_BUCKETS - 1)
    bucket = np.where(dist < max_exact, np.maximum(dist, 0), large)
    valid = (dist >= 0) & (dist < WINDOW)
    return np.where(valid, bucket, -1).astype(np.int32)


def _attention_kernel(sinks_ref, relb_ref, q_ref, kc_ref, vc_ref, kp_ref, vp_ref, glo_ref, ghi_ref,
                      bucket_ref, z_ref, wk_ref, wv_ref, bias_scr, *, bq, pos0):
    span = WINDOW + bq
    first = (pl.program_id(0) == 0) & (pl.program_id(1) == 0)
    fmin = jnp.finfo(F32).min

    @pl.when(first)
    def _():
        bucket = bucket_ref[...]
        for h in range(H_C):
            acc = jnp.where(bucket < 0, fmin, 0.0)
            for b in range(NUM_BUCKETS):
                acc = jnp.where(bucket == b, relb_ref[b, h], acc)
            bias_scr[h] = acc

    n = pl.program_id(1)
    kpos = pos0 - WINDOW + n * bq + lax.broadcasted_iota(jnp.int32, (bq, span), 1)
    kvalid = kpos >= 0
    kc = kc_ref[...]
    vc = vc_ref[...]
    kp = kp_ref[...]
    vp = vp_ref[...]
    q = q_ref[...] * (HD_C ** -0.5)
    for kk in range(KV_C):
        ks = slice(kk * HD_C, (kk + 1) * HD_C)
        kx = jnp.concatenate([kp[:, ks], kc[:, ks]], axis=0).astype(BF16)
        vx = jnp.concatenate([vp[:, ks], vc[:, ks]], axis=0).astype(BF16)
        for g in range(G_C):
            h = kk * G_C + g
            hs = slice(h * HD_C, (h + 1) * HD_C)
            s = lax.dot_general(q[:, hs].astype(BF16), kx, (((1,), (1,)), ((), ())),
                                preferred_element_type=F32)
            bias = bias_scr[h]
            s = jnp.where(kvalid & (bias > 0.5 * fmin), s + bias, fmin)
            sink = sinks_ref[h]
            m = jnp.maximum(jnp.max(s, axis=-1, keepdims=True), sink)
            p = jnp.exp(s - m)
            l = jnp.sum(p, axis=-1, keepdims=True) + jnp.exp(sink - m)
            o = jnp.dot(p.astype(BF16), vx, preferred_element_type=F32) / l
            gate = (glo_ref if h < H_C // 2 else ghi_ref)[:, (h % (H_C // 2)) * HD_C:
                                                           (h % (H_C // 2) + 1) * HD_C]
            z_ref[:, hs] = (o * _silu(gate)).astype(z_ref.dtype)

    @pl.when(n == pl.num_programs(1) - 1)
    def _():
        if bq == WINDOW:
            wk_ref[0] = kc
            wv_ref[0] = vc
        else:
            wk_ref[0, 0:WINDOW - bq, :] = kp[bq:, :]
            wk_ref[0, WINDOW - bq:, :] = kc
            wv_ref[0, 0:WINDOW - bq, :] = vp[bq:, :]
            wv_ref[0, WINDOW - bq:, :] = vc


def _attention(h_mix, batch, t, pos0, cache_k, cache_v, layer, sinks, rel_bias, *, name):
    bq = min(ATT_BLOCK, t)
    nb = t // bq
    span = WINDOW + bq
    has_cache = cache_k is not None
    row = lambda b, n, *_: b * nb + n
    half = W_C // 2
    in_specs = [
        pl.BlockSpec((bq, W_C), lambda b, n, *_: (row(b, n), OFF_QC // W_C)),
        pl.BlockSpec((bq, KVW), lambda b, n, *_: (row(b, n), OFF_KC // KVW)),
        pl.BlockSpec((bq, KVW), lambda b, n, *_: (row(b, n), OFF_VC // KVW)),
    ]
    args = [h_mix, h_mix, h_mix]
    if has_cache:
        assert nb == 1
        in_specs += [pl.BlockSpec((None, None, WINDOW, KVW), lambda b, n, *_: (layer, b, 0, 0)),
                     pl.BlockSpec((None, None, WINDOW, KVW), lambda b, n, *_: (layer, b, 0, 0))]
        args += [cache_k, cache_v]
    else:
        assert bq == WINDOW
        prev = lambda b, n, *_: jnp.maximum(row(b, n) - 1, 0)
        in_specs += [pl.BlockSpec((WINDOW, KVW), lambda b, n, *_: (prev(b, n), OFF_KC // KVW)),
                     pl.BlockSpec((WINDOW, KVW), lambda b, n, *_: (prev(b, n), OFF_VC // KVW))]
        args += [h_mix, h_mix]
    in_specs += [
        pl.BlockSpec((bq, half), lambda b, n, *_: (row(b, n), OFF_GC // half)),
        pl.BlockSpec((bq, half), lambda b, n, *_: (row(b, n), OFF_GC // half + 1)),
        pl.BlockSpec((bq, span), lambda b, n, *_: (0, 0)),
    ]
    args += [h_mix, h_mix, jnp.asarray(_t5_bucket_table(bq))]
    z, wk, wv = pl.pallas_call(
        functools.partial(_attention_kernel, bq=bq, pos0=pos0),
        out_shape=(jax.ShapeDtypeStruct((batch * t, W_C), BF16),
                   jax.ShapeDtypeStruct((batch, WINDOW, KVW), F32),
                   jax.ShapeDtypeStruct((batch, WINDOW, KVW), F32)),
        grid_spec=pltpu.PrefetchScalarGridSpec(
            num_scalar_prefetch=2,
            grid=(batch, nb),
            in_specs=in_specs,
            out_specs=(pl.BlockSpec((bq, W_C), lambda b, n, *_: (row(b, n), 0)),
                       pl.BlockSpec((1, WINDOW, KVW), lambda b, n, *_: (b, 0, 0)),
                       pl.BlockSpec((1, WINDOW, KVW), lambda b, n, *_: (b, 0, 0))),
            scratch_shapes=[pltpu.VMEM((H_C, bq, span), F32)]),
        compiler_params=_params("arbitrary", "arbitrary"),
        name=name,
    )(sinks, rel_bias, *args)
    return z, wk, wv


def _out_kernel(x_ref, za_ref, zb_ref, zc_ref, ma_ref, mb_ref, mc_ref, wa_ref, wb_ref, wc_ref,
                wo_ref, lng_ref, lnb_ref, y_ref, yb_ref, *, alpha):
    merged = _sigmoid(ma_ref[...]) * jnp.dot(za_ref[...], wa_ref[...], preferred_element_type=F32)
    merged += _sigmoid(mb_ref[...]) * jnp.dot(zb_ref[...], wb_ref[...], preferred_element_type=F32)
    merged += _sigmoid(mc_ref[...]) * jnp.dot(zc_ref[...], wc_ref[...], preferred_element_type=F32)
    y = jnp.dot(merged.astype(BF16), wo_ref[...], preferred_element_type=F32)
    r = alpha * x_ref[...] + y
    mu = jnp.mean(r, axis=-1, keepdims=True)
    d = r - mu
    var = jnp.mean(d * d, axis=-1, keepdims=True)
    out = d * lax.rsqrt(var + LN_EPS) * lng_ref[...] + lnb_ref[...]
    y_ref[...] = out
    yb_ref[...] = out.astype(BF16)


def _out_stage(x, za, zb, zc, h_gate, wa, wb, wc, wo, lng, lnb, alpha, *, tm=256, name):
    m = x.shape[0]
    rowblk = lambda w: pl.BlockSpec((tm, w), lambda i: (i, 0))
    gate = lambda j: pl.BlockSpec((tm, D_MODEL), lambda i: (i, j))
    const = lambda a: pl.BlockSpec(a.shape, lambda i: (0,) * a.ndim, pipeline_mode=pl.Buffered(1))
    return pl.pallas_call(
        functools.partial(_out_kernel, alpha=alpha),
        out_shape=(jax.ShapeDtypeStruct((m, D_MODEL), F32),
                   jax.ShapeDtypeStruct((m, D_MODEL), BF16)),
        grid=(m // tm,),
        in_specs=[rowblk(D_MODEL), rowblk(W_A), rowblk(W_B), rowblk(W_C),
                  gate(0), gate(1), gate(2),
                  const(wa), const(wb), const(wc), const(wo), const(lng), const(lnb)],
        out_specs=(rowblk(D_MODEL), rowblk(D_MODEL)),
        compiler_params=_params("parallel"),
        name=name,
    )(x, za, zb, zc, h_gate, h_gate, h_gate, wa, wb, wc, wo, lng, lnb)


def _mixer_layer(x, xb, batch, t, pos0, states, layer, w, alpha, tag):
    st_ret, st_k, st_v, st_pool = states if states is not None else (None,) * 4
    h_mix = _in_proj(xb, w["w_mix"], name=f"in_proj_mix_{tag}")
    h_gate = _in_proj(xb, w["w_gate"], name=f"in_proj_gate_{tag}")
    za, s_new = _retention(h_mix, batch, t, pos0, st_ret, layer, name=f"retention_{tag}")
    if t >= 512:
        nb, tb = 1, 512
    else:
        nb, tb = 16, t
    zb, p_new = _pool(h_mix, batch, t, pos0, st_pool, layer, w["w_pool_map"], w["pool_scale"],
                      nb=nb, tb=tb, name=f"pool_{tag}")
    zc, k_new, v_new = _attention(h_mix, batch, t, pos0, st_k, st_v, layer, w["sinks"],
                                  w["rel_bias"], name=f"attention_{tag}")
    y, yb = _out_stage(x, za, zb, zc, h_gate, w["w_ret_o"], w["w_pool_o"], w["w_att_o"], w["w_out"],
                       w["ln_g"], w["ln_b"], alpha, name=f"out_stage_{tag}")
    return y, yb, s_new, k_new, v_new, p_new


def kernel(x_prompt, x_sample, state_ret, cache_win_k, cache_win_v, state_pool, w_in, w_ret_o,
           w_pool_map, pool_scale, w_pool_o, attn_sinks, w_att_o, w_out, ln_g, ln_b, rel_bias):
    alpha = (2.0 * DEPTH) ** 0.25
    bp, tp, _ = x_prompt.shape
    bs, ts, _ = x_sample.shape
    xp = x_prompt.reshape(bp * tp, D_MODEL)
    xs = x_sample.reshape(bs * ts, D_MODEL)
    xpb, xsb = xp.astype(BF16), xs.astype(BF16)
    ck = cache_win_k.reshape(DEPTH, bs, WINDOW, KVW)
    cv = cache_win_v.reshape(DEPTH, bs, WINDOW, KVW)
    sample_states = (state_ret, ck, cv, state_pool)
    outs_p, outs_s = [], []
    for l in range(DEPTH):
        w = {
            "w_mix": w_in[l, :, :N_MIX].astype(BF16),
            "w_gate": w_in[l, :, N_MIX:].astype(BF16),
            "w_ret_o": w_ret_o[l].astype(BF16),
            "w_pool_map": w_pool_map[l].astype(BF16),
            "pool_scale": pool_scale[l].reshape(1, W_B),
            "w_pool_o": w_pool_o[l].astype(BF16),
            "sinks": attn_sinks[l],
            "w_att_o": w_att_o[l].astype(BF16),
            "w_out": w_out[l].astype(BF16),
            "ln_g": ln_g[l].reshape(1, D_MODEL),
            "ln_b": ln_b[l].reshape(1, D_MODEL),
            "rel_bias": rel_bias,
        }
        xp, xpb, *op = _mixer_layer(xp, xpb, bp, tp, 0, None, l, w, alpha, f"p{l}")
        xs, xsb, *os_ = _mixer_layer(xs, xsb, bs, ts, PAST_LEN, sample_states, l, w, alpha, f"s{l}")
        outs_p.append(op)
        outs_s.append(os_)
    stack = lambda outs, i: jnp.stack([o[i] for o in outs])
    ret_p, ret_s = stack(outs_p, 0), stack(outs_s, 0)
    kp = stack(outs_p, 1).reshape(DEPTH, bp, WINDOW, KV_C, HD_C)
    ks = stack(outs_s, 1).reshape(DEPTH, bs, WINDOW, KV_C, HD_C)
    vp = stack(outs_p, 2).reshape(DEPTH, bp, WINDOW, KV_C, HD_C)
    vs = stack(outs_s, 2).reshape(DEPTH, bs, WINDOW, KV_C, HD_C)
    pp, ps = stack(outs_p, 3), stack(outs_s, 3)
    return (xp.reshape(bp, tp, D_MODEL), xs.reshape(bs, ts, D_MODEL), ret_p, ret_s,
            kp, ks, vp, vs, pp, ps)
```

```python
import functools
import math

import jax
import jax.numpy as jnp
import numpy as np
from jax import lax
from jax.experimental import pallas as pl
from jax.experimental.pallas import tpu as pltpu

D_MODEL = 2048
DEPTH = 2
PAST_LEN = 8192
H_A, DK_A, DV_A = 4, 128, 256
W_A = H_A * DV_A
RET_CHUNK = 128
ROPE_BASE = 10000.0
N_POOL_GROUPS = 4
W_B = 1024
GW_B = W_B // N_POOL_GROUPS
POOL_WINDOWS = (2, 4, 8, 16)
POOL_BUF = 15
POOL_HALO = 16
H_C, KV_C, HD_C = 16, 4, 64
G_C = H_C // KV_C
W_C = H_C * HD_C
KVW = KV_C * HD_C
WINDOW = 128
ATT_BLOCK = 128
SOFTMAX_ROWS = 128
DENOM_W = 2 * HD_C
NUM_BUCKETS = 32
MAX_DISTANCE = 128
LN_EPS = 1e-5
RMS_EPS = 1e-6

OFF_QA, OFF_KA, OFF_VA, OFF_GA = 0, 512, 1024, 2048
OFF_UB, OFF_GB = 3072, 4096
OFF_QC, OFF_GC, OFF_KC, OFF_VC = 5120, 6144, 7168, 7424
N_MIX = 7680
N_GATE = 3 * D_MODEL
REF_QC, REF_KC, REF_VC, REF_GC = 5120, 6144, 6400, 6656

F32 = jnp.float32
BF16 = jnp.bfloat16
VMEM_LIMIT = 56 * 1024 * 1024


def _sigmoid(x):
    return 1.0 / (1.0 + jnp.exp(-x))


def _silu(x):
    return x * _sigmoid(x)


def _params(*sem):
    return pltpu.CompilerParams(dimension_semantics=sem, vmem_limit_bytes=VMEM_LIMIT)


def _matmul_kernel(x_ref, w_ref, o_ref):
    o_ref[...] = jnp.dot(x_ref[...], w_ref[...], preferred_element_type=F32)


def _in_proj(x, w, *, tm=512, tn=1536, name):
    m, k = x.shape
    n = w.shape[1]
    return pl.pallas_call(
        _matmul_kernel,
        out_shape=jax.ShapeDtypeStruct((m, n), F32),
        grid=(n // tn, m // tm),
        in_specs=[pl.BlockSpec((tm, k), lambda j, i: (i, 0)),
                  pl.BlockSpec((k, tn), lambda j, i: (0, j))],
        out_specs=pl.BlockSpec((tm, tn), lambda j, i: (i, j)),
        compiler_params=_params("parallel", "parallel"),
        name=name,
    )(x, w)


def _retention_kernel(*refs, chunk, has_state):
    if has_state:
        (q_ref, k_ref, v_ref, g_ref, cos_ref, sin_ref, dmask_ref, qdec_ref, kdec_ref, cdec_ref,
         s0_ref, z_ref, sout_ref, s_scr) = refs
    else:
        (q_ref, k_ref, v_ref, g_ref, cos_ref, sin_ref, dmask_ref, qdec_ref, kdec_ref, cdec_ref,
         z_ref, sout_ref, s_scr) = refs
    c = pl.program_id(1)

    @pl.when(c == 0)
    def _():
        if has_state:
            s_scr[...] = s0_ref[0]
        else:
            s_scr[...] = jnp.zeros_like(s_scr)

    cos = cos_ref[...]
    sin = sin_ref[...]
    for hd in range(H_A):
        q = q_ref[:, hd * DK_A:(hd + 1) * DK_A]
        k = k_ref[:, hd * DK_A:(hd + 1) * DK_A]
        v = v_ref[:, hd * DV_A:(hd + 1) * DV_A].astype(BF16)
        qr = q * cos + pltpu.roll(q, DK_A // 2, 1) * sin
        kr = (k * cos + pltpu.roll(k, DK_A // 2, 1) * sin) * (DK_A ** -0.5)
        qb = qr.astype(BF16)
        s = s_scr[hd]
        sc = lax.dot_general(qb, kr.astype(BF16), (((1,), (1,)), ((), ())),
                             preferred_element_type=F32) * dmask_ref[hd]
        o = (jnp.dot(sc.astype(BF16), v, preferred_element_type=F32)
             + jnp.dot(qb, s.astype(BF16), preferred_element_type=F32) * qdec_ref[hd])
        kd = (kr * kdec_ref[hd]).astype(BF16)
        s_scr[hd] = s * cdec_ref[hd] + lax.dot_general(
            kd, v, (((0,), (0,)), ((), ())), preferred_element_type=F32)
        o = o * lax.rsqrt(jnp.mean(o * o, axis=-1, keepdims=True) + RMS_EPS)
        g = g_ref[:, hd * DV_A:(hd + 1) * DV_A]
        z_ref[:, hd * DV_A:(hd + 1) * DV_A] = (o * _silu(g)).astype(z_ref.dtype)

    @pl.when(c == pl.num_programs(1) - 1)
    def _():
        sout_ref[0] = s_scr[...]


def _retention_tables(pos0, t, chunk):
    half = DK_A // 2
    inv = ROPE_BASE ** (-jnp.arange(half, dtype=F32) / half)
    pos = pos0 + jnp.arange(t, dtype=F32)
    ang = pos[:, None] * inv[None, :]
    cos, sin = jnp.cos(ang), jnp.sin(ang)
    cos2 = jnp.concatenate([cos, cos], -1)
    sin2 = jnp.concatenate([-sin, sin], -1)
    lg = jnp.log1p(-jnp.exp2(-5.0 - jnp.arange(H_A, dtype=F32)))
    idx = jnp.arange(chunk, dtype=F32)
    diff = idx[:, None] - idx[None, :]
    dmask = jnp.where(diff >= 0, jnp.exp(lg[:, None, None] * jnp.maximum(diff, 0.0)), 0.0)
    qdec = jnp.exp(lg[:, None] * (idx[None, :] + 1.0))
    kdec = jnp.exp(lg[:, None] * (chunk - 1.0 - idx[None, :]))
    cdec = jnp.exp(lg * chunk)
    qdec = jnp.broadcast_to(qdec[:, :, None], (H_A, chunk, DV_A))
    kdec = jnp.broadcast_to(kdec[:, :, None], (H_A, chunk, DK_A))
    cdec = jnp.broadcast_to(cdec[:, None, None], (H_A, 1, DV_A))
    return cos2, sin2, dmask, qdec, kdec, cdec


def _retention(h_mix, batch, t, pos0, state, layer, *, name):
    chunk = min(RET_CHUNK, t)
    n = t // chunk
    has_state = state is not None
    tables = _retention_tables(pos0, t, chunk)
    row = lambda b, c: b * n + c
    in_specs = [
        pl.BlockSpec((chunk, H_A * DK_A), lambda b, c: (row(b, c), OFF_QA // (H_A * DK_A))),
        pl.BlockSpec((chunk, H_A * DK_A), lambda b, c: (row(b, c), OFF_KA // (H_A * DK_A))),
        pl.BlockSpec((chunk, W_A), lambda b, c: (row(b, c), OFF_VA // W_A)),
        pl.BlockSpec((chunk, W_A), lambda b, c: (row(b, c), OFF_GA // W_A)),
        pl.BlockSpec((chunk, DK_A), lambda b, c: (c, 0)),
        pl.BlockSpec((chunk, DK_A), lambda b, c: (c, 0)),
        pl.BlockSpec((H_A, chunk, chunk), lambda b, c: (0, 0, 0)),
        pl.BlockSpec((H_A, chunk, DV_A), lambda b, c: (0, 0, 0)),
        pl.BlockSpec((H_A, chunk, DK_A), lambda b, c: (0, 0, 0)),
        pl.BlockSpec((H_A, 1, DV_A), lambda b, c: (0, 0, 0)),
    ]
    args = [h_mix, h_mix, h_mix, h_mix, *tables]
    if has_state:
        in_specs.append(pl.BlockSpec((None, 1, H_A, DK_A, DV_A), lambda b, c: (layer, b, 0, 0, 0)))
        args.append(state)
    z, s_out = pl.pallas_call(
        functools.partial(_retention_kernel, chunk=chunk, has_state=has_state),
        out_shape=(jax.ShapeDtypeStruct((batch * t, W_A), BF16),
                   jax.ShapeDtypeStruct((batch, H_A, DK_A, DV_A), F32)),
        grid=(batch, n),
        in_specs=in_specs,
        out_specs=(pl.BlockSpec((chunk, W_A), lambda b, c: (row(b, c), 0)),
                   pl.BlockSpec((1, H_A, DK_A, DV_A), lambda b, c: (b, 0, 0, 0))),
        scratch_shapes=[pltpu.VMEM((H_A, DK_A, DV_A), F32)],
        compiler_params=_params("parallel", "arbitrary"),
        name=name,
    )(*args)
    return z, s_out


def _pool_kernel(*refs, nb, tb, pos0, has_state):
    if has_state:
        u_ref, g_ref, st_ref, wmap_ref, scale_ref, z_ref, pout_ref, ext_scr, p_scr = refs
    else:
        u_ref, g_ref, halo_ref, wmap_ref, scale_ref, z_ref, pout_ref, ext_scr, p_scr = refs
    ti = pl.program_id(1)
    for bi in range(nb):
        rows = pl.ds(bi * tb, tb)
        if has_state:
            ext_scr[0:1, :] = jnp.zeros((1, W_B), F32)
            ext_scr[1:POOL_HALO, :] = st_ref[bi]
        else:
            ext_scr[0:POOL_HALO, :] = jnp.where(ti == 0, 0.0, halo_ref[...])
        ext_scr[POOL_HALO:POOL_HALO + tb, :] = u_ref[rows, :]
        t_idx = pos0 + ti * tb + lax.broadcasted_iota(jnp.int32, (tb, GW_B), 0)
        for gi, w in enumerate(POOL_WINDOWS):
            cols = slice(gi * GW_B, (gi + 1) * GW_B)
            acc = ext_scr[:, cols]
            shift = 1
            while shift < w:
                acc = acc + pltpu.roll(acc, shift, 0)
                shift *= 2
            cnt = jnp.minimum(t_idx + 1, w).astype(F32)
            u = ext_scr[POOL_HALO:POOL_HALO + tb, cols]
            p_scr[rows, cols] = acc[POOL_HALO:POOL_HALO + tb] / cnt - u

        @pl.when(ti == pl.num_programs(1) - 1)
        def _():
            pout_ref[bi] = ext_scr[tb + 1:tb + POOL_HALO, :]

    for gi in range(N_POOL_GROUPS):
        cols = slice(gi * GW_B, (gi + 1) * GW_B)
        pm = jnp.dot(p_scr[:, cols].astype(BF16), wmap_ref[gi], preferred_element_type=F32)
        pm = pm * scale_ref[:, cols]
        z_ref[:, cols] = (pm * _silu(g_ref[:, cols])).astype(z_ref.dtype)


def _pool(h_mix, batch, t, pos0, state, layer, wmap, scale, *, nb, tb, name):
    has_state = state is not None
    nt = t // tb
    assert nb == 1 or nt == 1
    rows = nb * tb
    rblk = lambda b, i: b * nt + i
    in_specs = [pl.BlockSpec((rows, W_B), lambda b, i: (rblk(b, i), OFF_UB // W_B)),
                pl.BlockSpec((rows, W_B), lambda b, i: (rblk(b, i), OFF_GB // W_B))]
    args = [h_mix, h_mix]
    if has_state:
        in_specs.append(pl.BlockSpec((None, nb, POOL_BUF, W_B), lambda b, i: (layer, b, 0, 0)))
        args.append(state)
    else:
        per = tb // POOL_HALO
        in_specs.append(pl.BlockSpec(
            (POOL_HALO, W_B), lambda b, i: (jnp.maximum(rblk(b, i) * per - 1, 0), OFF_UB // W_B)))
        args.append(h_mix)
    in_specs += [pl.BlockSpec((N_POOL_GROUPS, GW_B, GW_B), lambda b, i: (0, 0, 0)),
                 pl.BlockSpec((1, W_B), lambda b, i: (0, 0))]
    args += [wmap, scale]
    z, p_out = pl.pallas_call(
        functools.partial(_pool_kernel, nb=nb, tb=tb, pos0=pos0, has_state=has_state),
        out_shape=(jax.ShapeDtypeStruct((batch * t, W_B), BF16),
                   jax.ShapeDtypeStruct((batch, POOL_BUF, W_B), F32)),
        grid=(batch // nb, nt),
        in_specs=in_specs,
        out_specs=(pl.BlockSpec((rows, W_B), lambda b, i: (rblk(b, i), 0)),
                   pl.BlockSpec((nb, POOL_BUF, W_B), lambda b, i: (b, 0, 0))),
        scratch_shapes=[pltpu.VMEM((POOL_HALO + tb, W_B), F32),
                        pltpu.VMEM((rows, W_B), F32)],
        compiler_params=_params("parallel", "arbitrary"),
        name=name,
    )(*args)
    return z, p_out


def _t5_bucket_table(bq):
    span = WINDOW + bq
    dist = np.arange(bq)[:, None] + WINDOW - np.arange(span)[None, :]
    max_exact = NUM_BUCKETS // 2
    d = np.maximum(dist, 0).astype(np.float32)
    large = max_exact + (np.log(np.maximum(d, np.float32(1.0)) / np.float32(max_exact))
                         / np.float32(math.log(MAX_DISTANCE / max_exact))
                         * np.float32(NUM_BUCKETS - max_exact)).astype(np.int32)
    large = np.minimum(large, NUM_BUCKETS - 1)
    bucket = np.where(dist < max_exact, np.maximum(dist, 0), large)
    valid = (dist >= 0) & (dist < WINDOW)
    return np.where(valid, bucket, -1).astype(np.int32)


def _head_of_row_block(r):
    return (r % KV_C) * G_C + r // KV_C


def _attention_kernel(sinks_ref, relb_ref, q_ref, kc_ref, vc_ref, kp_ref, vp_ref, g_ref, bucket_ref,
                      z_ref, wk_ref, wv_ref, bias_scr, qbd_scr, s_scr, p_scr, o_scr,
                      *, bq, nbat, has_cache, mask_first):
    span = WINDOW + bq
    hrows = H_C * bq
    chunk = min(hrows, SOFTMAX_ROWS)
    fmin = jnp.finfo(F32).min
    n = pl.program_id(1)

    @pl.when((pl.program_id(0) == 0) & (n == 0))
    def _():
        bucket = bucket_ref[...]
        col = lax.broadcasted_iota(jnp.int32, (bq, span), 1)
        for r in range(H_C):
            h = _head_of_row_block(r)
            acc = jnp.where(bucket < 0, fmin, 0.0)
            for b in range(NUM_BUCKETS):
                acc = jnp.where(bucket == b, relb_ref[b, h], acc)
            acc = jnp.where(col == 0, sinks_ref[h], acc)
            bias_scr[0, r * bq:(r + 1) * bq, :] = acc
            if mask_first:
                bias_scr[1, r * bq:(r + 1) * bq, :] = jnp.where((col < WINDOW) & (col > 0), fmin, acc)

    bsel = jnp.where(n == 0, 1, 0) if mask_first else 0
    lane_kv = lax.broadcasted_iota(jnp.int32, (1, KVW), 1) // HD_C
    low_half = lax.broadcasted_iota(jnp.int32, (1, 2 * HD_C), 1) < HD_C
    row0 = lax.broadcasted_iota(jnp.int32, (WINDOW, KVW), 0) == 0
    ones = jnp.ones((span, DENOM_W), BF16)
    for bi in range(nbat):
        rows = slice(bi * bq, (bi + 1) * bq)
        base = bi * hrows
        kc = kc_ref[rows, :]
        vc = vc_ref[rows, :]
        kp = kp_ref[bi] if has_cache else kp_ref[...]
        vp = vp_ref[bi] if has_cache else vp_ref[...]
        kx = jnp.concatenate([jnp.where(row0, 0.0, kp), kc], axis=0).astype(BF16)
        vx = jnp.concatenate([jnp.where(row0, 0.0, vp), vc], axis=0).astype(BF16)
        vx1 = jnp.concatenate([vx, ones], axis=1)
        q = q_ref[rows, :] * (HD_C ** -0.5)
        for g in range(G_C):
            qg = q[:, g * KVW:(g + 1) * KVW]
            qbd_scr[base + g * KV_C * bq:base + (g + 1) * KV_C * bq, :] = jnp.concatenate(
                [jnp.where(lane_kv == kk, qg, 0.0) for kk in range(KV_C)], axis=0).astype(BF16)
        s_scr[base:base + hrows, :] = lax.dot_general(
            qbd_scr[base:base + hrows, :], kx, (((1,), (1,)), ((), ())), preferred_element_type=F32)
        for c in range(hrows // chunk):
            rs = slice(base + c * chunk, base + (c + 1) * chunk)
            s = s_scr[rs, :] + bias_scr[bsel, c * chunk:(c + 1) * chunk, :]
            m = jnp.max(s, axis=-1, keepdims=True)
            p_scr[rs, :] = jnp.exp(s - m).astype(BF16)
        o_scr[base:base + hrows, :] = jnp.dot(p_scr[base:base + hrows, :], vx1,
                                              preferred_element_type=F32)
        for g in range(G_C):
            halves = []
            for half in range(2):
                pair = []
                for kk in (2 * half, 2 * half + 1):
                    rr = slice(base + (g * KV_C + kk) * bq, base + (g * KV_C + kk + 1) * bq)
                    pair.append(o_scr[rr, 2 * half * HD_C:(2 * half + 2) * HD_C]
                                * (1.0 / o_scr[rr, KVW:KVW + DENOM_W]))
                halves.append(jnp.where(low_half, pair[0], pair[1]))
            og = jnp.concatenate(halves, axis=1)
            cols = slice(g * KVW, (g + 1) * KVW)
            z_ref[rows, cols] = (og * _silu(g_ref[rows, cols])).astype(z_ref.dtype)

        @pl.when(n == pl.num_programs(1) - 1)
        def _():
            if bq == WINDOW:
                wk_ref[bi] = kc
                wv_ref[bi] = vc
            else:
                wk_ref[bi, 0:WINDOW - bq, :] = kp[bq:, :]
                wk_ref[bi, WINDOW - bq:, :] = kc
                wv_ref[bi, 0:WINDOW - bq, :] = vp[bq:, :]
                wv_ref[bi, WINDOW - bq:, :] = vc


def _attention(h_mix, batch, t, pos0, cache_k, cache_v, layer, sinks, rel_bias, *, nbat, name):
    bq = min(ATT_BLOCK, t)
    nb = t // bq
    span = WINDOW + bq
    has_cache = cache_k is not None
    assert pos0 == 0 or pos0 >= WINDOW
    mask_first = pos0 == 0
    rows = nbat * bq
    assert nbat == 1 or nb == 1
    row = lambda b, n, *_: b * nb + n
    in_specs = [
        pl.BlockSpec((rows, W_C), lambda b, n, *_: (row(b, n), OFF_QC // W_C)),
        pl.BlockSpec((rows, KVW), lambda b, n, *_: (row(b, n), OFF_KC // KVW)),
        pl.BlockSpec((rows, KVW), lambda b, n, *_: (row(b, n), OFF_VC // KVW)),
    ]
    args = [h_mix, h_mix, h_mix]
    if has_cache:
        assert nb == 1
        in_specs += [pl.BlockSpec((None, nbat, WINDOW, KVW), lambda b, n, *_: (layer, b, 0, 0)),
                     pl.BlockSpec((None, nbat, WINDOW, KVW), lambda b, n, *_: (layer, b, 0, 0))]
        args += [cache_k, cache_v]
    else:
        assert bq == WINDOW and nbat == 1
        prev = lambda b, n, *_: jnp.maximum(row(b, n) - 1, 0)
        in_specs += [pl.BlockSpec((WINDOW, KVW), lambda b, n, *_: (prev(b, n), OFF_KC // KVW)),
                     pl.BlockSpec((WINDOW, KVW), lambda b, n, *_: (prev(b, n), OFF_VC // KVW))]
        args += [h_mix, h_mix]
    in_specs += [
        pl.BlockSpec((rows, W_C), lambda b, n, *_: (row(b, n), OFF_GC // W_C)),
        pl.BlockSpec((bq, span), lambda b, n, *_: (0, 0)),
    ]
    args += [h_mix, jnp.asarray(_t5_bucket_table(bq))]
    srows = nbat * H_C * bq
    z, wk, wv = pl.pallas_call(
        functools.partial(_attention_kernel, bq=bq, nbat=nbat, has_cache=has_cache,
                          mask_first=mask_first),
        out_shape=(jax.ShapeDtypeStruct((batch * t, W_C), BF16),
                   jax.ShapeDtypeStruct((batch, WINDOW, KVW), F32),
                   jax.ShapeDtypeStruct((batch, WINDOW, KVW), F32)),
        grid_spec=pltpu.PrefetchScalarGridSpec(
            num_scalar_prefetch=2,
            grid=(batch // nbat, nb),
            in_specs=in_specs,
            out_specs=(pl.BlockSpec((rows, W_C), lambda b, n, *_: (row(b, n), 0)),
                       pl.BlockSpec((nbat, WINDOW, KVW), lambda b, n, *_: (b, 0, 0)),
                       pl.BlockSpec((nbat, WINDOW, KVW), lambda b, n, *_: (b, 0, 0))),
            scratch_shapes=[pltpu.VMEM((2 if mask_first else 1, H_C * bq, span), F32),
                            pltpu.VMEM((srows, KVW), BF16),
                            pltpu.VMEM((srows, span), F32),
                            pltpu.VMEM((srows, span), BF16),
                            pltpu.VMEM((srows, KVW + DENOM_W), F32)]),
        compiler_params=_params("arbitrary", "arbitrary"),
        name=name,
    )(sinks, rel_bias, *args)
    return z, wk, wv


def _out_kernel(x_ref, za_ref, zb_ref, zc_ref, ma_ref, mb_ref, mc_ref, wa_ref, wb_ref, wc_ref,
                wo_ref, lng_ref, lnb_ref, y_ref, yb_ref, *, alpha):
    merged = _sigmoid(ma_ref[...]) * jnp.dot(za_ref[...], wa_ref[...], preferred_element_type=F32)
    merged += _sigmoid(mb_ref[...]) * jnp.dot(zb_ref[...], wb_ref[...], preferred_element_type=F32)
    merged += _sigmoid(mc_ref[...]) * jnp.dot(zc_ref[...], wc_ref[...], preferred_element_type=F32)
    y = jnp.dot(merged.astype(BF16), wo_ref[...], preferred_element_type=F32)
    r = alpha * x_ref[...] + y
    mu = jnp.mean(r, axis=-1, keepdims=True)
    d = r - mu
    var = jnp.mean(d * d, axis=-1, keepdims=True)
    out = d * lax.rsqrt(var + LN_EPS) * lng_ref[...] + lnb_ref[...]
    y_ref[...] = out
    yb_ref[...] = out.astype(BF16)


def _out_stage(x, za, zb, zc, h_gate, wa, wb, wc, wo, lng, lnb, alpha, *, tm=256, name):
    m = x.shape[0]
    rowblk = lambda w: pl.BlockSpec((tm, w), lambda i: (i, 0))
    gate = lambda j: pl.BlockSpec((tm, D_MODEL), lambda i: (i, j))
    const = lambda a: pl.BlockSpec(a.shape, lambda i: (0,) * a.ndim, pipeline_mode=pl.Buffered(1))
    return pl.pallas_call(
        functools.partial(_out_kernel, alpha=alpha),
        out_shape=(jax.ShapeDtypeStruct((m, D_MODEL), F32),
                   jax.ShapeDtypeStruct((m, D_MODEL), BF16)),
        grid=(m // tm,),
        in_specs=[rowblk(D_MODEL), rowblk(W_A), rowblk(W_B), rowblk(W_C),
                  gate(0), gate(1), gate(2),
                  const(wa), const(wb), const(wc), const(wo), const(lng), const(lnb)],
        out_specs=(rowblk(D_MODEL), rowblk(D_MODEL)),
        compiler_params=_params("parallel"),
        name=name,
    )(x, za, zb, zc, h_gate, h_gate, h_gate, wa, wb, wc, wo, lng, lnb)


def _mixer_layer(x, xb, batch, t, pos0, states, layer, w, alpha, tag):
    st_ret, st_k, st_v, st_pool = states if states is not None else (None,) * 4
    h_mix = _in_proj(xb, w["w_mix"], name=f"in_proj_mix_{tag}")
    h_gate = _in_proj(xb, w["w_gate"], name=f"in_proj_gate_{tag}")
    za, s_new = _retention(h_mix, batch, t, pos0, st_ret, layer, name=f"retention_{tag}")
    if t >= 512:
        nb, tb, nbat = 1, 512, 1
    else:
        nb, tb, nbat = 16, t, 8
    zb, p_new = _pool(h_mix, batch, t, pos0, st_pool, layer, w["w_pool_map"], w["pool_scale"],
                      nb=nb, tb=tb, name=f"pool_{tag}")
    zc, k_new, v_new = _attention(h_mix, batch, t, pos0, st_k, st_v, layer, w["sinks"],
                                  w["rel_bias"], nbat=nbat, name=f"attention_{tag}")
    y, yb = _out_stage(x, za, zb, zc, h_gate, w["w_ret_o"], w["w_pool_o"], w["w_att_o"], w["w_out"],
                       w["ln_g"], w["ln_b"], alpha, name=f"out_stage_{tag}")
    return y, yb, s_new, k_new, v_new, p_new


def _regroup_heads(w, axis):
    shp = w.shape
    w = w.reshape(shp[:axis] + (KV_C, G_C, HD_C) + shp[axis + 1:])
    w = jnp.swapaxes(w, axis, axis + 1)
    return w.reshape(shp)


def _prep_w_mix(wi):
    return jnp.concatenate([
        wi[:, :REF_QC],
        _regroup_heads(wi[:, REF_QC:REF_QC + W_C], 1),
        _regroup_heads(wi[:, REF_GC:REF_GC + W_C], 1),
        wi[:, REF_KC:REF_KC + KVW],
        wi[:, REF_VC:REF_VC + KVW]], axis=1).astype(BF16)


def kernel(x_prompt, x_sample, state_ret, cache_win_k, cache_win_v, state_pool, w_in, w_ret_o,
           w_pool_map, pool_scale, w_pool_o, attn_sinks, w_att_o, w_out, ln_g, ln_b, rel_bias):
    alpha = (2.0 * DEPTH) ** 0.25
    bp, tp, _ = x_prompt.shape
    bs, ts, _ = x_sample.shape
    xp = x_prompt.reshape(bp * tp, D_MODEL)
    xs = x_sample.reshape(bs * ts, D_MODEL)
    xpb, xsb = xp.astype(BF16), xs.astype(BF16)
    ck = cache_win_k.reshape(DEPTH, bs, WINDOW, KVW)
    cv = cache_win_v.reshape(DEPTH, bs, WINDOW, KVW)
    sample_states = (state_ret, ck, cv, state_pool)
    outs_p, outs_s = [], []
    for l in range(DEPTH):
        w = {
            "w_mix": _prep_w_mix(w_in[l]),
            "w_gate": w_in[l, :, N_MIX:].astype(BF16),
            "w_ret_o": w_ret_o[l].astype(BF16),
            "w_pool_map": w_pool_map[l].astype(BF16),
            "pool_scale": pool_scale[l].reshape(1, W_B),
            "w_pool_o": w_pool_o[l].astype(BF16),
            "sinks": attn_sinks[l],
            "w_att_o": _regroup_heads(w_att_o[l], 0).astype(BF16),
            "w_out": w_out[l].astype(BF16),
            "ln_g": ln_g[l].reshape(1, D_MODEL),
            "ln_b": ln_b[l].reshape(1, D_MODEL),
            "rel_bias": rel_bias,
        }
        xp, xpb, *op = _mixer_layer(xp, xpb, bp, tp, 0, None, l, w, alpha, f"p{l}")
        xs, xsb, *os_ = _mixer_layer(xs, xsb, bs, ts, PAST_LEN, sample_states, l, w, alpha, f"s{l}")
        outs_p.append(op)
        outs_s.append(os_)
    stack = lambda outs, i: jnp.stack([o[i] for o in outs])
    ret_p, ret_s = stack(outs_p, 0), stack(outs_s, 0)
    kp = stack(outs_p, 1).reshape(DEPTH, bp, WINDOW, KV_C, HD_C)
    ks = stack(outs_s, 1).reshape(DEPTH, bs, WINDOW, KV_C, HD_C)
    vp = stack(outs_p, 2).reshape(DEPTH, bp, WINDOW, KV_C, HD_C)
    vs = stack(outs_s, 2).reshape(DEPTH, bs, WINDOW, KV_C, HD_C)
    pp, ps = stack(outs_p, 3), stack(outs_s, 3)
    return (xp.reshape(bp, tp, D_MODEL), xs.reshape(bs, ts, D_MODEL), ret_p, ret_s,
            kp, ks, vp, vs, pp, ps)
```

```python
import functools
import math

import jax
import jax.numpy as jnp
import numpy as np
from jax import lax
from jax.experimental import pallas as pl
from jax.experimental.pallas import tpu as pltpu

D_MODEL = 2048
DEPTH = 2
PAST_LEN = 8192
H_A, DK_A, DV_A = 4, 128, 256
W_A = H_A * DV_A
RET_CHUNK = 128
ROPE_BASE = 10000.0
N_POOL_GROUPS = 4
W_B = 1024
GW_B = W_B // N_POOL_GROUPS
POOL_WINDOWS = (2, 4, 8, 16)
POOL_BUF = 15
POOL_HALO = 16
H_C, KV_C, HD_C = 16, 4, 64
G_C = H_C // KV_C
W_C = H_C * HD_C
KVW = KV_C * HD_C
WINDOW = 128
ATT_BLOCK = 128
SOFTMAX_ROWS = 128
DENOM_W = 2 * HD_C
NUM_BUCKETS = 32
MAX_DISTANCE = 128
LN_EPS = 1e-5
RMS_EPS = 1e-6

REF_QC, REF_KC, REF_VC, REF_GC = 5120, 6144, 6400, 6656
N_AB = 5120
N_ATT = 2560
REF_GATE = 7680
N_GATE = 3 * D_MODEL
OFF_QA, OFF_KA, OFF_VA, OFF_GA, OFF_UB, OFF_GB = 0, 512, 1024, 2048, 3072, 4096
OFF_QC, OFF_GC, OFF_KC, OFF_VC = 0, 1024, 2048, 2304

F32 = jnp.float32
BF16 = jnp.bfloat16
VMEM_LIMIT = 56 * 1024 * 1024

IN_PROJ_TM = 512
IN_PROJ_TN = {N_AB: 1280, N_ATT: 2560, N_GATE: 1536}
OUT_STAGE_TM = 256
POOL_ROWS = 512
SMALL_T_POOL_BATCH = 16
SMALL_T_ATT_BATCH = 8
RET_ROWS = 256
SMALL_T_RET_BATCH = 4


def _sigmoid(x):
    return 1.0 / (1.0 + jnp.exp(-x))


def _silu(x):
    return x * _sigmoid(x)


def _params(*sem):
    return pltpu.CompilerParams(dimension_semantics=sem, vmem_limit_bytes=VMEM_LIMIT)


def _alias_args(prev_outs, n_in):
    if prev_outs is None:
        return [], [], {}
    specs = [pl.BlockSpec(memory_space=pl.ANY) for _ in prev_outs]
    return list(prev_outs), specs, {n_in + i: out_idx for i, (out_idx) in enumerate(range(1, 1 + len(prev_outs)))}


def _matmul_kernel(x_ref, w_ref, o_ref, *rest, cast_w, emit_xb):
    rest = list(rest)
    xb_ref = rest.pop(0) if emit_xb else None
    if cast_w:
        wb_scr = rest.pop(0)

        @pl.when(pl.program_id(1) == 0)
        def _():
            wb_scr[...] = w_ref[...].astype(BF16)
        w = wb_scr[...]
    else:
        w = w_ref[...]
    x = x_ref[...].astype(BF16)
    if emit_xb:
        xb_ref[...] = x
    o_ref[...] = jnp.dot(x, w, preferred_element_type=F32)


def _in_proj(x, w, n, *, layer=None, col0=0, emit_xb=False, name):
    m, k = x.shape
    tm, tn = IN_PROJ_TM, IN_PROJ_TN[n]
    cast_w = w.ndim == 3
    if cast_w:
        w_spec = pl.BlockSpec((None, k, tn), lambda j, i: (layer, 0, col0 // tn + j))
    else:
        w_spec = pl.BlockSpec((k, tn), lambda j, i: (0, j))
    assert not emit_xb or n == tn
    out_shape = [jax.ShapeDtypeStruct((m, n), F32)]
    out_specs = [pl.BlockSpec((tm, tn), lambda j, i: (i, j))]
    if emit_xb:
        out_shape.append(jax.ShapeDtypeStruct((m, k), BF16))
        out_specs.append(pl.BlockSpec((tm, k), lambda j, i: (i, 0)))
    outs = pl.pallas_call(
        functools.partial(_matmul_kernel, cast_w=cast_w, emit_xb=emit_xb),
        out_shape=out_shape,
        grid=(n // tn, m // tm),
        in_specs=[pl.BlockSpec((tm, k), lambda j, i: (i, 0)), w_spec],
        out_specs=out_specs,
        scratch_shapes=[pltpu.VMEM((k, tn), BF16)] if cast_w else [],
        compiler_params=_params("arbitrary", "arbitrary"),
        name=name,
    )(x, w)
    return outs if emit_xb else outs[0]


def _retention_kernel(*refs, chunk, nchunk, nbat, has_state, has_prev):
    refs = list(refs)
    (q_ref, k_ref, v_ref, g_ref, cos_ref, sin_ref, dmask_ref, qdec_ref, kdec_ref, cdec_ref) = refs[:10]
    refs = refs[10:]
    s0_ref = refs.pop(0) if has_state else None
    if has_prev:
        refs.pop(0)
    z_ref, sout_ref, s_scr = refs
    c = pl.program_id(1)

    @pl.when(c == 0)
    def _():
        if has_state:
            s_scr[...] = s0_ref[...]
        else:
            s_scr[...] = jnp.zeros_like(s_scr)

    for bi in range(nbat):
        for hd in range(H_A):
            s = s_scr[bi, hd]
            for ci in range(nchunk):
                rows = slice((bi * nchunk + ci) * chunk, (bi * nchunk + ci + 1) * chunk)
                trow = slice(ci * chunk, (ci + 1) * chunk)
                cos = cos_ref[trow, :]
                sin = sin_ref[trow, :]
                q = q_ref[rows, hd * DK_A:(hd + 1) * DK_A]
                k = k_ref[rows, hd * DK_A:(hd + 1) * DK_A]
                v = v_ref[rows, hd * DV_A:(hd + 1) * DV_A].astype(BF16)
                qr = q * cos + pltpu.roll(q, DK_A // 2, 1) * sin
                kr = (k * cos + pltpu.roll(k, DK_A // 2, 1) * sin) * (DK_A ** -0.5)
                qb = qr.astype(BF16)
                sc = lax.dot_general(qb, kr.astype(BF16), (((1,), (1,)), ((), ())),
                                     preferred_element_type=F32) * dmask_ref[hd]
                o = (jnp.dot(sc.astype(BF16), v, preferred_element_type=F32)
                     + jnp.dot(qb, s.astype(BF16), preferred_element_type=F32) * qdec_ref[hd])
                kd = (kr * kdec_ref[hd]).astype(BF16)
                s = s * cdec_ref[hd] + lax.dot_general(
                    kd, v, (((0,), (0,)), ((), ())), preferred_element_type=F32)
                o = o * lax.rsqrt(jnp.mean(o * o, axis=-1, keepdims=True) + RMS_EPS)
                g = g_ref[rows, hd * DV_A:(hd + 1) * DV_A]
                z_ref[rows, hd * DV_A:(hd + 1) * DV_A] = (o * _silu(g)).astype(z_ref.dtype)
            s_scr[bi, hd] = s

    @pl.when(c == pl.num_programs(1) - 1)
    def _():
        sout_ref[...] = s_scr[...]


def _retention_tables(pos0, t, chunk):
    half = DK_A // 2
    inv = ROPE_BASE ** (-jnp.arange(half, dtype=F32) / half)
    pos = pos0 + jnp.arange(t, dtype=F32)
    ang = pos[:, None] * inv[None, :]
    cos, sin = jnp.cos(ang), jnp.sin(ang)
    cos2 = jnp.concatenate([cos, cos], -1)
    sin2 = jnp.concatenate([-sin, sin], -1)
    lg = jnp.log1p(-jnp.exp2(-5.0 - jnp.arange(H_A, dtype=F32)))
    idx = jnp.arange(chunk, dtype=F32)
    diff = idx[:, None] - idx[None, :]
    dmask = jnp.where(diff >= 0, jnp.exp(lg[:, None, None] * jnp.maximum(diff, 0.0)), 0.0)
    qdec = jnp.exp(lg[:, None] * (idx[None, :] + 1.0))
    kdec = jnp.exp(lg[:, None] * (chunk - 1.0 - idx[None, :]))
    cdec = jnp.exp(lg * chunk)
    qdec = jnp.broadcast_to(qdec[:, :, None], (H_A, chunk, DV_A))
    kdec = jnp.broadcast_to(kdec[:, :, None], (H_A, chunk, DK_A))
    cdec = jnp.broadcast_to(cdec[:, None, None], (H_A, 1, DV_A))
    return cos2, sin2, dmask, qdec, kdec, cdec


def _retention(h_ab, batch, t, pos0, state, layer, prev_outs, *, name):
    chunk = min(RET_CHUNK, t)
    if t > chunk:
        nbat, nchunk = 1, RET_ROWS // chunk
    else:
        nbat, nchunk = SMALL_T_RET_BATCH, 1
    rows = nbat * nchunk * chunk
    nsteps = t // (nchunk * chunk)
    has_state = state is not None
    tables = _retention_tables(pos0, t, chunk)
    row = lambda b, c: b * nsteps + c
    in_specs = [
        pl.BlockSpec((rows, H_A * DK_A), lambda b, c: (row(b, c), OFF_QA // (H_A * DK_A))),
        pl.BlockSpec((rows, H_A * DK_A), lambda b, c: (row(b, c), OFF_KA // (H_A * DK_A))),
        pl.BlockSpec((rows, W_A), lambda b, c: (row(b, c), OFF_VA // W_A)),
        pl.BlockSpec((rows, W_A), lambda b, c: (row(b, c), OFF_GA // W_A)),
        pl.BlockSpec((nchunk * chunk, DK_A), lambda b, c: (c, 0)),
        pl.BlockSpec((nchunk * chunk, DK_A), lambda b, c: (c, 0)),
        pl.BlockSpec((H_A, chunk, chunk), lambda b, c: (0, 0, 0)),
        pl.BlockSpec((H_A, chunk, DV_A), lambda b, c: (0, 0, 0)),
        pl.BlockSpec((H_A, chunk, DK_A), lambda b, c: (0, 0, 0)),
        pl.BlockSpec((H_A, 1, DV_A), lambda b, c: (0, 0, 0)),
    ]
    args = [h_ab, h_ab, h_ab, h_ab, *tables]
    if has_state:
        in_specs.append(pl.BlockSpec((None, nbat, H_A, DK_A, DV_A), lambda b, c: (layer, b, 0, 0, 0)))
        args.append(state)
    a_args, a_specs, aliases = _alias_args(prev_outs, len(args))
    z, s_out = pl.pallas_call(
        functools.partial(_retention_kernel, chunk=chunk, nchunk=nchunk, nbat=nbat,
                          has_state=has_state, has_prev=prev_outs is not None),
        out_shape=(jax.ShapeDtypeStruct((batch * t, W_A), BF16),
                   jax.ShapeDtypeStruct((DEPTH, batch, H_A, DK_A, DV_A), F32)),
        grid=(batch // nbat, nsteps),
        in_specs=in_specs + a_specs,
        out_specs=(pl.BlockSpec((rows, W_A), lambda b, c: (row(b, c), 0)),
                   pl.BlockSpec((None, nbat, H_A, DK_A, DV_A), lambda b, c: (layer, b, 0, 0, 0))),
        scratch_shapes=[pltpu.VMEM((nbat, H_A, DK_A, DV_A), F32)],
        input_output_aliases=aliases,
        compiler_params=_params("arbitrary", "arbitrary"),
        name=name,
    )(*args, *a_args)
    return z, (s_out,)


def _pool_kernel(*refs, nb, tb, pos0, has_state, has_prev):
    refs = list(refs)
    u_ref, g_ref, halo_ref, wmap_ref, scale_ref = refs[:5]
    refs = refs[5:]
    if has_prev:
        refs.pop(0)
    z_ref, pout_ref, ext_scr, p_scr = refs
    ti = pl.program_id(1)
    for bi in range(nb):
        rows = pl.ds(bi * tb, tb)
        if has_state:
            ext_scr[0:1, :] = jnp.zeros((1, W_B), F32)
            ext_scr[1:POOL_HALO, :] = halo_ref[bi]
        else:
            ext_scr[0:POOL_HALO, :] = jnp.where(ti == 0, 0.0, halo_ref[...])
        ext_scr[POOL_HALO:POOL_HALO + tb, :] = u_ref[rows, :]
        t_idx = pos0 + ti * tb + lax.broadcasted_iota(jnp.int32, (tb, GW_B), 0)
        for gi, w in enumerate(POOL_WINDOWS):
            cols = slice(gi * GW_B, (gi + 1) * GW_B)
            acc = ext_scr[:, cols]
            shift = 1
            while shift < w:
                acc = acc + pltpu.roll(acc, shift, 0)
                shift *= 2
            cnt = jnp.minimum(t_idx + 1, w).astype(F32)
            u = ext_scr[POOL_HALO:POOL_HALO + tb, cols]
            p_scr[rows, cols] = acc[POOL_HALO:POOL_HALO + tb] / cnt - u

        @pl.when(ti == pl.num_programs(1) - 1)
        def _():
            pout_ref[bi] = ext_scr[tb + 1:tb + POOL_HALO, :]

    for gi in range(N_POOL_GROUPS):
        cols = slice(gi * GW_B, (gi + 1) * GW_B)
        pm = jnp.dot(p_scr[:, cols].astype(BF16), wmap_ref[gi], preferred_element_type=F32)
        pm = pm * scale_ref[:, cols]
        z_ref[:, cols] = (pm * _silu(g_ref[:, cols])).astype(z_ref.dtype)


def _pool(h_ab, batch, t, pos0, state, layer, prev_outs, wmap, scale, *, name):
    has_state = state is not None
    if t >= POOL_ROWS:
        nb, tb = 1, POOL_ROWS
    else:
        nb, tb = SMALL_T_POOL_BATCH, t
    nt = t // tb
    rows = nb * tb
    rblk = lambda b, i: b * nt + i
    in_specs = [pl.BlockSpec((rows, W_B), lambda b, i: (rblk(b, i), OFF_UB // W_B)),
                pl.BlockSpec((rows, W_B), lambda b, i: (rblk(b, i), OFF_GB // W_B))]
    args = [h_ab, h_ab]
    if has_state:
        assert nt == 1
        in_specs.append(pl.BlockSpec((None, nb, POOL_BUF, W_B), lambda b, i: (layer, b, 0, 0)))
        args.append(state)
    else:
        assert nb == 1
        per = tb // POOL_HALO
        in_specs.append(pl.BlockSpec(
            (POOL_HALO, W_B), lambda b, i: (jnp.maximum(rblk(b, i) * per - 1, 0), OFF_UB // W_B)))
        args.append(h_ab)
    in_specs += [pl.BlockSpec((N_POOL_GROUPS, GW_B, GW_B), lambda b, i: (0, 0, 0)),
                 pl.BlockSpec((1, W_B), lambda b, i: (0, 0))]
    args += [wmap, scale]
    a_args, a_specs, aliases = _alias_args(prev_outs, len(args))
    z, p_out = pl.pallas_call(
        functools.partial(_pool_kernel, nb=nb, tb=tb, pos0=pos0, has_state=has_state,
                          has_prev=prev_outs is not None),
        out_shape=(jax.ShapeDtypeStruct((batch * t, W_B), BF16),
                   jax.ShapeDtypeStruct((DEPTH, batch, POOL_BUF, W_B), F32)),
        grid=(batch // nb, nt),
        in_specs=in_specs + a_specs,
        out_specs=(pl.BlockSpec((rows, W_B), lambda b, i: (rblk(b, i), 0)),
                   pl.BlockSpec((None, nb, POOL_BUF, W_B), lambda b, i: (layer, b, 0, 0))),
        scratch_shapes=[pltpu.VMEM((POOL_HALO + tb, W_B), F32),
                        pltpu.VMEM((rows, W_B), F32)],
        input_output_aliases=aliases,
        compiler_params=_params("arbitrary", "arbitrary"),
        name=name,
    )(*args, *a_args)
    return z, (p_out,)


def _t5_bucket_table(bq):
    span = WINDOW + bq
    dist = np.arange(bq)[:, None] + WINDOW - np.arange(span)[None, :]
    max_exact = NUM_BUCKETS // 2
    d = np.maximum(dist, 0).astype(np.float32)
    large = max_exact + (np.log(np.maximum(d, np.float32(1.0)) / np.float32(max_exact))
                         / np.float32(math.log(MAX_DISTANCE / max_exact))
                         * np.float32(NUM_BUCKETS - max_exact)).astype(np.int32)
    large = np.minimum(large, NUM_BUCKETS - 1)
    bucket = np.where(dist < max_exact, np.maximum(dist, 0), large)
    valid = (dist >= 0) & (dist < WINDOW)
    return np.where(valid, bucket, -1).astype(np.int32)


def _attention_kernel(*refs, bq, nbat, has_cache, mask_first, has_prev):
    refs = list(refs)
    (sinks_ref, relb_ref, q_ref, kc_ref, vc_ref, kp_ref, vp_ref, g_ref, bucket_ref) = refs[:9]
    refs = refs[9:]
    if has_prev:
        del refs[:2]
    z_ref, wk_ref, wv_ref, bias_scr, qbd_scr, s_scr, p_scr, o_scr, l_scr = refs
    span = WINDOW + bq
    hrows = H_C * bq
    grows = G_C * bq
    chunk = min(hrows, SOFTMAX_ROWS)
    one_matmul = hrows <= SOFTMAX_ROWS
    fmin = jnp.finfo(F32).min
    n = pl.program_id(1)

    @pl.when((pl.program_id(0) == 0) & (n == 0))
    def _():
        bucket = bucket_ref[...]
        col = lax.broadcasted_iota(jnp.int32, (bq, span), 1)
        for h in range(H_C):
            acc = jnp.where(bucket < 0, fmin, 0.0)
            for b in range(NUM_BUCKETS):
                acc = jnp.where(bucket == b, relb_ref[b, h], acc)
            acc = jnp.where(col == 0, sinks_ref[h], acc)
            bias_scr[0, h * bq:(h + 1) * bq, :] = acc
            if mask_first:
                bias_scr[1, h * bq:(h + 1) * bq, :] = jnp.where((col < WINDOW) & (col > 0), fmin, acc)

    bsel = jnp.where(n == 0, 1, 0) if mask_first else 0
    lane_kv = lax.broadcasted_iota(jnp.int32, (1, KVW), 1) // HD_C
    low_half = lax.broadcasted_iota(jnp.int32, (1, 2 * HD_C), 1) < HD_C
    row0 = lax.broadcasted_iota(jnp.int32, (WINDOW, KVW), 0) == 0
    ones = jnp.ones((span, DENOM_W), BF16)
    for bi in range(nbat):
        rows = slice(bi * bq, (bi + 1) * bq)
        base = bi * hrows
        kc = kc_ref[rows, :]
        vc = vc_ref[rows, :]
        kp = kp_ref[bi] if has_cache else kp_ref[...]
        vp = vp_ref[bi] if has_cache else vp_ref[...]
        kx = jnp.concatenate([jnp.where(row0, 0.0, kp), kc], axis=0).astype(BF16)
        vx = jnp.concatenate([jnp.where(row0, 0.0, vp), vc], axis=0).astype(BF16)
        q = q_ref[rows, :] * (HD_C ** -0.5)
        for kk in range(KV_C):
            qbd_scr[base + kk * grows:base + (kk + 1) * grows, :] = jnp.concatenate(
                [jnp.where(lane_kv == kk, q[:, g * KVW:(g + 1) * KVW], 0.0) for g in range(G_C)],
                axis=0).astype(BF16)
        s_scr[base:base + hrows, :] = lax.dot_general(
            qbd_scr[base:base + hrows, :], kx, (((1,), (1,)), ((), ())), preferred_element_type=F32)
        for c in range(hrows // chunk):
            rs = slice(base + c * chunk, base + (c + 1) * chunk)
            s = s_scr[rs, :] + bias_scr[bsel, c * chunk:(c + 1) * chunk, :]
            m = jnp.max(s, axis=-1, keepdims=True)
            p_scr[rs, :] = jnp.exp(s - m).astype(BF16)
        if one_matmul:
            o3 = jnp.dot(p_scr[base:base + hrows, :], jnp.concatenate([vx, ones], axis=1),
                         preferred_element_type=F32)
            for kk in range(KV_C):
                o_scr[base + kk * grows:base + (kk + 1) * grows, :] = (
                    o3[kk * grows:(kk + 1) * grows, (kk // 2) * 2 * HD_C:(kk // 2 + 1) * 2 * HD_C])
            l_scr[base:base + hrows, :] = o3[:, KVW:KVW + DENOM_W]
        else:
            for kk in range(KV_C):
                rk = slice(base + kk * grows, base + (kk + 1) * grows)
                slab = vx[:, (kk // 2) * 2 * HD_C:(kk // 2 + 1) * 2 * HD_C]
                vk = jnp.where(low_half if kk % 2 == 0 else ~low_half, slab, jnp.zeros_like(slab))
                o_scr[rk, :] = jnp.dot(p_scr[rk, :], vk, preferred_element_type=F32)
            l_scr[base:base + hrows, :] = jnp.dot(p_scr[base:base + hrows, :], ones,
                                                  preferred_element_type=F32)
        for g in range(G_C):
            halves = []
            for half in range(2):
                pair = []
                for kk in (2 * half, 2 * half + 1):
                    rr = slice(base + kk * grows + g * bq, base + kk * grows + (g + 1) * bq)
                    pair.append(o_scr[rr, :] * (1.0 / l_scr[rr, :]))
                halves.append(jnp.where(low_half, pair[0], pair[1]))
            og = jnp.concatenate(halves, axis=1)
            cols = slice(g * KVW, (g + 1) * KVW)
            z_ref[rows, cols] = (og * _silu(g_ref[rows, cols])).astype(z_ref.dtype)

        @pl.when(n == pl.num_programs(1) - 1)
        def _():
            if bq == WINDOW:
                wk_ref[bi] = kc
                wv_ref[bi] = vc
            else:
                wk_ref[bi, 0:WINDOW - bq, :] = kp[bq:, :]
                wk_ref[bi, WINDOW - bq:, :] = kc
                wv_ref[bi, 0:WINDOW - bq, :] = vp[bq:, :]
                wv_ref[bi, WINDOW - bq:, :] = vc


def _attention(h_att, batch, t, pos0, cache_k, cache_v, layer, prev_outs, sinks, rel_bias, *, name):
    bq = min(ATT_BLOCK, t)
    nb = t // bq
    nbat = 1 if nb > 1 else SMALL_T_ATT_BATCH
    span = WINDOW + bq
    has_cache = cache_k is not None
    assert pos0 == 0 or pos0 >= WINDOW
    mask_first = pos0 == 0
    rows = nbat * bq
    row = lambda b, n, *_: b * nb + n
    in_specs = [
        pl.BlockSpec((rows, W_C), lambda b, n, *_: (row(b, n), OFF_QC // W_C)),
        pl.BlockSpec((rows, KVW), lambda b, n, *_: (row(b, n), OFF_KC // KVW)),
        pl.BlockSpec((rows, KVW), lambda b, n, *_: (row(b, n), OFF_VC // KVW)),
    ]
    args = [h_att, h_att, h_att]
    if has_cache:
        assert nb == 1
        in_specs += [pl.BlockSpec((None, nbat, WINDOW, KVW), lambda b, n, *_: (layer, b, 0, 0)),
                     pl.BlockSpec((None, nbat, WINDOW, KVW), lambda b, n, *_: (layer, b, 0, 0))]
        args += [cache_k, cache_v]
    else:
        assert bq == WINDOW and nbat == 1
        prev = lambda b, n, *_: jnp.maximum(row(b, n) - 1, 0)
        in_specs += [pl.BlockSpec((WINDOW, KVW), lambda b, n, *_: (prev(b, n), OFF_KC // KVW)),
                     pl.BlockSpec((WINDOW, KVW), lambda b, n, *_: (prev(b, n), OFF_VC // KVW))]
        args += [h_att, h_att]
    in_specs += [
        pl.BlockSpec((rows, W_C), lambda b, n, *_: (row(b, n), OFF_GC // W_C)),
        pl.BlockSpec((bq, span), lambda b, n, *_: (0, 0)),
    ]
    args += [h_att, jnp.asarray(_t5_bucket_table(bq))]
    a_args, a_specs, aliases = _alias_args(prev_outs, 2 + len(args))
    srows = nbat * H_C * bq
    z, wk, wv = pl.pallas_call(
        functools.partial(_attention_kernel, bq=bq, nbat=nbat, has_cache=has_cache,
                          mask_first=mask_first, has_prev=prev_outs is not None),
        out_shape=(jax.ShapeDtypeStruct((batch * t, W_C), BF16),
                   jax.ShapeDtypeStruct((DEPTH, batch, WINDOW, KVW), F32),
                   jax.ShapeDtypeStruct((DEPTH, batch, WINDOW, KVW), F32)),
        grid_spec=pltpu.PrefetchScalarGridSpec(
            num_scalar_prefetch=2,
            grid=(batch // nbat, nb),
            in_specs=in_specs + a_specs,
            out_specs=(pl.BlockSpec((rows, W_C), lambda b, n, *_: (row(b, n), 0)),
                       pl.BlockSpec((None, nbat, WINDOW, KVW), lambda b, n, *_: (layer, b, 0, 0)),
                       pl.BlockSpec((None, nbat, WINDOW, KVW), lambda b, n, *_: (layer, b, 0, 0))),
            scratch_shapes=[pltpu.VMEM((2 if mask_first else 1, H_C * bq, span), F32),
                            pltpu.VMEM((srows, KVW), BF16),
                            pltpu.VMEM((srows, span), F32),
                            pltpu.VMEM((srows, span), BF16),
                            pltpu.VMEM((srows, 2 * HD_C), F32),
                            pltpu.VMEM((srows, DENOM_W), F32)]),
        input_output_aliases=aliases,
        compiler_params=_params("arbitrary", "arbitrary"),
        name=name,
    )(sinks, rel_bias, *args, *a_args)
    return z, (wk, wv)


def _out_kernel(x_ref, za_ref, zb_ref, zc_ref, ma_ref, mb_ref, mc_ref, wa_ref, wb_ref, wc_ref,
                wo_ref, lng_ref, lnb_ref, y_ref, yb_ref, *, alpha):
    merged = _sigmoid(ma_ref[...]) * jnp.dot(za_ref[...], wa_ref[...], preferred_element_type=F32)
    merged += _sigmoid(mb_ref[...]) * jnp.dot(zb_ref[...], wb_ref[...], preferred_element_type=F32)
    merged += _sigmoid(mc_ref[...]) * jnp.dot(zc_ref[...], wc_ref[...], preferred_element_type=F32)
    y = jnp.dot(merged.astype(BF16), wo_ref[...], preferred_element_type=F32)
    r = alpha * x_ref[...] + y
    mu = jnp.mean(r, axis=-1, keepdims=True)
    d = r - mu
    var = jnp.mean(d * d, axis=-1, keepdims=True)
    out = d * lax.rsqrt(var + LN_EPS) * lng_ref[...] + lnb_ref[...]
    y_ref[...] = out
    yb_ref[...] = out.astype(BF16)


def _out_stage(x, za, zb, zc, h_gate, wa, wb, wc, wo, lng, lnb, alpha, *, name):
    m = x.shape[0]
    tm = OUT_STAGE_TM
    rowblk = lambda w: pl.BlockSpec((tm, w), lambda i: (i, 0))
    gate = lambda j: pl.BlockSpec((tm, D_MODEL), lambda i: (i, j))
    const = lambda a: pl.BlockSpec(a.shape, lambda i: (0,) * a.ndim, pipeline_mode=pl.Buffered(1))
    return pl.pallas_call(
        functools.partial(_out_kernel, alpha=alpha),
        out_shape=(jax.ShapeDtypeStruct((m, D_MODEL), F32),
                   jax.ShapeDtypeStruct((m, D_MODEL), BF16)),
        grid=(m // tm,),
        in_specs=[rowblk(D_MODEL), rowblk(W_A), rowblk(W_B), rowblk(W_C),
                  gate(0), gate(1), gate(2),
                  const(wa), const(wb), const(wc), const(wo), const(lng), const(lnb)],
        out_specs=(rowblk(D_MODEL), rowblk(D_MODEL)),
        compiler_params=_params("parallel"),
        name=name,
    )(x, za, zb, zc, h_gate, h_gate, h_gate, wa, wb, wc, wo, lng, lnb)


def _mixer_layer(x, xb, batch, t, pos0, states, layer, prev, w, w_in, alpha, tag):
    st_ret, st_k, st_v, st_pool = states if states is not None else (None,) * 4
    pv_ret, pv_win, pv_pool = prev if prev is not None else (None,) * 3
    if xb is None:
        h_att, xb = _in_proj(x, w["w_att"], N_ATT, emit_xb=True, name=f"in_proj_att_{tag}")
    else:
        h_att = _in_proj(xb, w["w_att"], N_ATT, name=f"in_proj_att_{tag}")
    h_ab = _in_proj(xb, w_in, N_AB, layer=layer, col0=0, name=f"in_proj_ab_{tag}")
    h_gate = _in_proj(xb, w_in, N_GATE, layer=layer, col0=REF_GATE, name=f"in_proj_gate_{tag}")
    za, o_ret = _retention(h_ab, batch, t, pos0, st_ret, layer, pv_ret, name=f"retention_{tag}")
    zb, o_pool = _pool(h_ab, batch, t, pos0, st_pool, layer, pv_pool, w["w_pool_map"],
                       w["pool_scale"], name=f"pool_{tag}")
    zc, o_win = _attention(h_att, batch, t, pos0, st_k, st_v, layer, pv_win, w["sinks"],
                           w["rel_bias"], name=f"attention_{tag}")
    y, yb = _out_stage(x, za, zb, zc, h_gate, w["w_ret_o"], w["w_pool_o"], w["w_att_o"], w["w_out"],
                       w["ln_g"], w["ln_b"], alpha, name=f"out_stage_{tag}")
    return y, yb, (o_ret, o_win, o_pool)


def _regroup_heads(w, axis):
    shp = w.shape
    w = w.reshape(shp[:axis] + (KV_C, G_C, HD_C) + shp[axis + 1:])
    w = jnp.swapaxes(w, axis, axis + 1)
    return w.reshape(shp)


def _prep_w_att(wi):
    return jnp.concatenate([
        _regroup_heads(wi[:, REF_QC:REF_QC + W_C], 1),
        _regroup_heads(wi[:, REF_GC:REF_GC + W_C], 1),
        wi[:, REF_KC:REF_KC + KVW],
        wi[:, REF_VC:REF_VC + KVW]], axis=1).astype(BF16)


def kernel(x_prompt, x_sample, state_ret, cache_win_k, cache_win_v, state_pool, w_in, w_ret_o,
           w_pool_map, pool_scale, w_pool_o, attn_sinks, w_att_o, w_out, ln_g, ln_b, rel_bias):
    alpha = (2.0 * DEPTH) ** 0.25
    bp, tp, _ = x_prompt.shape
    bs, ts, _ = x_sample.shape
    xp = x_prompt.reshape(bp * tp, D_MODEL)
    xs = x_sample.reshape(bs * ts, D_MODEL)
    xpb = xsb = None
    ck = cache_win_k.reshape(DEPTH, bs, WINDOW, KVW)
    cv = cache_win_v.reshape(DEPTH, bs, WINDOW, KVW)
    sample_states = (state_ret, ck, cv, state_pool)
    out_p = out_s = None
    for l in range(DEPTH):
        w = {
            "w_att": _prep_w_att(w_in[l]),
            "w_ret_o": w_ret_o[l].astype(BF16),
            "w_pool_map": w_pool_map[l].astype(BF16),
            "pool_scale": pool_scale[l].reshape(1, W_B),
            "w_pool_o": w_pool_o[l].astype(BF16),
            "sinks": attn_sinks[l],
            "w_att_o": _regroup_heads(w_att_o[l], 0).astype(BF16),
            "w_out": w_out[l].astype(BF16),
            "ln_g": ln_g[l].reshape(1, D_MODEL),
            "ln_b": ln_b[l].reshape(1, D_MODEL),
            "rel_bias": rel_bias,
        }
        xp, xpb, out_p = _mixer_layer(xp, xpb, bp, tp, 0, None, l, out_p, w, w_in, alpha, f"p{l}")
        xs, xsb, out_s = _mixer_layer(xs, xsb, bs, ts, PAST_LEN, sample_states, l, out_s, w, w_in,
                                      alpha, f"s{l}")
    (ret_p,), (kp, vp), (pp,) = out_p
    (ret_s,), (ks, vs), (ps,) = out_s
    win = lambda a, b: a.reshape(DEPTH, b, WINDOW, KV_C, HD_C)
    return (xp.reshape(bp, tp, D_MODEL), xs.reshape(bs, ts, D_MODEL), ret_p, ret_s,
            win(kp, bp), win(ks, bs), win(vp, bp), win(vs, bs), pp, ps)
```

```python
import functools
import math

import jax
import jax.numpy as jnp
import numpy as np
from jax import lax
from jax.experimental import pallas as pl
from jax.experimental.pallas import tpu as pltpu

D_MODEL = 2048
DEPTH = 2
PAST_LEN = 8192
H_A, DK_A, DV_A = 4, 128, 256
W_A = H_A * DV_A
RET_CHUNK = 128
ROPE_BASE = 10000.0
N_POOL_GROUPS = 4
W_B = 1024
GW_B = W_B // N_POOL_GROUPS
POOL_WINDOWS = (2, 4, 8, 16)
POOL_BUF = 15
POOL_HALO = 16
H_C, KV_C, HD_C = 16, 4, 64
G_C = H_C // KV_C
W_C = H_C * HD_C
KVW = KV_C * HD_C
WINDOW = 128
ATT_BLOCK = 128
SOFTMAX_ROWS = 128
DENOM_W = 2 * HD_C
NUM_BUCKETS = 32
MAX_DISTANCE = 128
LN_EPS = 1e-5
RMS_EPS = 1e-6

REF_QC, REF_KC, REF_VC, REF_GC = 5120, 6144, 6400, 6656
N_AB = 5120
N_ATT = 2560
REF_GATE = 7680
N_GATE = 3 * D_MODEL
OFF_QA, OFF_KA, OFF_VA, OFF_GA, OFF_UB, OFF_GB = 0, 512, 1024, 2048, 3072, 4096
OFF_QC, OFF_GC, OFF_KC, OFF_VC = 0, 1024, 2048, 2304

F32 = jnp.float32
BF16 = jnp.bfloat16
VMEM_LIMIT = 56 * 1024 * 1024

IN_PROJ_TM = 512
IN_PROJ_TN = {N_AB: 1280, N_ATT: 2560, N_GATE: 1536}
OUT_STAGE_TM = 256
OUT_STAGE_TN = 512
LN_ROWS = 16
POOL_ROWS = 512
SMALL_T_POOL_BATCH = 16
SMALL_T_ATT_BATCH = 8
RET_ROWS = 256
SMALL_T_RET_BATCH = 4


def _sigmoid(x):
    return 1.0 / (1.0 + jnp.exp(-x))


def _silu(x):
    return x * _sigmoid(x)


def _params(*sem):
    return pltpu.CompilerParams(dimension_semantics=sem, vmem_limit_bytes=VMEM_LIMIT)


def _alias_args(prev_outs, n_in):
    if prev_outs is None:
        return [], [], {}
    specs = [pl.BlockSpec(memory_space=pl.ANY) for _ in prev_outs]
    return list(prev_outs), specs, {n_in + i: 1 + i for i in range(len(prev_outs))}


def _state_out_spec(block, index_map, layer, has_prev):
    if has_prev:
        return pl.BlockSpec((None,) + block, lambda *a: (layer,) + index_map(*a))
    return pl.BlockSpec((DEPTH,) + block, lambda *a: (0,) + index_map(*a))


def _state_store(ref, idx, value, layer, has_prev):
    if has_prev:
        ref[idx] = value
    else:
        for l in range(DEPTH):
            ref[(l,) + idx] = value if l == layer else jnp.zeros_like(value)


def _matmul_kernel(x_ref, w_ref, o_ref, *rest, cast_w, emit_xb):
    rest = list(rest)
    xb_ref = rest.pop(0) if emit_xb else None
    if cast_w:
        wb_scr = rest.pop(0)

        @pl.when(pl.program_id(1) == 0)
        def _():
            wb_scr[...] = w_ref[...].astype(BF16)
        w = wb_scr[...]
    else:
        w = w_ref[...]
    x = x_ref[...].astype(BF16)
    if emit_xb:
        xb_ref[...] = x
    o_ref[...] = jnp.dot(x, w, preferred_element_type=F32).astype(o_ref.dtype)


def _in_proj(x, w, n, *, layer=None, col0=0, emit_xb=False, out_dtype=F32, name):
    m, k = x.shape
    tm, tn = IN_PROJ_TM, IN_PROJ_TN[n]
    cast_w = w.ndim == 3
    if cast_w:
        w_spec = pl.BlockSpec((None, k, tn), lambda j, i: (layer, 0, col0 // tn + j))
    else:
        w_spec = pl.BlockSpec((k, tn), lambda j, i: (0, j))
    assert not emit_xb or n == tn
    out_shape = [jax.ShapeDtypeStruct((m, n), out_dtype)]
    out_specs = [pl.BlockSpec((tm, tn), lambda j, i: (i, j))]
    if emit_xb:
        out_shape.append(jax.ShapeDtypeStruct((m, k), BF16))
        out_specs.append(pl.BlockSpec((tm, k), lambda j, i: (i, 0)))
    outs = pl.pallas_call(
        functools.partial(_matmul_kernel, cast_w=cast_w, emit_xb=emit_xb),
        out_shape=out_shape,
        grid=(n // tn, m // tm),
        in_specs=[pl.BlockSpec((tm, k), lambda j, i: (i, 0)), w_spec],
        out_specs=out_specs,
        scratch_shapes=[pltpu.VMEM((k, tn), BF16)] if cast_w else [],
        compiler_params=_params("arbitrary", "arbitrary"),
        name=name,
    )(x, w)
    return outs if emit_xb else outs[0]


def _retention_kernel(*refs, chunk, nchunk, nbat, has_state, layer, has_prev):
    refs = list(refs)
    (q_ref, k_ref, v_ref, g_ref, cos_ref, sin_ref, dmask_ref, qdec_ref, kdec_ref, cdec_ref) = refs[:10]
    refs = refs[10:]
    s0_ref = refs.pop(0) if has_state else None
    if has_prev:
        refs.pop(0)
    z_ref, sout_ref, s_scr = refs
    c = pl.program_id(1)

    @pl.when(c == 0)
    def _():
        if has_state:
            s_scr[...] = s0_ref[...]
        else:
            s_scr[...] = jnp.zeros_like(s_scr)

    for bi in range(nbat):
        for hd in range(H_A):
            s = s_scr[bi, hd]
            for ci in range(nchunk):
                rows = slice((bi * nchunk + ci) * chunk, (bi * nchunk + ci + 1) * chunk)
                trow = slice(ci * chunk, (ci + 1) * chunk)
                cos = cos_ref[trow, :]
                sin = sin_ref[trow, :]
                q = q_ref[rows, hd * DK_A:(hd + 1) * DK_A]
                k = k_ref[rows, hd * DK_A:(hd + 1) * DK_A]
                v = v_ref[rows, hd * DV_A:(hd + 1) * DV_A].astype(BF16)
                qr = q * cos + pltpu.roll(q, DK_A // 2, 1) * sin
                kr = (k * cos + pltpu.roll(k, DK_A // 2, 1) * sin) * (DK_A ** -0.5)
                qb = qr.astype(BF16)
                sc = lax.dot_general(qb, kr.astype(BF16), (((1,), (1,)), ((), ())),
                                     preferred_element_type=F32) * dmask_ref[hd]
                o = (jnp.dot(sc.astype(BF16), v, preferred_element_type=F32)
                     + jnp.dot(qb, s.astype(BF16), preferred_element_type=F32) * qdec_ref[hd])
                kd = (kr * kdec_ref[hd]).astype(BF16)
                s = s * cdec_ref[hd] + lax.dot_general(
                    kd, v, (((0,), (0,)), ((), ())), preferred_element_type=F32)
                o = o * lax.rsqrt(jnp.mean(o * o, axis=-1, keepdims=True) + RMS_EPS)
                g = g_ref[rows, hd * DV_A:(hd + 1) * DV_A]
                z_ref[rows, hd * DV_A:(hd + 1) * DV_A] = (o * _silu(g)).astype(z_ref.dtype)
            s_scr[bi, hd] = s

    @pl.when(c == pl.num_programs(1) - 1)
    def _():
        _state_store(sout_ref, (Ellipsis,), s_scr[...], layer, has_prev)


def _retention_tables(pos0, t, chunk):
    half = DK_A // 2
    inv = ROPE_BASE ** (-jnp.arange(half, dtype=F32) / half)
    pos = pos0 + jnp.arange(t, dtype=F32)
    ang = pos[:, None] * inv[None, :]
    cos, sin = jnp.cos(ang), jnp.sin(ang)
    cos2 = jnp.concatenate([cos, cos], -1)
    sin2 = jnp.concatenate([-sin, sin], -1)
    lg = jnp.log1p(-jnp.exp2(-5.0 - jnp.arange(H_A, dtype=F32)))
    idx = jnp.arange(chunk, dtype=F32)
    diff = idx[:, None] - idx[None, :]
    dmask = jnp.where(diff >= 0, jnp.exp(lg[:, None, None] * jnp.maximum(diff, 0.0)), 0.0)
    qdec = jnp.exp(lg[:, None] * (idx[None, :] + 1.0))
    kdec = jnp.exp(lg[:, None] * (chunk - 1.0 - idx[None, :]))
    cdec = jnp.exp(lg * chunk)
    qdec = jnp.broadcast_to(qdec[:, :, None], (H_A, chunk, DV_A))
    kdec = jnp.broadcast_to(kdec[:, :, None], (H_A, chunk, DK_A))
    cdec = jnp.broadcast_to(cdec[:, None, None], (H_A, 1, DV_A))
    return cos2, sin2, dmask, qdec, kdec, cdec


def _retention(h_ab, batch, t, pos0, state, layer, prev_outs, *, name):
    chunk = min(RET_CHUNK, t)
    if t > chunk:
        nbat, nchunk = 1, RET_ROWS // chunk
    else:
        nbat, nchunk = SMALL_T_RET_BATCH, 1
    rows = nbat * nchunk * chunk
    nsteps = t // (nchunk * chunk)
    has_state = state is not None
    tables = _retention_tables(pos0, t, chunk)
    row = lambda b, c: b * nsteps + c
    in_specs = [
        pl.BlockSpec((rows, H_A * DK_A), lambda b, c: (row(b, c), OFF_QA // (H_A * DK_A))),
        pl.BlockSpec((rows, H_A * DK_A), lambda b, c: (row(b, c), OFF_KA // (H_A * DK_A))),
        pl.BlockSpec((rows, W_A), lambda b, c: (row(b, c), OFF_VA // W_A)),
        pl.BlockSpec((rows, W_A), lambda b, c: (row(b, c), OFF_GA // W_A)),
        pl.BlockSpec((nchunk * chunk, DK_A), lambda b, c: (c, 0)),
        pl.BlockSpec((nchunk * chunk, DK_A), lambda b, c: (c, 0)),
        pl.BlockSpec((H_A, chunk, chunk), lambda b, c: (0, 0, 0)),
        pl.BlockSpec((H_A, chunk, DV_A), lambda b, c: (0, 0, 0)),
        pl.BlockSpec((H_A, chunk, DK_A), lambda b, c: (0, 0, 0)),
        pl.BlockSpec((H_A, 1, DV_A), lambda b, c: (0, 0, 0)),
    ]
    args = [h_ab, h_ab, h_ab, h_ab, *tables]
    if has_state:
        in_specs.append(pl.BlockSpec((None, nbat, H_A, DK_A, DV_A), lambda b, c: (layer, b, 0, 0, 0)))
        args.append(state)
    a_args, a_specs, aliases = _alias_args(prev_outs, len(args))
    z, s_out = pl.pallas_call(
        functools.partial(_retention_kernel, chunk=chunk, nchunk=nchunk, nbat=nbat,
                          has_state=has_state, layer=layer, has_prev=prev_outs is not None),
        out_shape=(jax.ShapeDtypeStruct((batch * t, W_A), BF16),
                   jax.ShapeDtypeStruct((DEPTH, batch, H_A, DK_A, DV_A), F32)),
        grid=(batch // nbat, nsteps),
        in_specs=in_specs + a_specs,
        out_specs=(pl.BlockSpec((rows, W_A), lambda b, c: (row(b, c), 0)),
                   _state_out_spec((nbat, H_A, DK_A, DV_A), lambda b, c: (b, 0, 0, 0), layer,
                                   prev_outs is not None)),
        scratch_shapes=[pltpu.VMEM((nbat, H_A, DK_A, DV_A), F32)],
        input_output_aliases=aliases,
        compiler_params=_params("arbitrary", "arbitrary"),
        name=name,
    )(*args, *a_args)
    return z, (s_out,)


def _pool_kernel(*refs, nb, tb, pos0, has_state, layer, has_prev):
    refs = list(refs)
    u_ref, g_ref, halo_ref, wmap_ref, scale_ref = refs[:5]
    refs = refs[5:]
    if has_prev:
        refs.pop(0)
    z_ref, pout_ref, ext_scr, p_scr = refs
    ti = pl.program_id(1)
    for bi in range(nb):
        rows = pl.ds(bi * tb, tb)
        if has_state:
            ext_scr[0:1, :] = jnp.zeros((1, W_B), F32)
            ext_scr[1:POOL_HALO, :] = halo_ref[bi]
        else:
            ext_scr[0:POOL_HALO, :] = jnp.where(ti == 0, 0.0, halo_ref[...])
        ext_scr[POOL_HALO:POOL_HALO + tb, :] = u_ref[rows, :]
        t_idx = pos0 + ti * tb + lax.broadcasted_iota(jnp.int32, (tb, GW_B), 0)
        for gi, w in enumerate(POOL_WINDOWS):
            cols = slice(gi * GW_B, (gi + 1) * GW_B)
            acc = ext_scr[:, cols]
            shift = 1
            while shift < w:
                acc = acc + pltpu.roll(acc, shift, 0)
                shift *= 2
            cnt = jnp.minimum(t_idx + 1, w).astype(F32)
            u = ext_scr[POOL_HALO:POOL_HALO + tb, cols]
            p_scr[rows, cols] = acc[POOL_HALO:POOL_HALO + tb] / cnt - u

        @pl.when(ti == pl.num_programs(1) - 1)
        def _():
            _state_store(pout_ref, (bi,), ext_scr[tb + 1:tb + POOL_HALO, :], layer, has_prev)

    for gi in range(N_POOL_GROUPS):
        cols = slice(gi * GW_B, (gi + 1) * GW_B)
        pm = jnp.dot(p_scr[:, cols].astype(BF16), wmap_ref[gi], preferred_element_type=F32)
        pm = pm * scale_ref[:, cols]
        z_ref[:, cols] = (pm * _silu(g_ref[:, cols])).astype(z_ref.dtype)


def _pool(h_ab, batch, t, pos0, state, layer, prev_outs, wmap, scale, *, name):
    has_state = state is not None
    if t >= POOL_ROWS:
        nb, tb = 1, POOL_ROWS
    else:
        nb, tb = SMALL_T_POOL_BATCH, t
    nt = t // tb
    rows = nb * tb
    rblk = lambda b, i: b * nt + i
    in_specs = [pl.BlockSpec((rows, W_B), lambda b, i: (rblk(b, i), OFF_UB // W_B)),
                pl.BlockSpec((rows, W_B), lambda b, i: (rblk(b, i), OFF_GB // W_B))]
    args = [h_ab, h_ab]
    if has_state:
        assert nt == 1
        in_specs.append(pl.BlockSpec((None, nb, POOL_BUF, W_B), lambda b, i: (layer, b, 0, 0)))
        args.append(state)
    else:
        assert nb == 1
        per = tb // POOL_HALO
        in_specs.append(pl.BlockSpec(
            (POOL_HALO, W_B), lambda b, i: (jnp.maximum(rblk(b, i) * per - 1, 0), OFF_UB // W_B)))
        args.append(h_ab)
    in_specs += [pl.BlockSpec((N_POOL_GROUPS, GW_B, GW_B), lambda b, i: (0, 0, 0)),
                 pl.BlockSpec((1, W_B), lambda b, i: (0, 0))]
    args += [wmap, scale]
    a_args, a_specs, aliases = _alias_args(prev_outs, len(args))
    z, p_out = pl.pallas_call(
        functools.partial(_pool_kernel, nb=nb, tb=tb, pos0=pos0, has_state=has_state,
                          layer=layer, has_prev=prev_outs is not None),
        out_shape=(jax.ShapeDtypeStruct((batch * t, W_B), BF16),
                   jax.ShapeDtypeStruct((DEPTH, batch, POOL_BUF, W_B), F32)),
        grid=(batch // nb, nt),
        in_specs=in_specs + a_specs,
        out_specs=(pl.BlockSpec((rows, W_B), lambda b, i: (rblk(b, i), 0)),
                   _state_out_spec((nb, POOL_BUF, W_B), lambda b, i: (b, 0, 0), layer,
                                   prev_outs is not None)),
        scratch_shapes=[pltpu.VMEM((POOL_HALO + tb, W_B), F32),
                        pltpu.VMEM((rows, W_B), F32)],
        input_output_aliases=aliases,
        compiler_params=_params("arbitrary", "arbitrary"),
        name=name,
    )(*args, *a_args)
    return z, (p_out,)


def _t5_bucket_table(bq):
    span = WINDOW + bq
    dist = np.arange(bq)[:, None] + WINDOW - np.arange(span)[None, :]
    max_exact = NUM_BUCKETS // 2
    d = np.maximum(dist, 0).astype(np.float32)
    large = max_exact + (np.log(np.maximum(d, np.float32(1.0)) / np.float32(max_exact))
                         / np.float32(math.log(MAX_DISTANCE / max_exact))
                         * np.float32(NUM_BUCKETS - max_exact)).astype(np.int32)
    large = np.minimum(large, NUM_BUCKETS - 1)
    bucket = np.where(dist < max_exact, np.maximum(dist, 0), large)
    valid = (dist >= 0) & (dist < WINDOW)
    return np.where(valid, bucket, -1).astype(np.int32)


def _attention_kernel(*refs, bq, nbat, has_cache, mask_first, layer, has_prev):
    refs = list(refs)
    (sinks_ref, relb_ref, q_ref, kc_ref, vc_ref, kp_ref, vp_ref, g_ref, bucket_ref) = refs[:9]
    refs = refs[9:]
    if has_prev:
        del refs[:2]
    z_ref, wk_ref, wv_ref, bias_scr, qbd_scr, s_scr, p_scr, o_scr, l_scr = refs
    span = WINDOW + bq
    hrows = H_C * bq
    grows = G_C * bq
    chunk = min(hrows, SOFTMAX_ROWS)
    one_matmul = hrows <= SOFTMAX_ROWS
    fmin = jnp.finfo(F32).min
    n = pl.program_id(1)

    @pl.when((pl.program_id(0) == 0) & (n == 0))
    def _():
        bucket = bucket_ref[...]
        col = lax.broadcasted_iota(jnp.int32, (bq, span), 1)
        for h in range(H_C):
            acc = jnp.where(bucket < 0, fmin, 0.0)
            for b in range(NUM_BUCKETS):
                acc = jnp.where(bucket == b, relb_ref[b, h], acc)
            acc = jnp.where(col == 0, sinks_ref[h], acc)
            bias_scr[0, h * bq:(h + 1) * bq, :] = acc
            if mask_first:
                bias_scr[1, h * bq:(h + 1) * bq, :] = jnp.where((col < WINDOW) & (col > 0), fmin, acc)

    bsel = jnp.where(n == 0, 1, 0) if mask_first else 0
    lane_kv = lax.broadcasted_iota(jnp.int32, (1, KVW), 1) // HD_C
    low_half = lax.broadcasted_iota(jnp.int32, (1, 2 * HD_C), 1) < HD_C
    row0 = lax.broadcasted_iota(jnp.int32, (WINDOW, KVW), 0) == 0
    ones = jnp.ones((span, DENOM_W), BF16)
    for bi in range(nbat):
        rows = slice(bi * bq, (bi + 1) * bq)
        base = bi * hrows
        kc = kc_ref[rows, :]
        vc = vc_ref[rows, :]
        kp = kp_ref[bi] if has_cache else kp_ref[...]
        vp = vp_ref[bi] if has_cache else vp_ref[...]
        kx = jnp.concatenate([jnp.where(row0, 0.0, kp), kc], axis=0).astype(BF16)
        vx = jnp.concatenate([jnp.where(row0, 0.0, vp), vc], axis=0).astype(BF16)
        q = q_ref[rows, :] * (HD_C ** -0.5)
        for kk in range(KV_C):
            qbd_scr[base + kk * grows:base + (kk + 1) * grows, :] = jnp.concatenate(
                [jnp.where(lane_kv == kk, q[:, g * KVW:(g + 1) * KVW], 0.0) for g in range(G_C)],
                axis=0).astype(BF16)
        s_scr[base:base + hrows, :] = lax.dot_general(
            qbd_scr[base:base + hrows, :], kx, (((1,), (1,)), ((), ())), preferred_element_type=F32)
        for c in range(hrows // chunk):
            rs = slice(base + c * chunk, base + (c + 1) * chunk)
            s = s_scr[rs, :] + bias_scr[bsel, c * chunk:(c + 1) * chunk, :]
            m = jnp.max(s, axis=-1, keepdims=True)
            p_scr[rs, :] = jnp.exp(s - m).astype(BF16)
        if one_matmul:
            o3 = jnp.dot(p_scr[base:base + hrows, :], jnp.concatenate([vx, ones], axis=1),
                         preferred_element_type=F32)
            for kk in range(KV_C):
                o_scr[base + kk * grows:base + (kk + 1) * grows, :] = (
                    o3[kk * grows:(kk + 1) * grows, (kk // 2) * 2 * HD_C:(kk // 2 + 1) * 2 * HD_C])
            l_scr[base:base + hrows, :] = o3[:, KVW:KVW + DENOM_W]
        else:
            for kk in range(KV_C):
                rk = slice(base + kk * grows, base + (kk + 1) * grows)
                slab = vx[:, (kk // 2) * 2 * HD_C:(kk // 2 + 1) * 2 * HD_C]
                vk = jnp.where(low_half if kk % 2 == 0 else ~low_half, slab, jnp.zeros_like(slab))
                ol = jnp.dot(p_scr[rk, :], jnp.concatenate([vk, ones], axis=1),
                             preferred_element_type=F32)
                o_scr[rk, :] = ol[:, :2 * HD_C]
                l_scr[rk, :] = ol[:, 2 * HD_C:]
        for g in range(G_C):
            halves = []
            for half in range(2):
                ra, rb = (slice(base + kk * grows + g * bq, base + kk * grows + (g + 1) * bq)
                          for kk in (2 * half, 2 * half + 1))
                num = jnp.where(low_half, o_scr[ra, :], o_scr[rb, :])
                den = jnp.where(low_half, l_scr[ra, :], l_scr[rb, :])
                halves.append(num * (1.0 / den))
            og = jnp.concatenate(halves, axis=1)
            cols = slice(g * KVW, (g + 1) * KVW)
            z_ref[rows, cols] = (og * _silu(g_ref[rows, cols])).astype(z_ref.dtype)

        @pl.when(n == pl.num_programs(1) - 1)
        def _():
            wk = kc if bq == WINDOW else jnp.concatenate([kp[bq:, :], kc], axis=0)
            wv = vc if bq == WINDOW else jnp.concatenate([vp[bq:, :], vc], axis=0)
            _state_store(wk_ref, (bi,), wk, layer, has_prev)
            _state_store(wv_ref, (bi,), wv, layer, has_prev)


def _attention(h_att, batch, t, pos0, cache_k, cache_v, layer, prev_outs, sinks, rel_bias, *, name):
    bq = min(ATT_BLOCK, t)
    nb = t // bq
    nbat = 1 if nb > 1 else SMALL_T_ATT_BATCH
    span = WINDOW + bq
    has_cache = cache_k is not None
    assert pos0 == 0 or pos0 >= WINDOW
    mask_first = pos0 == 0
    rows = nbat * bq
    row = lambda b, n, *_: b * nb + n
    in_specs = [
        pl.BlockSpec((rows, W_C), lambda b, n, *_: (row(b, n), OFF_QC // W_C)),
        pl.BlockSpec((rows, KVW), lambda b, n, *_: (row(b, n), OFF_KC // KVW)),
        pl.BlockSpec((rows, KVW), lambda b, n, *_: (row(b, n), OFF_VC // KVW)),
    ]
    args = [h_att, h_att, h_att]
    if has_cache:
        assert nb == 1
        in_specs += [pl.BlockSpec((None, nbat, WINDOW, KVW), lambda b, n, *_: (layer, b, 0, 0)),
                     pl.BlockSpec((None, nbat, WINDOW, KVW), lambda b, n, *_: (layer, b, 0, 0))]
        args += [cache_k, cache_v]
    else:
        assert bq == WINDOW and nbat == 1
        prev = lambda b, n, *_: jnp.maximum(row(b, n) - 1, 0)
        in_specs += [pl.BlockSpec((WINDOW, KVW), lambda b, n, *_: (prev(b, n), OFF_KC // KVW)),
                     pl.BlockSpec((WINDOW, KVW), lambda b, n, *_: (prev(b, n), OFF_VC // KVW))]
        args += [h_att, h_att]
    in_specs += [
        pl.BlockSpec((rows, W_C), lambda b, n, *_: (row(b, n), OFF_GC // W_C)),
        pl.BlockSpec((bq, span), lambda b, n, *_: (0, 0)),
    ]
    args += [h_att, jnp.asarray(_t5_bucket_table(bq))]
    a_args, a_specs, aliases = _alias_args(prev_outs, 2 + len(args))
    srows = nbat * H_C * bq
    z, wk, wv = pl.pallas_call(
        functools.partial(_attention_kernel, bq=bq, nbat=nbat, has_cache=has_cache,
                          mask_first=mask_first, layer=layer, has_prev=prev_outs is not None),
        out_shape=(jax.ShapeDtypeStruct((batch * t, W_C), BF16),
                   jax.ShapeDtypeStruct((DEPTH, batch, WINDOW, KVW), F32),
                   jax.ShapeDtypeStruct((DEPTH, batch, WINDOW, KVW), F32)),
        grid_spec=pltpu.PrefetchScalarGridSpec(
            num_scalar_prefetch=2,
            grid=(batch // nbat, nb),
            in_specs=in_specs + a_specs,
            out_specs=(pl.BlockSpec((rows, W_C), lambda b, n, *_: (row(b, n), 0)),
                       _state_out_spec((nbat, WINDOW, KVW), lambda b, n, *_: (b, 0, 0), layer,
                                       prev_outs is not None),
                       _state_out_spec((nbat, WINDOW, KVW), lambda b, n, *_: (b, 0, 0), layer,
                                       prev_outs is not None)),
            scratch_shapes=[pltpu.VMEM((2 if mask_first else 1, H_C * bq, span), F32),
                            pltpu.VMEM((srows, KVW), BF16),
                            pltpu.VMEM((srows, span), F32),
                            pltpu.VMEM((srows, span), BF16),
                            pltpu.VMEM((srows, 2 * HD_C), F32),
                            pltpu.VMEM((srows, DENOM_W), F32)]),
        input_output_aliases=aliases,
        compiler_params=_params("arbitrary", "arbitrary"),
        name=name,
    )(sinks, rel_bias, *args, *a_args)
    return z, (wk, wv)


def _out_kernel(x_ref, za_ref, zb_ref, zc_ref, ma_ref, mb_ref, mc_ref, wa_ref, wb_ref, wc_ref,
                wo_ref, lng_ref, lnb_ref, *rest, alpha, emit_bf16):
    if emit_bf16:
        y_ref, yb_ref, merged_scr, r_scr = rest
    else:
        y_ref, merged_scr, r_scr = rest

    @pl.when(pl.program_id(0) == 0)
    def _():
        r_scr[...] = jnp.zeros_like(r_scr)

    lng, lnb = lng_ref[...], lnb_ref[...]
    for rc in range(r_scr.shape[0] // LN_ROWS):
        rs = slice(rc * LN_ROWS, (rc + 1) * LN_ROWS)
        r = r_scr[rs, :]
        mu = jnp.mean(r, axis=-1, keepdims=True)
        d = r - mu
        var = jnp.mean(d * d, axis=-1, keepdims=True)
        out = d * lax.rsqrt(var + LN_EPS) * lng + lnb
        y_ref[rs, :] = out
        if emit_bf16:
            yb_ref[rs, :] = out.astype(BF16)

    za, zb, zc = za_ref[...], zb_ref[...], zc_ref[...]
    for c in range(D_MODEL // OUT_STAGE_TN):
        cs = slice(c * OUT_STAGE_TN, (c + 1) * OUT_STAGE_TN)
        merged = (_sigmoid(ma_ref[:, cs].astype(F32))
                  * jnp.dot(za, wa_ref[:, cs], preferred_element_type=F32))
        merged += (_sigmoid(mb_ref[:, cs].astype(F32))
                   * jnp.dot(zb, wb_ref[:, cs], preferred_element_type=F32))
        merged += (_sigmoid(mc_ref[:, cs].astype(F32))
                   * jnp.dot(zc, wc_ref[:, cs], preferred_element_type=F32))
        merged_scr[:, cs] = merged.astype(BF16)
    r_scr[...] = alpha * x_ref[...] + jnp.dot(merged_scr[...], wo_ref[...],
                                              preferred_element_type=F32)


def _out_stage(x, za, zb, zc, h_gate, wa, wb, wc, wo, lng, lnb, alpha, *, emit_bf16, name):
    m = x.shape[0]
    tm = OUT_STAGE_TM
    nsteps = m // tm
    cur = lambda i: jnp.minimum(i, nsteps - 1)
    rowblk = lambda w: pl.BlockSpec((tm, w), lambda i: (cur(i), 0))
    gate = lambda j: pl.BlockSpec((tm, D_MODEL), lambda i: (cur(i), j))
    outblk = pl.BlockSpec((tm, D_MODEL), lambda i: (jnp.maximum(i - 1, 0), 0))
    const = lambda a: pl.BlockSpec(a.shape, lambda i: (0,) * a.ndim, pipeline_mode=pl.Buffered(1))
    out_shape = [jax.ShapeDtypeStruct((m, D_MODEL), F32)]
    if emit_bf16:
        out_shape.append(jax.ShapeDtypeStruct((m, D_MODEL), BF16))
    outs = pl.pallas_call(
        functools.partial(_out_kernel, alpha=alpha, emit_bf16=emit_bf16),
        out_shape=out_shape,
        grid=(nsteps + 1,),
        in_specs=[rowblk(D_MODEL), rowblk(W_A), rowblk(W_B), rowblk(W_C),
                  gate(0), gate(1), gate(2),
                  const(wa), const(wb), const(wc), const(wo), const(lng), const(lnb)],
        out_specs=[outblk] * len(out_shape),
        scratch_shapes=[pltpu.VMEM((tm, D_MODEL), BF16), pltpu.VMEM((tm, D_MODEL), F32)],
        compiler_params=_params("arbitrary"),
        name=name,
    )(x, za, zb, zc, h_gate, h_gate, h_gate, wa, wb, wc, wo, lng, lnb)
    return (outs[0], outs[1]) if emit_bf16 else (outs[0], None)


def _mixer_layer(x, xb, batch, t, pos0, states, layer, prev, w, w_in, alpha, tag):
    st_ret, st_k, st_v, st_pool = states if states is not None else (None,) * 4
    pv_ret, pv_win, pv_pool = prev if prev is not None else (None,) * 3
    if xb is None:
        h_att, xb = _in_proj(x, w["w_att"], N_ATT, emit_xb=True, name=f"in_proj_att_{tag}")
    else:
        h_att = _in_proj(xb, w["w_att"], N_ATT, name=f"in_proj_att_{tag}")
    h_ab = _in_proj(xb, w_in, N_AB, layer=layer, col0=0, name=f"in_proj_ab_{tag}")
    h_gate = _in_proj(xb, w_in, N_GATE, layer=layer, col0=REF_GATE, out_dtype=BF16,
                      name=f"in_proj_gate_{tag}")
    za, o_ret = _retention(h_ab, batch, t, pos0, st_ret, layer, pv_ret, name=f"retention_{tag}")
    zb, o_pool = _pool(h_ab, batch, t, pos0, st_pool, layer, pv_pool, w["w_pool_map"],
                       w["pool_scale"], name=f"pool_{tag}")
    zc, o_win = _attention(h_att, batch, t, pos0, st_k, st_v, layer, pv_win, w["sinks"],
                           w["rel_bias"], name=f"attention_{tag}")
    y, yb = _out_stage(x, za, zb, zc, h_gate, w["w_ret_o"], w["w_pool_o"], w["w_att_o"], w["w_out"],
                       w["ln_g"], w["ln_b"], alpha, emit_bf16=layer + 1 < DEPTH,
                       name=f"out_stage_{tag}")
    return y, yb, (o_ret, o_win, o_pool)


def _regroup_heads(w, axis):
    shp = w.shape
    w = w.reshape(shp[:axis] + (KV_C, G_C, HD_C) + shp[axis + 1:])
    w = jnp.swapaxes(w, axis, axis + 1)
    return w.reshape(shp)


def _prep_w_att(wi):
    w = jnp.concatenate([
        _regroup_heads(wi[:, REF_QC:REF_QC + W_C], 1),
        _regroup_heads(wi[:, REF_GC:REF_GC + W_C], 1),
        wi[:, REF_KC:REF_KC + KVW],
        wi[:, REF_VC:REF_VC + KVW]], axis=1)
    return lax.optimization_barrier(w).astype(BF16)


def kernel(x_prompt, x_sample, state_ret, cache_win_k, cache_win_v, state_pool, w_in, w_ret_o,
           w_pool_map, pool_scale, w_pool_o, attn_sinks, w_att_o, w_out, ln_g, ln_b, rel_bias):
    alpha = (2.0 * DEPTH) ** 0.25
    bp, tp, _ = x_prompt.shape
    bs, ts, _ = x_sample.shape
    xp = x_prompt.reshape(bp * tp, D_MODEL)
    xs = x_sample.reshape(bs * ts, D_MODEL)
    xpb = xsb = None
    ck = cache_win_k.reshape(DEPTH, bs, WINDOW, KVW)
    cv = cache_win_v.reshape(DEPTH, bs, WINDOW, KVW)
    sample_states = (state_ret, ck, cv, state_pool)
    out_p = out_s = None
    for l in range(DEPTH):
        w = {
            "w_att": _prep_w_att(w_in[l]),
            "w_ret_o": w_ret_o[l].astype(BF16),
            "w_pool_map": w_pool_map[l].astype(BF16),
            "pool_scale": pool_scale[l].reshape(1, W_B),
            "w_pool_o": w_pool_o[l].astype(BF16),
            "sinks": attn_sinks[l],
            "w_att_o": _regroup_heads(w_att_o[l], 0).astype(BF16),
            "w_out": w_out[l].astype(BF16),
            "ln_g": ln_g[l].reshape(1, D_MODEL),
            "ln_b": ln_b[l].reshape(1, D_MODEL),
            "rel_bias": rel_bias,
        }
        xp, xpb, out_p = _mixer_layer(xp, xpb, bp, tp, 0, None, l, out_p, w, w_in, alpha, f"p{l}")
        xs, xsb, out_s = _mixer_layer(xs, xsb, bs, ts, PAST_LEN, sample_states, l, out_s, w, w_in,
                                      alpha, f"s{l}")
    (ret_p,), (kp, vp), (pp,) = out_p
    (ret_s,), (ks, vs), (ps,) = out_s
    win = lambda a, b: a.reshape(DEPTH, b, WINDOW, KV_C, HD_C)
    return (xp.reshape(bp, tp, D_MODEL), xs.reshape(bs, ts, D_MODEL), ret_p, ret_s,
            win(kp, bp), win(ks, bs), win(vp, bp), win(vs, bs), pp, ps)
```

```python
import functools
import math

import jax
import jax.numpy as jnp
import numpy as np
from jax import lax
from jax.experimental import pallas as pl
from jax.experimental.pallas import tpu as pltpu

D_MODEL = 2048
DEPTH = 2
PAST_LEN = 8192
H_A, DK_A, DV_A = 4, 128, 256
W_A = H_A * DV_A
RET_CHUNK = 128
ROPE_BASE = 10000.0
N_POOL_GROUPS = 4
W_B = 1024
GW_B = W_B // N_POOL_GROUPS
POOL_WINDOWS = (2, 4, 8, 16)
POOL_BUF = 15
POOL_HALO = 16
H_C, KV_C, HD_C = 16, 4, 64
G_C = H_C // KV_C
W_C = H_C * HD_C
KVW = KV_C * HD_C
WINDOW = 128
ATT_BLOCK = 128
SOFTMAX_ROWS = 128
DENOM_W = 2 * HD_C
NUM_BUCKETS = 32
MAX_DISTANCE = 128
LN_EPS = 1e-5
RMS_EPS = 1e-6

REF_QC, REF_KC, REF_VC, REF_GC = 5120, 6144, 6400, 6656
N_AB = 5120
N_ATT = 2560
REF_GATE = 7680
N_GATE = 3 * D_MODEL
OFF_QA, OFF_KA, OFF_VA, OFF_GA, OFF_UB, OFF_GB = 0, 512, 1024, 2048, 3072, 4096
OFF_QC, OFF_GC, OFF_KC, OFF_VC = 0, 1024, 2048, 2304

F32 = jnp.float32
BF16 = jnp.bfloat16
VMEM_LIMIT = 56 * 1024 * 1024

IN_PROJ_TM = 1024
IN_PROJ_TM_F32_X = 512
IN_PROJ_TN = {N_AB: 1280, N_ATT: 1280, N_GATE: 1536}
OUT_STAGE_TM = 256
OUT_STAGE_TN = 512
LN_ROWS = 16
POOL_ROWS = 512
SMALL_T_POOL_BATCH = 16
SMALL_T_ATT_BATCH = 8
RET_ROWS = 256
SMALL_T_RET_BATCH = 4


def _sigmoid(x):
    return 1.0 / (1.0 + jnp.exp(-x))


def _silu(x):
    return x * _sigmoid(x)


def _params(*sem):
    return pltpu.CompilerParams(dimension_semantics=sem, vmem_limit_bytes=VMEM_LIMIT)


def _alias_args(prev_outs, n_in):
    if prev_outs is None:
        return [], [], {}
    specs = [pl.BlockSpec(memory_space=pl.ANY) for _ in prev_outs]
    return list(prev_outs), specs, {n_in + i: 1 + i for i in range(len(prev_outs))}


def _state_out_spec(block, index_map, layer, has_prev):
    if has_prev:
        return pl.BlockSpec((None,) + block, lambda *a: (layer,) + index_map(*a))
    return pl.BlockSpec((DEPTH,) + block, lambda *a: (0,) + index_map(*a))


def _state_store(ref, idx, value, layer, has_prev):
    if has_prev:
        ref[idx] = value
    else:
        for l in range(DEPTH):
            ref[(l,) + idx] = value if l == layer else jnp.zeros_like(value)


def _matmul_kernel(x_ref, w_ref, o_ref, *rest, emit_xb):
    if emit_xb:
        xb_ref, wb_scr = rest
    else:
        wb_scr, = rest

    @pl.when(pl.program_id(1) == 0)
    def _():
        wb_scr[...] = w_ref[...].astype(BF16)

    x = x_ref[...].astype(BF16)
    if emit_xb:
        xb_ref[...] = x
    o_ref[...] = jnp.dot(x, wb_scr[...], preferred_element_type=F32).astype(o_ref.dtype)


def _in_proj(x, w, n, *, layer=None, col0=0, emit_xb=False, out_dtype=F32, name):
    m, k = x.shape
    tm, tn = (IN_PROJ_TM_F32_X if x.dtype == F32 else IN_PROJ_TM), IN_PROJ_TN[n]
    if w.ndim == 3:
        w_spec = pl.BlockSpec((None, k, tn), lambda j, i: (layer, 0, col0 // tn + j))
    else:
        w_spec = pl.BlockSpec((k, tn), lambda j, i: (0, j))
    out_shape = [jax.ShapeDtypeStruct((m, n), out_dtype)]
    out_specs = [pl.BlockSpec((tm, tn), lambda j, i: (i, j))]
    if emit_xb:
        out_shape.append(jax.ShapeDtypeStruct((m, k), BF16))
        out_specs.append(pl.BlockSpec((tm, k), lambda j, i: (i, 0)))
    outs = pl.pallas_call(
        functools.partial(_matmul_kernel, emit_xb=emit_xb),
        out_shape=out_shape,
        grid=(n // tn, m // tm),
        in_specs=[pl.BlockSpec((tm, k), lambda j, i: (i, 0)), w_spec],
        out_specs=out_specs,
        scratch_shapes=[pltpu.VMEM((k, tn), BF16)],
        compiler_params=_params("arbitrary", "arbitrary"),
        name=name,
    )(x, w)
    return outs if emit_xb else outs[0]


def _retention_kernel(*refs, chunk, nchunk, nbat, has_state, layer, has_prev):
    refs = list(refs)
    (q_ref, k_ref, v_ref, g_ref, cos_ref, sin_ref, dmask_ref, qdec_ref, kdec_ref, cdec_ref) = refs[:10]
    refs = refs[10:]
    s0_ref = refs.pop(0) if has_state else None
    if has_prev:
        refs.pop(0)
    z_ref, sout_ref, s_scr = refs
    c = pl.program_id(1)

    @pl.when(c == 0)
    def _():
        if has_state:
            s_scr[...] = s0_ref[...]
        else:
            s_scr[...] = jnp.zeros_like(s_scr)

    for bi in range(nbat):
        for hd in range(H_A):
            s = s_scr[bi, hd]
            for ci in range(nchunk):
                rows = slice((bi * nchunk + ci) * chunk, (bi * nchunk + ci + 1) * chunk)
                trow = slice(ci * chunk, (ci + 1) * chunk)
                cos = cos_ref[trow, :]
                sin = sin_ref[trow, :]
                q = q_ref[rows, hd * DK_A:(hd + 1) * DK_A]
                k = k_ref[rows, hd * DK_A:(hd + 1) * DK_A]
                v = v_ref[rows, hd * DV_A:(hd + 1) * DV_A].astype(BF16)
                qr = q * cos + pltpu.roll(q, DK_A // 2, 1) * sin
                kr = (k * cos + pltpu.roll(k, DK_A // 2, 1) * sin) * (DK_A ** -0.5)
                qb = qr.astype(BF16)
                sc = lax.dot_general(qb, kr.astype(BF16), (((1,), (1,)), ((), ())),
                                     preferred_element_type=F32) * dmask_ref[hd]
                o = (jnp.dot(sc.astype(BF16), v, preferred_element_type=F32)
                     + jnp.dot(qb, s.astype(BF16), preferred_element_type=F32) * qdec_ref[hd])
                kd = (kr * kdec_ref[hd]).astype(BF16)
                s = s * cdec_ref[hd] + lax.dot_general(
                    kd, v, (((0,), (0,)), ((), ())), preferred_element_type=F32)
                o = o * lax.rsqrt(jnp.mean(o * o, axis=-1, keepdims=True) + RMS_EPS)
                g = g_ref[rows, hd * DV_A:(hd + 1) * DV_A]
                z_ref[rows, hd * DV_A:(hd + 1) * DV_A] = (o * _silu(g)).astype(z_ref.dtype)
            s_scr[bi, hd] = s

    @pl.when(c == pl.num_programs(1) - 1)
    def _():
        _state_store(sout_ref, (Ellipsis,), s_scr[...], layer, has_prev)


def _retention_tables(pos0, t, chunk):
    half = DK_A // 2
    inv = ROPE_BASE ** (-jnp.arange(half, dtype=F32) / half)
    pos = pos0 + jnp.arange(t, dtype=F32)
    ang = pos[:, None] * inv[None, :]
    cos, sin = jnp.cos(ang), jnp.sin(ang)
    cos2 = jnp.concatenate([cos, cos], -1)
    sin2 = jnp.concatenate([-sin, sin], -1)
    lg = jnp.log1p(-jnp.exp2(-5.0 - jnp.arange(H_A, dtype=F32)))
    idx = jnp.arange(chunk, dtype=F32)
    diff = idx[:, None] - idx[None, :]
    dmask = jnp.where(diff >= 0, jnp.exp(lg[:, None, None] * jnp.maximum(diff, 0.0)), 0.0)
    qdec = jnp.exp(lg[:, None] * (idx[None, :] + 1.0))
    kdec = jnp.exp(lg[:, None] * (chunk - 1.0 - idx[None, :]))
    cdec = jnp.exp(lg * chunk)
    qdec = jnp.broadcast_to(qdec[:, :, None], (H_A, chunk, DV_A))
    kdec = jnp.broadcast_to(kdec[:, :, None], (H_A, chunk, DK_A))
    cdec = jnp.broadcast_to(cdec[:, None, None], (H_A, 1, DV_A))
    return cos2, sin2, dmask, qdec, kdec, cdec


def _retention(h_ab, batch, t, pos0, state, layer, prev_outs, *, name):
    chunk = min(RET_CHUNK, t)
    if t > chunk:
        nbat, nchunk = 1, RET_ROWS // chunk
    else:
        nbat, nchunk = SMALL_T_RET_BATCH, 1
    rows = nbat * nchunk * chunk
    nsteps = t // (nchunk * chunk)
    has_state = state is not None
    tables = _retention_tables(pos0, t, chunk)
    row = lambda b, c: b * nsteps + c
    in_specs = [
        pl.BlockSpec((rows, H_A * DK_A), lambda b, c: (row(b, c), OFF_QA // (H_A * DK_A))),
        pl.BlockSpec((rows, H_A * DK_A), lambda b, c: (row(b, c), OFF_KA // (H_A * DK_A))),
        pl.BlockSpec((rows, W_A), lambda b, c: (row(b, c), OFF_VA // W_A)),
        pl.BlockSpec((rows, W_A), lambda b, c: (row(b, c), OFF_GA // W_A)),
        pl.BlockSpec((nchunk * chunk, DK_A), lambda b, c: (c, 0)),
        pl.BlockSpec((nchunk * chunk, DK_A), lambda b, c: (c, 0)),
        pl.BlockSpec((H_A, chunk, chunk), lambda b, c: (0, 0, 0)),
        pl.BlockSpec((H_A, chunk, DV_A), lambda b, c: (0, 0, 0)),
        pl.BlockSpec((H_A, chunk, DK_A), lambda b, c: (0, 0, 0)),
        pl.BlockSpec((H_A, 1, DV_A), lambda b, c: (0, 0, 0)),
    ]
    args = [h_ab, h_ab, h_ab, h_ab, *tables]
    if has_state:
        in_specs.append(pl.BlockSpec((None, nbat, H_A, DK_A, DV_A), lambda b, c: (layer, b, 0, 0, 0)))
        args.append(state)
    a_args, a_specs, aliases = _alias_args(prev_outs, len(args))
    z, s_out = pl.pallas_call(
        functools.partial(_retention_kernel, chunk=chunk, nchunk=nchunk, nbat=nbat,
                          has_state=has_state, layer=layer, has_prev=prev_outs is not None),
        out_shape=(jax.ShapeDtypeStruct((batch * t, W_A), BF16),
                   jax.ShapeDtypeStruct((DEPTH, batch, H_A, DK_A, DV_A), F32)),
        grid=(batch // nbat, nsteps),
        in_specs=in_specs + a_specs,
        out_specs=(pl.BlockSpec((rows, W_A), lambda b, c: (row(b, c), 0)),
                   _state_out_spec((nbat, H_A, DK_A, DV_A), lambda b, c: (b, 0, 0, 0), layer,
                                   prev_outs is not None)),
        scratch_shapes=[pltpu.VMEM((nbat, H_A, DK_A, DV_A), F32)],
        input_output_aliases=aliases,
        compiler_params=_params("arbitrary", "arbitrary"),
        name=name,
    )(*args, *a_args)
    return z, (s_out,)


def _pool_kernel(*refs, nb, tb, pos0, has_state, layer, has_prev):
    refs = list(refs)
    u_ref, g_ref, halo_ref, wmap_ref, scale_ref = refs[:5]
    refs = refs[5:]
    if has_prev:
        refs.pop(0)
    z_ref, pout_ref, ext_scr, p_scr = refs
    ti = pl.program_id(1)
    for bi in range(nb):
        rows = pl.ds(bi * tb, tb)
        if has_state:
            ext_scr[0:1, :] = jnp.zeros((1, W_B), F32)
            ext_scr[1:POOL_HALO, :] = halo_ref[bi]
        else:
            ext_scr[0:POOL_HALO, :] = jnp.where(ti == 0, 0.0, halo_ref[...])
        ext_scr[POOL_HALO:POOL_HALO + tb, :] = u_ref[rows, :]
        t_idx = pos0 + ti * tb + lax.broadcasted_iota(jnp.int32, (tb, GW_B), 0)
        for gi, w in enumerate(POOL_WINDOWS):
            cols = slice(gi * GW_B, (gi + 1) * GW_B)
            acc = ext_scr[:, cols]
            shift = 1
            while shift < w:
                acc = acc + pltpu.roll(acc, shift, 0)
                shift *= 2
            cnt = jnp.minimum(t_idx + 1, w).astype(F32)
            u = ext_scr[POOL_HALO:POOL_HALO + tb, cols]
            p_scr[rows, cols] = acc[POOL_HALO:POOL_HALO + tb] / cnt - u

        @pl.when(ti == pl.num_programs(1) - 1)
        def _():
            _state_store(pout_ref, (bi,), ext_scr[tb + 1:tb + POOL_HALO, :], layer, has_prev)

    for gi in range(N_POOL_GROUPS):
        cols = slice(gi * GW_B, (gi + 1) * GW_B)
        pm = jnp.dot(p_scr[:, cols].astype(BF16), wmap_ref[gi], preferred_element_type=F32)
        pm = pm * scale_ref[:, cols]
        z_ref[:, cols] = (pm * _silu(g_ref[:, cols])).astype(z_ref.dtype)


def _pool(h_ab, batch, t, pos0, state, layer, prev_outs, wmap, scale, *, name):
    has_state = state is not None
    if t >= POOL_ROWS:
        nb, tb = 1, POOL_ROWS
    else:
        nb, tb = SMALL_T_POOL_BATCH, t
    nt = t // tb
    rows = nb * tb
    rblk = lambda b, i: b * nt + i
    in_specs = [pl.BlockSpec((rows, W_B), lambda b, i: (rblk(b, i), OFF_UB // W_B)),
                pl.BlockSpec((rows, W_B), lambda b, i: (rblk(b, i), OFF_GB // W_B))]
    args = [h_ab, h_ab]
    if has_state:
        assert nt == 1
        in_specs.append(pl.BlockSpec((None, nb, POOL_BUF, W_B), lambda b, i: (layer, b, 0, 0)))
        args.append(state)
    else:
        assert nb == 1
        per = tb // POOL_HALO
        in_specs.append(pl.BlockSpec(
            (POOL_HALO, W_B), lambda b, i: (jnp.maximum(rblk(b, i) * per - 1, 0), OFF_UB // W_B)))
        args.append(h_ab)
    in_specs += [pl.BlockSpec((N_POOL_GROUPS, GW_B, GW_B), lambda b, i: (0, 0, 0)),
                 pl.BlockSpec((1, W_B), lambda b, i: (0, 0))]
    args += [wmap, scale]
    a_args, a_specs, aliases = _alias_args(prev_outs, len(args))
    z, p_out = pl.pallas_call(
        functools.partial(_pool_kernel, nb=nb, tb=tb, pos0=pos0, has_state=has_state,
                          layer=layer, has_prev=prev_outs is not None),
        out_shape=(jax.ShapeDtypeStruct((batch * t, W_B), BF16),
                   jax.ShapeDtypeStruct((DEPTH, batch, POOL_BUF, W_B), F32)),
        grid=(batch // nb, nt),
        in_specs=in_specs + a_specs,
        out_specs=(pl.BlockSpec((rows, W_B), lambda b, i: (rblk(b, i), 0)),
                   _state_out_spec((nb, POOL_BUF, W_B), lambda b, i: (b, 0, 0), layer,
                                   prev_outs is not None)),
        scratch_shapes=[pltpu.VMEM((POOL_HALO + tb, W_B), F32),
                        pltpu.VMEM((rows, W_B), F32)],
        input_output_aliases=aliases,
        compiler_params=_params("arbitrary", "arbitrary"),
        name=name,
    )(*args, *a_args)
    return z, (p_out,)


def _t5_bucket_table(bq):
    span = WINDOW + bq
    dist = np.arange(bq)[:, None] + WINDOW - np.arange(span)[None, :]
    max_exact = NUM_BUCKETS // 2
    d = np.maximum(dist, 0).astype(np.float32)
    large = max_exact + (np.log(np.maximum(d, np.float32(1.0)) / np.float32(max_exact))
                         / np.float32(math.log(MAX_DISTANCE / max_exact))
                         * np.float32(NUM_BUCKETS - max_exact)).astype(np.int32)
    large = np.minimum(large, NUM_BUCKETS - 1)
    bucket = np.where(dist < max_exact, np.maximum(dist, 0), large)
    valid = (dist >= 0) & (dist < WINDOW)
    return np.where(valid, bucket, -1).astype(np.int32)


def _attention_kernel(*refs, bq, nbat, has_cache, mask_first, layer, has_prev):
    refs = list(refs)
    (sinks_ref, relb_ref, q_ref, kc_ref, vc_ref, kp_ref, vp_ref, g_ref, bucket_ref) = refs[:9]
    refs = refs[9:]
    if has_prev:
        del refs[:2]
    z_ref, wk_ref, wv_ref, bias_scr, qbd_scr, s_scr, p_scr, o_scr, l_scr = refs
    span = WINDOW + bq
    hrows = H_C * bq
    grows = G_C * bq
    chunk = min(hrows, SOFTMAX_ROWS)
    one_matmul = hrows <= SOFTMAX_ROWS
    fmin = jnp.finfo(F32).min
    n = pl.program_id(1)

    @pl.when((pl.program_id(0) == 0) & (n == 0))
    def _():
        bucket = bucket_ref[...]
        col = lax.broadcasted_iota(jnp.int32, (bq, span), 1)
        for h in range(H_C):
            acc = jnp.where(bucket < 0, fmin, 0.0)
            for b in range(NUM_BUCKETS):
                acc = jnp.where(bucket == b, relb_ref[b, h], acc)
            acc = jnp.where(col == 0, sinks_ref[h], acc)
            bias_scr[0, h * bq:(h + 1) * bq, :] = acc
            if mask_first:
                bias_scr[1, h * bq:(h + 1) * bq, :] = jnp.where((col < WINDOW) & (col > 0), fmin, acc)

    bsel = jnp.where(n == 0, 1, 0) if mask_first else 0
    lane_kv = lax.broadcasted_iota(jnp.int32, (1, KVW), 1) // HD_C
    low_half = lax.broadcasted_iota(jnp.int32, (1, 2 * HD_C), 1) < HD_C
    row0 = lax.broadcasted_iota(jnp.int32, (WINDOW, KVW), 0) == 0
    ones = jnp.ones((span, DENOM_W), BF16)
    for bi in range(nbat):
        rows = slice(bi * bq, (bi + 1) * bq)
        base = bi * hrows
        kc = kc_ref[rows, :]
        vc = vc_ref[rows, :]
        kp = kp_ref[bi] if has_cache else kp_ref[...]
        vp = vp_ref[bi] if has_cache else vp_ref[...]
        kx = jnp.concatenate([jnp.where(row0, 0.0, kp), kc], axis=0).astype(BF16)
        vx = jnp.concatenate([jnp.where(row0, 0.0, vp), vc], axis=0).astype(BF16)
        q = q_ref[rows, :] * (HD_C ** -0.5)
        for kk in range(KV_C):
            qbd_scr[base + kk * grows:base + (kk + 1) * grows, :] = jnp.concatenate(
                [jnp.where(lane_kv == kk, q[:, g * KVW:(g + 1) * KVW], 0.0) for g in range(G_C)],
                axis=0).astype(BF16)
        s_scr[base:base + hrows, :] = lax.dot_general(
            qbd_scr[base:base + hrows, :], kx, (((1,), (1,)), ((), ())), preferred_element_type=F32)
        for c in range(hrows // chunk):
            rs = slice(base + c * chunk, base + (c + 1) * chunk)
            s = s_scr[rs, :] + bias_scr[bsel, c * chunk:(c + 1) * chunk, :]
            m = jnp.max(s, axis=-1, keepdims=True)
            p_scr[rs, :] = jnp.exp(s - m).astype(BF16)
        if one_matmul:
            o3 = jnp.dot(p_scr[base:base + hrows, :], jnp.concatenate([vx, ones], axis=1),
                         preferred_element_type=F32)
            for kk in range(KV_C):
                o_scr[base + kk * grows:base + (kk + 1) * grows, :] = (
                    o3[kk * grows:(kk + 1) * grows, (kk // 2) * 2 * HD_C:(kk // 2 + 1) * 2 * HD_C])
            l_scr[base:base + hrows, :] = o3[:, KVW:KVW + DENOM_W]
        else:
            for kk in range(KV_C):
                rk = slice(base + kk * grows, base + (kk + 1) * grows)
                slab = vx[:, (kk // 2) * 2 * HD_C:(kk // 2 + 1) * 2 * HD_C]
                vk = jnp.where(low_half if kk % 2 == 0 else ~low_half, slab, jnp.zeros_like(slab))
                ol = jnp.dot(p_scr[rk, :], jnp.concatenate([vk, ones], axis=1),
                             preferred_element_type=F32)
                o_scr[rk, :] = ol[:, :2 * HD_C]
                l_scr[rk, :] = ol[:, 2 * HD_C:]
        for g in range(G_C):
            halves = []
            for half in range(2):
                ra, rb = (slice(base + kk * grows + g * bq, base + kk * grows + (g + 1) * bq)
                          for kk in (2 * half, 2 * half + 1))
                num = jnp.where(low_half, o_scr[ra, :], o_scr[rb, :])
                den = jnp.where(low_half, l_scr[ra, :], l_scr[rb, :])
                halves.append(num * (1.0 / den))
            og = jnp.concatenate(halves, axis=1)
            cols = slice(g * KVW, (g + 1) * KVW)
            z_ref[rows, cols] = (og * _silu(g_ref[rows, cols])).astype(z_ref.dtype)

        @pl.when(n == pl.num_programs(1) - 1)
        def _():
            wk = kc if bq == WINDOW else jnp.concatenate([kp[bq:, :], kc], axis=0)
            wv = vc if bq == WINDOW else jnp.concatenate([vp[bq:, :], vc], axis=0)
            _state_store(wk_ref, (bi,), wk, layer, has_prev)
            _state_store(wv_ref, (bi,), wv, layer, has_prev)


def _attention(h_att, batch, t, pos0, cache_k, cache_v, layer, prev_outs, sinks, rel_bias, *, name):
    bq = min(ATT_BLOCK, t)
    nb = t // bq
    nbat = 1 if nb > 1 else SMALL_T_ATT_BATCH
    span = WINDOW + bq
    has_cache = cache_k is not None
    assert pos0 == 0 or pos0 >= WINDOW
    mask_first = pos0 == 0
    rows = nbat * bq
    row = lambda b, n, *_: b * nb + n
    in_specs = [
        pl.BlockSpec((rows, W_C), lambda b, n, *_: (row(b, n), OFF_QC // W_C)),
        pl.BlockSpec((rows, KVW), lambda b, n, *_: (row(b, n), OFF_KC // KVW)),
        pl.BlockSpec((rows, KVW), lambda b, n, *_: (row(b, n), OFF_VC // KVW)),
    ]
    args = [h_att, h_att, h_att]
    if has_cache:
        assert nb == 1
        in_specs += [pl.BlockSpec((None, nbat, WINDOW, KVW), lambda b, n, *_: (layer, b, 0, 0)),
                     pl.BlockSpec((None, nbat, WINDOW, KVW), lambda b, n, *_: (layer, b, 0, 0))]
        args += [cache_k, cache_v]
    else:
        assert bq == WINDOW and nbat == 1
        prev = lambda b, n, *_: jnp.maximum(row(b, n) - 1, 0)
        in_specs += [pl.BlockSpec((WINDOW, KVW), lambda b, n, *_: (prev(b, n), OFF_KC // KVW)),
                     pl.BlockSpec((WINDOW, KVW), lambda b, n, *_: (prev(b, n), OFF_VC // KVW))]
        args += [h_att, h_att]
    in_specs += [
        pl.BlockSpec((rows, W_C), lambda b, n, *_: (row(b, n), OFF_GC // W_C)),
        pl.BlockSpec((bq, span), lambda b, n, *_: (0, 0)),
    ]
    args += [h_att, jnp.asarray(_t5_bucket_table(bq))]
    a_args, a_specs, aliases = _alias_args(prev_outs, 2 + len(args))
    srows = nbat * H_C * bq
    z, wk, wv = pl.pallas_call(
        functools.partial(_attention_kernel, bq=bq, nbat=nbat, has_cache=has_cache,
                          mask_first=mask_first, layer=layer, has_prev=prev_outs is not None),
        out_shape=(jax.ShapeDtypeStruct((batch * t, W_C), BF16),
                   jax.ShapeDtypeStruct((DEPTH, batch, WINDOW, KVW), F32),
                   jax.ShapeDtypeStruct((DEPTH, batch, WINDOW, KVW), F32)),
        grid_spec=pltpu.PrefetchScalarGridSpec(
            num_scalar_prefetch=2,
            grid=(batch // nbat, nb),
            in_specs=in_specs + a_specs,
            out_specs=(pl.BlockSpec((rows, W_C), lambda b, n, *_: (row(b, n), 0)),
                       _state_out_spec((nbat, WINDOW, KVW), lambda b, n, *_: (b, 0, 0), layer,
                                       prev_outs is not None),
                       _state_out_spec((nbat, WINDOW, KVW), lambda b, n, *_: (b, 0, 0), layer,
                                       prev_outs is not None)),
            scratch_shapes=[pltpu.VMEM((2 if mask_first else 1, H_C * bq, span), F32),
                            pltpu.VMEM((srows, KVW), BF16),
                            pltpu.VMEM((srows, span), F32),
                            pltpu.VMEM((srows, span), BF16),
                            pltpu.VMEM((srows, 2 * HD_C), F32),
                            pltpu.VMEM((srows, DENOM_W), F32)]),
        input_output_aliases=aliases,
        compiler_params=_params("arbitrary", "arbitrary"),
        name=name,
    )(sinks, rel_bias, *args, *a_args)
    return z, (wk, wv)


def _out_kernel(x_ref, za_ref, zb_ref, zc_ref, ma_ref, mb_ref, mc_ref, wa_ref, wb_ref, wc_ref,
                wo_ref, lng_ref, lnb_ref, *rest, alpha, emit_bf16):
    if emit_bf16:
        y_ref, yb_ref, merged_scr, r_scr = rest
    else:
        y_ref, merged_scr, r_scr = rest

    @pl.when(pl.program_id(0) == 0)
    def _():
        r_scr[...] = jnp.zeros_like(r_scr)

    lng, lnb = lng_ref[...], lnb_ref[...]
    for rc in range(r_scr.shape[0] // LN_ROWS):
        rs = slice(rc * LN_ROWS, (rc + 1) * LN_ROWS)
        r = r_scr[rs, :]
        mu = jnp.mean(r, axis=-1, keepdims=True)
        d = r - mu
        var = jnp.mean(d * d, axis=-1, keepdims=True)
        out = d * lax.rsqrt(var + LN_EPS) * lng + lnb
        y_ref[rs, :] = out
        if emit_bf16:
            yb_ref[rs, :] = out.astype(BF16)

    za, zb, zc = za_ref[...], zb_ref[...], zc_ref[...]
    for c in range(D_MODEL // OUT_STAGE_TN):
        cs = slice(c * OUT_STAGE_TN, (c + 1) * OUT_STAGE_TN)
        merged = (_sigmoid(ma_ref[:, cs].astype(F32))
                  * jnp.dot(za, wa_ref[:, cs], preferred_element_type=F32))
        merged += (_sigmoid(mb_ref[:, cs].astype(F32))
                   * jnp.dot(zb, wb_ref[:, cs], preferred_element_type=F32))
        merged += (_sigmoid(mc_ref[:, cs].astype(F32))
                   * jnp.dot(zc, wc_ref[:, cs], preferred_element_type=F32))
        merged_scr[:, cs] = merged.astype(BF16)
    r_scr[...] = alpha * x_ref[...] + jnp.dot(merged_scr[...], wo_ref[...],
                                              preferred_element_type=F32)


def _out_stage(x, za, zb, zc, h_gate, wa, wb, wc, wo, lng, lnb, alpha, *, emit_bf16, name):
    m = x.shape[0]
    tm = OUT_STAGE_TM
    nsteps = m // tm
    cur = lambda i: jnp.minimum(i, nsteps - 1)
    rowblk = lambda w: pl.BlockSpec((tm, w), lambda i: (cur(i), 0))
    gate = lambda j: pl.BlockSpec((tm, D_MODEL), lambda i: (cur(i), j))
    outblk = pl.BlockSpec((tm, D_MODEL), lambda i: (jnp.maximum(i - 1, 0), 0))
    const = lambda a: pl.BlockSpec(a.shape, lambda i: (0,) * a.ndim, pipeline_mode=pl.Buffered(1))
    out_shape = [jax.ShapeDtypeStruct((m, D_MODEL), F32)]
    if emit_bf16:
        out_shape.append(jax.ShapeDtypeStruct((m, D_MODEL), BF16))
    outs = pl.pallas_call(
        functools.partial(_out_kernel, alpha=alpha, emit_bf16=emit_bf16),
        out_shape=out_shape,
        grid=(nsteps + 1,),
        in_specs=[rowblk(D_MODEL), rowblk(W_A), rowblk(W_B), rowblk(W_C),
                  gate(0), gate(1), gate(2),
                  const(wa), const(wb), const(wc), const(wo), const(lng), const(lnb)],
        out_specs=[outblk] * len(out_shape),
        scratch_shapes=[pltpu.VMEM((tm, D_MODEL), BF16), pltpu.VMEM((tm, D_MODEL), F32)],
        compiler_params=_params("arbitrary"),
        name=name,
    )(x, za, zb, zc, h_gate, h_gate, h_gate, wa, wb, wc, wo, lng, lnb)
    return (outs[0], outs[1]) if emit_bf16 else (outs[0], None)


def _mixer_layer(x, xb, batch, t, pos0, states, layer, prev, w, w_in, alpha, tag):
    st_ret, st_k, st_v, st_pool = states if states is not None else (None,) * 4
    pv_ret, pv_win, pv_pool = prev if prev is not None else (None,) * 3
    if xb is None:
        h_att, xb = _in_proj(x, w["w_att"], N_ATT, emit_xb=True, name=f"in_proj_att_{tag}")
    else:
        h_att = _in_proj(xb, w["w_att"], N_ATT, name=f"in_proj_att_{tag}")
    h_ab = _in_proj(xb, w_in, N_AB, layer=layer, col0=0, name=f"in_proj_ab_{tag}")
    h_gate = _in_proj(xb, w_in, N_GATE, layer=layer, col0=REF_GATE, out_dtype=BF16,
                      name=f"in_proj_gate_{tag}")
    za, o_ret = _retention(h_ab, batch, t, pos0, st_ret, layer, pv_ret, name=f"retention_{tag}")
    zb, o_pool = _pool(h_ab, batch, t, pos0, st_pool, layer, pv_pool, w["w_pool_map"],
                       w["pool_scale"], name=f"pool_{tag}")
    zc, o_win = _attention(h_att, batch, t, pos0, st_k, st_v, layer, pv_win, w["sinks"],
                           w["rel_bias"], name=f"attention_{tag}")
    y, yb = _out_stage(x, za, zb, zc, h_gate, w["w_ret_o"], w["w_pool_o"], w["w_att_o"], w["w_out"],
                       w["ln_g"], w["ln_b"], alpha, emit_bf16=layer + 1 < DEPTH,
                       name=f"out_stage_{tag}")
    return y, yb, (o_ret, o_win, o_pool)


def _regroup_heads(w, axis):
    shp = w.shape
    w = w.reshape(shp[:axis] + (KV_C, G_C, HD_C) + shp[axis + 1:])
    w = jnp.swapaxes(w, axis, axis + 1)
    return w.reshape(shp)


def _prep_w_att(w_in, l):
    cols = lambda c0, n: lax.slice(w_in, (l, 0, c0), (l + 1, D_MODEL, c0 + n)).reshape(D_MODEL, n)
    return jnp.concatenate([
        _regroup_heads(cols(REF_QC, W_C), 1),
        _regroup_heads(cols(REF_GC, W_C), 1),
        cols(REF_KC, KVW),
        cols(REF_VC, KVW)], axis=1)


def kernel(x_prompt, x_sample, state_ret, cache_win_k, cache_win_v, state_pool, w_in, w_ret_o,
           w_pool_map, pool_scale, w_pool_o, attn_sinks, w_att_o, w_out, ln_g, ln_b, rel_bias):
    alpha = (2.0 * DEPTH) ** 0.25
    bp, tp, _ = x_prompt.shape
    bs, ts, _ = x_sample.shape
    xp = x_prompt.reshape(bp * tp, D_MODEL)
    xs = x_sample.reshape(bs * ts, D_MODEL)
    xpb = xsb = None
    ck = cache_win_k.reshape(DEPTH, bs, WINDOW, KVW)
    cv = cache_win_v.reshape(DEPTH, bs, WINDOW, KVW)
    sample_states = (state_ret, ck, cv, state_pool)
    out_p = out_s = None
    for l in range(DEPTH):
        w = {
            "w_att": _prep_w_att(w_in, l),
            "w_ret_o": w_ret_o[l].astype(BF16),
            "w_pool_map": w_pool_map[l].astype(BF16),
            "pool_scale": pool_scale[l].reshape(1, W_B),
            "w_pool_o": w_pool_o[l].astype(BF16),
            "sinks": attn_sinks[l],
            "w_att_o": _regroup_heads(w_att_o[l], 0).astype(BF16),
            "w_out": w_out[l].astype(BF16),
            "ln_g": ln_g[l].reshape(1, D_MODEL),
            "ln_b": ln_b[l].reshape(1, D_MODEL),
            "rel_bias": rel_bias,
        }
        xp, xpb, out_p = _mixer_layer(xp, xpb, bp, tp, 0, None, l, out_p, w, w_in, alpha, f"p{l}")
        xs, xsb, out_s = _mixer_layer(xs, xsb, bs, ts, PAST_LEN, sample_states, l, out_s, w, w_in,
                                      alpha, f"s{l}")
    (ret_p,), (kp, vp), (pp,) = out_p
    (ret_s,), (ks, vs), (ps,) = out_s
    win = lambda a, b: a.reshape(DEPTH, b, WINDOW, KV_C, HD_C)
    return (xp.reshape(bp, tp, D_MODEL), xs.reshape(bs, ts, D_MODEL), ret_p, ret_s,
            win(kp, bp), win(ks, bs), win(vp, bp), win(vs, bs), pp, ps)
```

```python
import functools
import math

import jax
import jax.numpy as jnp
import numpy as np
from jax import lax
from jax.experimental import pallas as pl
from jax.experimental.pallas import tpu as pltpu

D_MODEL = 2048
DEPTH = 2
PAST_LEN = 8192
H_A, DK_A, DV_A = 4, 128, 256
W_A = H_A * DV_A
RET_CHUNK = 128
ROPE_BASE = 10000.0
N_POOL_GROUPS = 4
W_B = 1024
GW_B = W_B // N_POOL_GROUPS
POOL_WINDOWS = (2, 4, 8, 16)
POOL_BUF = 15
POOL_HALO = 16
H_C, KV_C, HD_C = 16, 4, 64
G_C = H_C // KV_C
W_C = H_C * HD_C
KVW = KV_C * HD_C
WINDOW = 128
ATT_BLOCK = 128
SOFTMAX_ROWS = 128
DENOM_W = 2 * HD_C
NUM_BUCKETS = 32
MAX_DISTANCE = 128
LN_EPS = 1e-5
RMS_EPS = 1e-6

REF_QC, REF_KC, REF_VC, REF_GC = 5120, 6144, 6400, 6656
N_AB = 5120
N_ATT = 2560
REF_GATE = 7680
N_GATE = 3 * D_MODEL
OFF_QA, OFF_KA, OFF_VA, OFF_GA, OFF_UB, OFF_GB = 0, 512, 1024, 2048, 3072, 4096
OFF_QC, OFF_GC, OFF_KC, OFF_VC = 0, 1024, 2048, 2304

F32 = jnp.float32
BF16 = jnp.bfloat16
VMEM_LIMIT = 56 * 1024 * 1024

IN_PROJ_TM = 1024
IN_PROJ_TM_F32_X = 512
IN_PROJ_TN = {N_AB: 1280, N_ATT: 2560, N_GATE: 1536}
OUT_STAGE_TM = 256
OUT_STAGE_TN = 512
LN_ROWS = 16
POOL_ROWS = 1024
SMALL_T_POOL_BATCH = 16
ATT_BLOCKS_PER_STEP = 2
SMALL_T_ATT_BATCH = 8
RET_ROWS = 512
SMALL_T_RET_BATCH = 4


def _sigmoid(x):
    return 1.0 / (1.0 + jnp.exp(-x))


def _silu(x):
    return x * _sigmoid(x)


def _params(*sem):
    return pltpu.CompilerParams(dimension_semantics=sem, vmem_limit_bytes=VMEM_LIMIT)


def _alias_args(prev_outs, n_in):
    if prev_outs is None:
        return [], [], {}
    specs = [pl.BlockSpec(memory_space=pl.ANY) for _ in prev_outs]
    return list(prev_outs), specs, {n_in + i: 1 + i for i in range(len(prev_outs))}


def _state_out_spec(block, index_map, layer, has_prev):
    if has_prev:
        return pl.BlockSpec((None,) + block, lambda *a: (layer,) + index_map(*a))
    return pl.BlockSpec((DEPTH,) + block, lambda *a: (0,) + index_map(*a))


def _state_store(ref, idx, value, layer, has_prev):
    if has_prev:
        ref[idx] = value
    else:
        for l in range(DEPTH):
            ref[(l,) + idx] = value if l == layer else jnp.zeros_like(value)


def _matmul_kernel(x_ref, w_ref, o_ref, *rest, cast_w, emit_xb):
    rest = list(rest)
    xb_ref = rest.pop(0) if emit_xb else None
    if cast_w:
        wb_scr, = rest

        @pl.when(pl.program_id(1) == 0)
        def _():
            wb_scr[...] = w_ref[...].astype(BF16)
        w = wb_scr[...]
    else:
        w = w_ref[...]
    x = x_ref[...].astype(BF16)
    if emit_xb:
        xb_ref[...] = x
    o_ref[...] = jnp.dot(x, w, preferred_element_type=F32).astype(o_ref.dtype)


def _in_proj(x, w, n, *, layer=None, col0=0, emit_xb=False, out_dtype=F32, name):
    m, k = x.shape
    tm, tn = (IN_PROJ_TM_F32_X if x.dtype == F32 else IN_PROJ_TM), IN_PROJ_TN[n]
    cast_w = w.dtype == F32
    if cast_w:
        w_spec = pl.BlockSpec((None, k, tn), lambda j, i: (layer, 0, col0 // tn + j))
    else:
        w_spec = pl.BlockSpec((k, tn), lambda j, i: (0, j))
    assert not emit_xb or n == tn
    out_shape = [jax.ShapeDtypeStruct((m, n), out_dtype)]
    out_specs = [pl.BlockSpec((tm, tn), lambda j, i: (i, j))]
    if emit_xb:
        out_shape.append(jax.ShapeDtypeStruct((m, k), BF16))
        out_specs.append(pl.BlockSpec((tm, k), lambda j, i: (i, 0)))
    outs = pl.pallas_call(
        functools.partial(_matmul_kernel, cast_w=cast_w, emit_xb=emit_xb),
        out_shape=out_shape,
        grid=(n // tn, m // tm),
        in_specs=[pl.BlockSpec((tm, k), lambda j, i: (i, 0)), w_spec],
        out_specs=out_specs,
        scratch_shapes=[pltpu.VMEM((k, tn), BF16)] if cast_w else [],
        compiler_params=_params("arbitrary", "arbitrary"),
        name=name,
    )(x, w)
    return outs if emit_xb else outs[0]


def _retention_kernel(*refs, chunk, nchunk, nbat, has_state, layer, has_prev):
    refs = list(refs)
    (q_ref, k_ref, v_ref, g_ref, cos_ref, sin_ref, dmask_ref, qdec_ref, kdec_ref, cdec_ref) = refs[:10]
    refs = refs[10:]
    s0_ref = refs.pop(0) if has_state else None
    if has_prev:
        refs.pop(0)
    z_ref, sout_ref, s_scr = refs
    c = pl.program_id(1)

    @pl.when(c == 0)
    def _():
        if has_state:
            s_scr[...] = s0_ref[...]
        else:
            s_scr[...] = jnp.zeros_like(s_scr)

    for bi in range(nbat):
        for hd in range(H_A):
            s = s_scr[bi, hd]
            for ci in range(nchunk):
                rows = slice((bi * nchunk + ci) * chunk, (bi * nchunk + ci + 1) * chunk)
                trow = slice(ci * chunk, (ci + 1) * chunk)
                cos = cos_ref[trow, :]
                sin = sin_ref[trow, :]
                q = q_ref[rows, hd * DK_A:(hd + 1) * DK_A]
                k = k_ref[rows, hd * DK_A:(hd + 1) * DK_A]
                v = v_ref[rows, hd * DV_A:(hd + 1) * DV_A].astype(BF16)
                qr = q * cos + pltpu.roll(q, DK_A // 2, 1) * sin
                kr = (k * cos + pltpu.roll(k, DK_A // 2, 1) * sin) * (DK_A ** -0.5)
                qb = qr.astype(BF16)
                sc = lax.dot_general(qb, kr.astype(BF16), (((1,), (1,)), ((), ())),
                                     preferred_element_type=F32) * dmask_ref[hd]
                o = (jnp.dot(sc.astype(BF16), v, preferred_element_type=F32)
                     + jnp.dot(qb, s.astype(BF16), preferred_element_type=F32) * qdec_ref[hd])
                kd = (kr * kdec_ref[hd]).astype(BF16)
                s = s * cdec_ref[hd] + lax.dot_general(
                    kd, v, (((0,), (0,)), ((), ())), preferred_element_type=F32)
                o = o * lax.rsqrt(jnp.mean(o * o, axis=-1, keepdims=True) + RMS_EPS)
                g = g_ref[rows, hd * DV_A:(hd + 1) * DV_A]
                z_ref[rows, hd * DV_A:(hd + 1) * DV_A] = (o * _silu(g)).astype(z_ref.dtype)
            s_scr[bi, hd] = s

    @pl.when(c == pl.num_programs(1) - 1)
    def _():
        _state_store(sout_ref, (Ellipsis,), s_scr[...], layer, has_prev)


def _retention_tables(pos0, t, chunk):
    half = DK_A // 2
    inv = ROPE_BASE ** (-jnp.arange(half, dtype=F32) / half)
    pos = pos0 + jnp.arange(t, dtype=F32)
    ang = pos[:, None] * inv[None, :]
    cos, sin = jnp.cos(ang), jnp.sin(ang)
    cos2 = jnp.concatenate([cos, cos], -1)
    sin2 = jnp.concatenate([-sin, sin], -1)
    lg = jnp.log1p(-jnp.exp2(-5.0 - jnp.arange(H_A, dtype=F32)))
    idx = jnp.arange(chunk, dtype=F32)
    diff = idx[:, None] - idx[None, :]
    dmask = jnp.where(diff >= 0, jnp.exp(lg[:, None, None] * jnp.maximum(diff, 0.0)), 0.0)
    qdec = jnp.exp(lg[:, None] * (idx[None, :] + 1.0))
    kdec = jnp.exp(lg[:, None] * (chunk - 1.0 - idx[None, :]))
    cdec = jnp.exp(lg * chunk)
    qdec = jnp.broadcast_to(qdec[:, :, None], (H_A, chunk, DV_A))
    kdec = jnp.broadcast_to(kdec[:, :, None], (H_A, chunk, DK_A))
    cdec = jnp.broadcast_to(cdec[:, None, None], (H_A, 1, DV_A))
    return cos2, sin2, dmask, qdec, kdec, cdec


def _retention(h_ab, batch, t, pos0, state, layer, prev_outs, *, name):
    chunk = min(RET_CHUNK, t)
    if t > chunk:
        nbat, nchunk = 1, RET_ROWS // chunk
    else:
        nbat, nchunk = SMALL_T_RET_BATCH, 1
    rows = nbat * nchunk * chunk
    nsteps = t // (nchunk * chunk)
    has_state = state is not None
    tables = _retention_tables(pos0, t, chunk)
    row = lambda b, c: b * nsteps + c
    in_specs = [
        pl.BlockSpec((rows, H_A * DK_A), lambda b, c: (row(b, c), OFF_QA // (H_A * DK_A))),
        pl.BlockSpec((rows, H_A * DK_A), lambda b, c: (row(b, c), OFF_KA // (H_A * DK_A))),
        pl.BlockSpec((rows, W_A), lambda b, c: (row(b, c), OFF_VA // W_A)),
        pl.BlockSpec((rows, W_A), lambda b, c: (row(b, c), OFF_GA // W_A)),
        pl.BlockSpec((nchunk * chunk, DK_A), lambda b, c: (c, 0)),
        pl.BlockSpec((nchunk * chunk, DK_A), lambda b, c: (c, 0)),
        pl.BlockSpec((H_A, chunk, chunk), lambda b, c: (0, 0, 0)),
        pl.BlockSpec((H_A, chunk, DV_A), lambda b, c: (0, 0, 0)),
        pl.BlockSpec((H_A, chunk, DK_A), lambda b, c: (0, 0, 0)),
        pl.BlockSpec((H_A, 1, DV_A), lambda b, c: (0, 0, 0)),
    ]
    args = [h_ab, h_ab, h_ab, h_ab, *tables]
    if has_state:
        in_specs.append(pl.BlockSpec((None, nbat, H_A, DK_A, DV_A), lambda b, c: (layer, b, 0, 0, 0)))
        args.append(state)
    a_args, a_specs, aliases = _alias_args(prev_outs, len(args))
    z, s_out = pl.pallas_call(
        functools.partial(_retention_kernel, chunk=chunk, nchunk=nchunk, nbat=nbat,
                          has_state=has_state, layer=layer, has_prev=prev_outs is not None),
        out_shape=(jax.ShapeDtypeStruct((batch * t, W_A), BF16),
                   jax.ShapeDtypeStruct((DEPTH, batch, H_A, DK_A, DV_A), F32)),
        grid=(batch // nbat, nsteps),
        in_specs=in_specs + a_specs,
        out_specs=(pl.BlockSpec((rows, W_A), lambda b, c: (row(b, c), 0)),
                   _state_out_spec((nbat, H_A, DK_A, DV_A), lambda b, c: (b, 0, 0, 0), layer,
                                   prev_outs is not None)),
        scratch_shapes=[pltpu.VMEM((nbat, H_A, DK_A, DV_A), F32)],
        input_output_aliases=aliases,
        compiler_params=_params("arbitrary", "arbitrary"),
        name=name,
    )(*args, *a_args)
    return z, (s_out,)


def _pool_kernel(*refs, nb, tb, pos0, has_state, layer, has_prev):
    refs = list(refs)
    u_ref, g_ref, halo_ref, wmap_ref, scale_ref = refs[:5]
    refs = refs[5:]
    if has_prev:
        refs.pop(0)
    z_ref, pout_ref, ext_scr, p_scr = refs
    ti = pl.program_id(1)
    for bi in range(nb):
        rows = pl.ds(bi * tb, tb)
        if has_state:
            ext_scr[0:1, :] = jnp.zeros((1, W_B), F32)
            ext_scr[1:POOL_HALO, :] = halo_ref[bi]
        else:
            ext_scr[0:POOL_HALO, :] = jnp.where(ti == 0, 0.0, halo_ref[...])
        ext_scr[POOL_HALO:POOL_HALO + tb, :] = u_ref[rows, :]
        t_idx = pos0 + ti * tb + lax.broadcasted_iota(jnp.int32, (tb, GW_B), 0)
        for gi, w in enumerate(POOL_WINDOWS):
            cols = slice(gi * GW_B, (gi + 1) * GW_B)
            acc = ext_scr[:, cols]
            shift = 1
            while shift < w:
                acc = acc + pltpu.roll(acc, shift, 0)
                shift *= 2
            cnt = jnp.minimum(t_idx + 1, w).astype(F32)
            u = ext_scr[POOL_HALO:POOL_HALO + tb, cols]
            p_scr[rows, cols] = acc[POOL_HALO:POOL_HALO + tb] / cnt - u

        @pl.when(ti == pl.num_programs(1) - 1)
        def _():
            _state_store(pout_ref, (bi,), ext_scr[tb + 1:tb + POOL_HALO, :], layer, has_prev)

    for gi in range(N_POOL_GROUPS):
        cols = slice(gi * GW_B, (gi + 1) * GW_B)
        pm = jnp.dot(p_scr[:, cols].astype(BF16), wmap_ref[gi], preferred_element_type=F32)
        pm = pm * scale_ref[:, cols]
        z_ref[:, cols] = (pm * _silu(g_ref[:, cols])).astype(z_ref.dtype)


def _pool(h_ab, batch, t, pos0, state, layer, prev_outs, wmap, scale, *, name):
    has_state = state is not None
    if t >= POOL_ROWS:
        nb, tb = 1, POOL_ROWS
    else:
        nb, tb = SMALL_T_POOL_BATCH, t
    nt = t // tb
    rows = nb * tb
    rblk = lambda b, i: b * nt + i
    in_specs = [pl.BlockSpec((rows, W_B), lambda b, i: (rblk(b, i), OFF_UB // W_B)),
                pl.BlockSpec((rows, W_B), lambda b, i: (rblk(b, i), OFF_GB // W_B))]
    args = [h_ab, h_ab]
    if has_state:
        assert nt == 1
        in_specs.append(pl.BlockSpec((None, nb, POOL_BUF, W_B), lambda b, i: (layer, b, 0, 0)))
        args.append(state)
    else:
        assert nb == 1
        per = tb // POOL_HALO
        in_specs.append(pl.BlockSpec(
            (POOL_HALO, W_B), lambda b, i: (jnp.maximum(rblk(b, i) * per - 1, 0), OFF_UB // W_B)))
        args.append(h_ab)
    in_specs += [pl.BlockSpec((N_POOL_GROUPS, GW_B, GW_B), lambda b, i: (0, 0, 0)),
                 pl.BlockSpec((1, W_B), lambda b, i: (0, 0))]
    args += [wmap, scale]
    a_args, a_specs, aliases = _alias_args(prev_outs, len(args))
    z, p_out = pl.pallas_call(
        functools.partial(_pool_kernel, nb=nb, tb=tb, pos0=pos0, has_state=has_state,
                          layer=layer, has_prev=prev_outs is not None),
        out_shape=(jax.ShapeDtypeStruct((batch * t, W_B), BF16),
                   jax.ShapeDtypeStruct((DEPTH, batch, POOL_BUF, W_B), F32)),
        grid=(batch // nb, nt),
        in_specs=in_specs + a_specs,
        out_specs=(pl.BlockSpec((rows, W_B), lambda b, i: (rblk(b, i), 0)),
                   _state_out_spec((nb, POOL_BUF, W_B), lambda b, i: (b, 0, 0), layer,
                                   prev_outs is not None)),
        scratch_shapes=[pltpu.VMEM((POOL_HALO + tb, W_B), F32),
                        pltpu.VMEM((rows, W_B), F32)],
        input_output_aliases=aliases,
        compiler_params=_params("arbitrary", "arbitrary"),
        name=name,
    )(*args, *a_args)
    return z, (p_out,)


def _t5_bucket_table(bq):
    span = WINDOW + bq
    dist = np.arange(bq)[:, None] + WINDOW - np.arange(span)[None, :]
    max_exact = NUM_BUCKETS // 2
    d = np.maximum(dist, 0).astype(np.float32)
    large = max_exact + (np.log(np.maximum(d, np.float32(1.0)) / np.float32(max_exact))
                         / np.float32(math.log(MAX_DISTANCE / max_exact))
                         * np.float32(NUM_BUCKETS - max_exact)).astype(np.int32)
    large = np.minimum(large, NUM_BUCKETS - 1)
    bucket = np.where(dist < max_exact, np.maximum(dist, 0), large)
    valid = (dist >= 0) & (dist < WINDOW)
    return np.where(valid, bucket, -1).astype(np.int32)


def _attention_kernel(*refs, bq, nbat, has_cache, mask_first, layer, has_prev):
    refs = list(refs)
    (sinks_ref, relb_ref, q_ref, kc_ref, vc_ref, kp_ref, vp_ref, g_ref, bucket_ref) = refs[:9]
    refs = refs[9:]
    if has_prev:
        del refs[:2]
    z_ref, wk_ref, wv_ref, bias_scr, qbd_scr, s_scr, p_scr, o_scr, l_scr = refs
    span = WINDOW + bq
    hrows = H_C * bq
    grows = G_C * bq
    chunk = min(hrows, SOFTMAX_ROWS)
    one_matmul = hrows <= SOFTMAX_ROWS
    fmin = jnp.finfo(F32).min
    n = pl.program_id(1)

    @pl.when((pl.program_id(0) == 0) & (n == 0))
    def _():
        bucket = bucket_ref[...]
        col = lax.broadcasted_iota(jnp.int32, (bq, span), 1)
        for h in range(H_C):
            acc = jnp.where(bucket < 0, fmin, 0.0)
            for b in range(NUM_BUCKETS):
                acc = jnp.where(bucket == b, relb_ref[b, h], acc)
            acc = jnp.where(col == 0, sinks_ref[h], acc)
            bias_scr[0, h * bq:(h + 1) * bq, :] = acc
            if mask_first:
                bias_scr[1, h * bq:(h + 1) * bq, :] = jnp.where((col < WINDOW) & (col > 0), fmin, acc)

    lane_kv = lax.broadcasted_iota(jnp.int32, (1, KVW), 1) // HD_C
    low_half = lax.broadcasted_iota(jnp.int32, (1, 2 * HD_C), 1) < HD_C
    row0 = lax.broadcasted_iota(jnp.int32, (WINDOW, KVW), 0) == 0
    ones = jnp.ones((span, DENOM_W), BF16)
    kc = vc = None
    for bi in range(nbat):
        rows = slice(bi * bq, (bi + 1) * bq)
        base = bi * hrows
        if has_cache:
            kp, vp = kp_ref[bi], vp_ref[bi]
        elif bi == 0:
            kp, vp = kp_ref[...], vp_ref[...]
        else:
            kp, vp = kc, vc
        bsel = jnp.where(n == 0, 1, 0) if (mask_first and bi == 0) else 0
        kc = kc_ref[rows, :]
        vc = vc_ref[rows, :]
        kx = jnp.concatenate([jnp.where(row0, 0.0, kp), kc], axis=0).astype(BF16)
        vx = jnp.concatenate([jnp.where(row0, 0.0, vp), vc], axis=0).astype(BF16)
        q = q_ref[rows, :] * (HD_C ** -0.5)
        for kk in range(KV_C):
            qbd_scr[base + kk * grows:base + (kk + 1) * grows, :] = jnp.concatenate(
                [jnp.where(lane_kv == kk, q[:, g * KVW:(g + 1) * KVW], 0.0) for g in range(G_C)],
                axis=0).astype(BF16)
        s_scr[base:base + hrows, :] = lax.dot_general(
            qbd_scr[base:base + hrows, :], kx, (((1,), (1,)), ((), ())), preferred_element_type=F32)
        for c in range(hrows // chunk):
            rs = slice(base + c * chunk, base + (c + 1) * chunk)
            s = s_scr[rs, :] + bias_scr[bsel, c * chunk:(c + 1) * chunk, :]
            m = jnp.max(s, axis=-1, keepdims=True)
            p_scr[rs, :] = jnp.exp(s - m).astype(BF16)
        if one_matmul:
            o3 = jnp.dot(p_scr[base:base + hrows, :], jnp.concatenate([vx, ones], axis=1),
                         preferred_element_type=F32)
            for kk in range(KV_C):
                o_scr[base + kk * grows:base + (kk + 1) * grows, :] = (
                    o3[kk * grows:(kk + 1) * grows, (kk // 2) * 2 * HD_C:(kk // 2 + 1) * 2 * HD_C])
            l_scr[base:base + hrows, :] = o3[:, KVW:KVW + DENOM_W]
        else:
            for kk in range(KV_C):
                rk = slice(base + kk * grows, base + (kk + 1) * grows)
                slab = vx[:, (kk // 2) * 2 * HD_C:(kk // 2 + 1) * 2 * HD_C]
                vk = jnp.where(low_half if kk % 2 == 0 else ~low_half, slab, jnp.zeros_like(slab))
                ol = jnp.dot(p_scr[rk, :], jnp.concatenate([vk, ones], axis=1),
                             preferred_element_type=F32)
                o_scr[rk, :] = ol[:, :2 * HD_C]
                l_scr[rk, :] = ol[:, 2 * HD_C:]
        for g in range(G_C):
            halves = []
            for half in range(2):
                ra, rb = (slice(base + kk * grows + g * bq, base + kk * grows + (g + 1) * bq)
                          for kk in (2 * half, 2 * half + 1))
                num = jnp.where(low_half, o_scr[ra, :], o_scr[rb, :])
                den = jnp.where(low_half, l_scr[ra, :], l_scr[rb, :])
                halves.append(num * (1.0 / den))
            og = jnp.concatenate(halves, axis=1)
            cols = slice(g * KVW, (g + 1) * KVW)
            z_ref[rows, cols] = (og * _silu(g_ref[rows, cols])).astype(z_ref.dtype)

        if has_cache or bi == nbat - 1:
            @pl.when(n == pl.num_programs(1) - 1)
            def _():
                wk = kc if bq == WINDOW else jnp.concatenate([kp[bq:, :], kc], axis=0)
                wv = vc if bq == WINDOW else jnp.concatenate([vp[bq:, :], vc], axis=0)
                _state_store(wk_ref, (bi if has_cache else 0,), wk, layer, has_prev)
                _state_store(wv_ref, (bi if has_cache else 0,), wv, layer, has_prev)


def _attention(h_att, batch, t, pos0, cache_k, cache_v, layer, prev_outs, sinks, rel_bias, *, name):
    bq = min(ATT_BLOCK, t)
    nb = t // bq
    has_cache = cache_k is not None
    nbat = SMALL_T_ATT_BATCH if has_cache else ATT_BLOCKS_PER_STEP
    nsteps = 1 if has_cache else nb // nbat
    wb = nbat if has_cache else 1
    span = WINDOW + bq
    assert pos0 == 0 or pos0 >= WINDOW
    mask_first = pos0 == 0
    rows = nbat * bq
    row = lambda b, n, *_: b * nsteps + n
    in_specs = [
        pl.BlockSpec((rows, W_C), lambda b, n, *_: (row(b, n), OFF_QC // W_C)),
        pl.BlockSpec((rows, KVW), lambda b, n, *_: (row(b, n), OFF_KC // KVW)),
        pl.BlockSpec((rows, KVW), lambda b, n, *_: (row(b, n), OFF_VC // KVW)),
    ]
    args = [h_att, h_att, h_att]
    if has_cache:
        assert nb == 1
        in_specs += [pl.BlockSpec((None, nbat, WINDOW, KVW), lambda b, n, *_: (layer, b, 0, 0)),
                     pl.BlockSpec((None, nbat, WINDOW, KVW), lambda b, n, *_: (layer, b, 0, 0))]
        args += [cache_k, cache_v]
    else:
        assert bq == WINDOW and nb % nbat == 0
        prev = lambda b, n, *_: jnp.maximum(row(b, n) * nbat - 1, 0)
        in_specs += [pl.BlockSpec((WINDOW, KVW), lambda b, n, *_: (prev(b, n), OFF_KC // KVW)),
                     pl.BlockSpec((WINDOW, KVW), lambda b, n, *_: (prev(b, n), OFF_VC // KVW))]
        args += [h_att, h_att]
    in_specs += [
        pl.BlockSpec((rows, W_C), lambda b, n, *_: (row(b, n), OFF_GC // W_C)),
        pl.BlockSpec((bq, span), lambda b, n, *_: (0, 0)),
    ]
    args += [h_att, jnp.asarray(_t5_bucket_table(bq))]
    a_args, a_specs, aliases = _alias_args(prev_outs, 2 + len(args))
    srows = nbat * H_C * bq
    z, wk, wv = pl.pallas_call(
        functools.partial(_attention_kernel, bq=bq, nbat=nbat, has_cache=has_cache,
                          mask_first=mask_first, layer=layer, has_prev=prev_outs is not None),
        out_shape=(jax.ShapeDtypeStruct((batch * t, W_C), BF16),
                   jax.ShapeDtypeStruct((DEPTH, batch, WINDOW, KVW), F32),
                   jax.ShapeDtypeStruct((DEPTH, batch, WINDOW, KVW), F32)),
        grid_spec=pltpu.PrefetchScalarGridSpec(
            num_scalar_prefetch=2,
            grid=(batch // wb, nsteps),
            in_specs=in_specs + a_specs,
            out_specs=(pl.BlockSpec((rows, W_C), lambda b, n, *_: (row(b, n), 0)),
                       _state_out_spec((wb, WINDOW, KVW), lambda b, n, *_: (b, 0, 0), layer,
                                       prev_outs is not None),
                       _state_out_spec((wb, WINDOW, KVW), lambda b, n, *_: (b, 0, 0), layer,
                                       prev_outs is not None)),
            scratch_shapes=[pltpu.VMEM((2 if mask_first else 1, H_C * bq, span), F32),
                            pltpu.VMEM((srows, KVW), BF16),
                            pltpu.VMEM((srows, span), F32),
                            pltpu.VMEM((srows, span), BF16),
                            pltpu.VMEM((srows, 2 * HD_C), F32),
                            pltpu.VMEM((srows, DENOM_W), F32)]),
        input_output_aliases=aliases,
        compiler_params=_params("arbitrary", "arbitrary"),
        name=name,
    )(sinks, rel_bias, *args, *a_args)
    return z, (wk, wv)


def _out_kernel(x_ref, za_ref, zb_ref, zc_ref, ma_ref, mb_ref, mc_ref, wa_ref, wb_ref, wc_ref,
                wo_ref, lng_ref, lnb_ref, *rest, alpha, emit_bf16):
    if emit_bf16:
        y_ref, yb_ref, merged_scr, r_scr = rest
    else:
        y_ref, merged_scr, r_scr = rest

    @pl.when(pl.program_id(0) == 0)
    def _():
        r_scr[...] = jnp.zeros_like(r_scr)

    lng, lnb = lng_ref[...], lnb_ref[...]
    for rc in range(r_scr.shape[0] // LN_ROWS):
        rs = slice(rc * LN_ROWS, (rc + 1) * LN_ROWS)
        r = r_scr[rs, :]
        mu = jnp.mean(r, axis=-1, keepdims=True)
        d = r - mu
        var = jnp.mean(d * d, axis=-1, keepdims=True)
        out = d * lax.rsqrt(var + LN_EPS) * lng + lnb
        y_ref[rs, :] = out
        if emit_bf16:
            yb_ref[rs, :] = out.astype(BF16)

    za, zb, zc = za_ref[...], zb_ref[...], zc_ref[...]
    for c in range(D_MODEL // OUT_STAGE_TN):
        cs = slice(c * OUT_STAGE_TN, (c + 1) * OUT_STAGE_TN)
        merged = (_sigmoid(ma_ref[:, cs].astype(F32))
                  * jnp.dot(za, wa_ref[:, cs], preferred_element_type=F32))
        merged += (_sigmoid(mb_ref[:, cs].astype(F32))
                   * jnp.dot(zb, wb_ref[:, cs], preferred_element_type=F32))
        merged += (_sigmoid(mc_ref[:, cs].astype(F32))
                   * jnp.dot(zc, wc_ref[:, cs], preferred_element_type=F32))
        merged_scr[:, cs] = merged.astype(BF16)
    r_scr[...] = alpha * x_ref[...] + jnp.dot(merged_scr[...], wo_ref[...],
                                              preferred_element_type=F32)


def _out_stage(x, za, zb, zc, h_gate, wa, wb, wc, wo, lng, lnb, alpha, *, emit_bf16, name):
    m = x.shape[0]
    tm = OUT_STAGE_TM
    nsteps = m // tm
    cur = lambda i: jnp.minimum(i, nsteps - 1)
    rowblk = lambda w: pl.BlockSpec((tm, w), lambda i: (cur(i), 0))
    gate = lambda j: pl.BlockSpec((tm, D_MODEL), lambda i: (cur(i), j))
    outblk = pl.BlockSpec((tm, D_MODEL), lambda i: (jnp.maximum(i - 1, 0), 0))
    const = lambda a: pl.BlockSpec(a.shape, lambda i: (0,) * a.ndim, pipeline_mode=pl.Buffered(1))
    out_shape = [jax.ShapeDtypeStruct((m, D_MODEL), F32)]
    if emit_bf16:
        out_shape.append(jax.ShapeDtypeStruct((m, D_MODEL), BF16))
    outs = pl.pallas_call(
        functools.partial(_out_kernel, alpha=alpha, emit_bf16=emit_bf16),
        out_shape=out_shape,
        grid=(nsteps + 1,),
        in_specs=[rowblk(D_MODEL), rowblk(W_A), rowblk(W_B), rowblk(W_C),
                  gate(0), gate(1), gate(2),
                  const(wa), const(wb), const(wc), const(wo), const(lng), const(lnb)],
        out_specs=[outblk] * len(out_shape),
        scratch_shapes=[pltpu.VMEM((tm, D_MODEL), BF16), pltpu.VMEM((tm, D_MODEL), F32)],
        compiler_params=_params("arbitrary"),
        name=name,
    )(x, za, zb, zc, h_gate, h_gate, h_gate, wa, wb, wc, wo, lng, lnb)
    return (outs[0], outs[1]) if emit_bf16 else (outs[0], None)


def _mixer_layer(x, xb, batch, t, pos0, states, layer, prev, w, w_in, alpha, tag):
    st_ret, st_k, st_v, st_pool = states if states is not None else (None,) * 4
    pv_ret, pv_win, pv_pool = prev if prev is not None else (None,) * 3
    if xb is None:
        h_att, xb = _in_proj(x, w["w_att"], N_ATT, emit_xb=True, name=f"in_proj_att_{tag}")
    else:
        h_att = _in_proj(xb, w["w_att"], N_ATT, name=f"in_proj_att_{tag}")
    h_ab = _in_proj(xb, w_in, N_AB, layer=layer, col0=0, name=f"in_proj_ab_{tag}")
    h_gate = _in_proj(xb, w_in, N_GATE, layer=layer, col0=REF_GATE, out_dtype=BF16,
                      name=f"in_proj_gate_{tag}")
    za, o_ret = _retention(h_ab, batch, t, pos0, st_ret, layer, pv_ret, name=f"retention_{tag}")
    zb, o_pool = _pool(h_ab, batch, t, pos0, st_pool, layer, pv_pool, w["w_pool_map"],
                       w["pool_scale"], name=f"pool_{tag}")
    zc, o_win = _attention(h_att, batch, t, pos0, st_k, st_v, layer, pv_win, w["sinks"],
                           w["rel_bias"], name=f"attention_{tag}")
    y, yb = _out_stage(x, za, zb, zc, h_gate, w["w_ret_o"], w["w_pool_o"], w["w_att_o"], w["w_out"],
                       w["ln_g"], w["ln_b"], alpha, emit_bf16=layer + 1 < DEPTH,
                       name=f"out_stage_{tag}")
    return y, yb, (o_ret, o_win, o_pool)


def _regroup_heads(w, axis):
    shp = w.shape
    w = w.reshape(shp[:axis] + (KV_C, G_C, HD_C) + shp[axis + 1:])
    w = jnp.swapaxes(w, axis, axis + 1)
    return w.reshape(shp)


def _prep_w_att(w_in, l):
    cols = lambda c0, n: lax.slice(w_in, (l, 0, c0), (l + 1, D_MODEL, c0 + n)).reshape(D_MODEL, n)
    return jnp.concatenate([
        _regroup_heads(cols(REF_QC, W_C), 1),
        _regroup_heads(cols(REF_GC, W_C), 1),
        cols(REF_KC, KVW),
        cols(REF_VC, KVW)], axis=1).astype(BF16)


def kernel(x_prompt, x_sample, state_ret, cache_win_k, cache_win_v, state_pool, w_in, w_ret_o,
           w_pool_map, pool_scale, w_pool_o, attn_sinks, w_att_o, w_out, ln_g, ln_b, rel_bias):
    alpha = (2.0 * DEPTH) ** 0.25
    bp, tp, _ = x_prompt.shape
    bs, ts, _ = x_sample.shape
    xp = x_prompt.reshape(bp * tp, D_MODEL)
    xs = x_sample.reshape(bs * ts, D_MODEL)
    xpb = xsb = None
    ck = cache_win_k.reshape(DEPTH, bs, WINDOW, KVW)
    cv = cache_win_v.reshape(DEPTH, bs, WINDOW, KVW)
    sample_states = (state_ret, ck, cv, state_pool)
    out_p = out_s = None
    for l in range(DEPTH):
        w = {
            "w_att": _prep_w_att(w_in, l),
            "w_ret_o": w_ret_o[l].astype(BF16),
            "w_pool_map": w_pool_map[l].astype(BF16),
            "pool_scale": pool_scale[l].reshape(1, W_B),
            "w_pool_o": w_pool_o[l].astype(BF16),
            "sinks": attn_sinks[l],
            "w_att_o": _regroup_heads(w_att_o[l], 0).astype(BF16),
            "w_out": w_out[l].astype(BF16),
            "ln_g": ln_g[l].reshape(1, D_MODEL),
            "ln_b": ln_b[l].reshape(1, D_MODEL),
            "rel_bias": rel_bias,
        }
        xp, xpb, out_p = _mixer_layer(xp, xpb, bp, tp, 0, None, l, out_p, w, w_in, alpha, f"p{l}")
        xs, xsb, out_s = _mixer_layer(xs, xsb, bs, ts, PAST_LEN, sample_states, l, out_s, w, w_in,
                                      alpha, f"s{l}")
    (ret_p,), (kp, vp), (pp,) = out_p
    (ret_s,), (ks, vs), (ps,) = out_s
    win = lambda a, b: a.reshape(DEPTH, b, WINDOW, KV_C, HD_C)
    return (xp.reshape(bp, tp, D_MODEL), xs.reshape(bs, ts, D_MODEL), ret_p, ret_s,
            win(kp, bp), win(ks, bs), win(vp, bp), win(vs, bs), pp, ps)
```

```python
import functools
import math

import jax
import jax.numpy as jnp
import numpy as np
from jax import lax
from jax.experimental import pallas as pl
from jax.experimental.pallas import tpu as pltpu

D_MODEL = 2048
DEPTH = 2
PAST_LEN = 8192
H_A, DK_A, DV_A = 4, 128, 256
W_A = H_A * DV_A
RET_CHUNK = 128
ROPE_BASE = 10000.0
N_POOL_GROUPS = 4
W_B = 1024
GW_B = W_B // N_POOL_GROUPS
POOL_WINDOWS = (2, 4, 8, 16)
POOL_BUF = 15
POOL_HALO = 16
H_C, KV_C, HD_C = 16, 4, 64
G_C = H_C // KV_C
W_C = H_C * HD_C
KVW = KV_C * HD_C
WINDOW = 128
ATT_BLOCK = 128
SOFTMAX_ROWS = 128
DENOM_W = 2 * HD_C
NUM_BUCKETS = 32
MAX_DISTANCE = 128
LN_EPS = 1e-5
RMS_EPS = 1e-6

REF_QC, REF_KC, REF_VC, REF_GC = 5120, 6144, 6400, 6656
N_AB = 5120
N_ATT = 2560
REF_GATE = 7680
N_GATE = 3 * D_MODEL
OFF_QA, OFF_KA, OFF_VA, OFF_GA, OFF_UB, OFF_GB = 0, 512, 1024, 2048, 3072, 4096
OFF_QC, OFF_GC, OFF_KC, OFF_VC = 0, 1024, 2048, 2304

F32 = jnp.float32
BF16 = jnp.bfloat16
VMEM_LIMIT = 56 * 1024 * 1024

IN_PROJ_TM = 1024
IN_PROJ_TM_F32_X = 512
IN_PROJ_TN = {N_AB: 1280, N_ATT: 2560, N_GATE: 1536}
OUT_STAGE_TM = 256
OUT_STAGE_TN = 512
LN_ROWS = 16
POOL_ROWS = 1024
SMALL_T_POOL_BATCH = 16
CAST_ROWS = 256
ATT_BLOCKS_PER_STEP = 2
SMALL_T_ATT_BATCH = 8
RET_ROWS = 512
SMALL_T_RET_BATCH = 4


def _sigmoid(x):
    return 1.0 / (1.0 + jnp.exp(-x))


def _silu(x):
    return x * _sigmoid(x)


def _params(*sem):
    return pltpu.CompilerParams(dimension_semantics=sem, vmem_limit_bytes=VMEM_LIMIT)


def _alias_args(prev_outs, n_in):
    if prev_outs is None:
        return [], [], {}
    specs = [pl.BlockSpec(memory_space=pl.ANY) for _ in prev_outs]
    return list(prev_outs), specs, {n_in + i: 1 + i for i in range(len(prev_outs))}


def _state_out_spec(block, index_map, layer, has_prev):
    if has_prev:
        return pl.BlockSpec((None,) + block, lambda *a: (layer,) + index_map(*a))
    return pl.BlockSpec((DEPTH,) + block, lambda *a: (0,) + index_map(*a))


def _state_store(ref, idx, value, layer, has_prev):
    if has_prev:
        ref[idx] = value
    else:
        for l in range(DEPTH):
            ref[(l,) + idx] = value if l == layer else jnp.zeros_like(value)


def _matmul_kernel(x_ref, w_ref, o_ref, *rest, cast_w, emit_xb):
    rest = list(rest)
    xb_ref = rest.pop(0) if emit_xb else None
    if cast_w:
        wb_scr, = rest

        @pl.when(pl.program_id(1) == 0)
        def _():
            wb_scr[...] = w_ref[...].astype(BF16)
        w = wb_scr[...]
    else:
        w = w_ref[...]
    x = x_ref[...].astype(BF16)
    if emit_xb:
        xb_ref[...] = x
    o_ref[...] = jnp.dot(x, w, preferred_element_type=F32).astype(o_ref.dtype)


def _in_proj(x, w, n, *, layer=None, col0=0, emit_xb=False, out_dtype=F32, name):
    m, k = x.shape
    tm, tn = (IN_PROJ_TM_F32_X if x.dtype == F32 else IN_PROJ_TM), IN_PROJ_TN[n]
    cast_w = w.dtype == F32
    if cast_w:
        w_spec = pl.BlockSpec((None, k, tn), lambda j, i: (layer, 0, col0 // tn + j))
    else:
        w_spec = pl.BlockSpec((k, tn), lambda j, i: (0, j))
    assert not emit_xb or n == tn
    out_shape = [jax.ShapeDtypeStruct((m, n), out_dtype)]
    out_specs = [pl.BlockSpec((tm, tn), lambda j, i: (i, j))]
    if emit_xb:
        out_shape.append(jax.ShapeDtypeStruct((m, k), BF16))
        out_specs.append(pl.BlockSpec((tm, k), lambda j, i: (i, 0)))
    outs = pl.pallas_call(
        functools.partial(_matmul_kernel, cast_w=cast_w, emit_xb=emit_xb),
        out_shape=out_shape,
        grid=(n // tn, m // tm),
        in_specs=[pl.BlockSpec((tm, k), lambda j, i: (i, 0)), w_spec],
        out_specs=out_specs,
        scratch_shapes=[pltpu.VMEM((k, tn), BF16)] if cast_w else [],
        compiler_params=_params("arbitrary", "arbitrary"),
        name=name,
    )(x, w)
    return outs if emit_xb else outs[0]


def _retention_kernel(*refs, chunk, nchunk, nbat, has_state, layer, has_prev):
    refs = list(refs)
    (q_ref, k_ref, v_ref, g_ref, cos_ref, sin_ref, dmask_ref, qdec_ref, kdec_ref, cdec_ref) = refs[:10]
    refs = refs[10:]
    s0_ref = refs.pop(0) if has_state else None
    if has_prev:
        refs.pop(0)
    z_ref, sout_ref, s_scr = refs
    c = pl.program_id(1)

    @pl.when(c == 0)
    def _():
        if has_state:
            s_scr[...] = s0_ref[...]
        else:
            s_scr[...] = jnp.zeros_like(s_scr)

    for bi in range(nbat):
        for hd in range(H_A):
            s = s_scr[bi, hd]
            for ci in range(nchunk):
                rows = slice((bi * nchunk + ci) * chunk, (bi * nchunk + ci + 1) * chunk)
                trow = slice(ci * chunk, (ci + 1) * chunk)
                cos = cos_ref[trow, :]
                sin = sin_ref[trow, :]
                q = q_ref[rows, hd * DK_A:(hd + 1) * DK_A]
                k = k_ref[rows, hd * DK_A:(hd + 1) * DK_A]
                v = v_ref[rows, hd * DV_A:(hd + 1) * DV_A].astype(BF16)
                qr = q * cos + pltpu.roll(q, DK_A // 2, 1) * sin
                kr = (k * cos + pltpu.roll(k, DK_A // 2, 1) * sin) * (DK_A ** -0.5)
                qb = qr.astype(BF16)
                sc = lax.dot_general(qb, kr.astype(BF16), (((1,), (1,)), ((), ())),
                                     preferred_element_type=F32) * dmask_ref[hd]
                o = (jnp.dot(sc.astype(BF16), v, preferred_element_type=F32)
                     + jnp.dot(qb, s.astype(BF16), preferred_element_type=F32) * qdec_ref[hd])
                kd = (kr * kdec_ref[hd]).astype(BF16)
                s = s * cdec_ref[hd] + lax.dot_general(
                    kd, v, (((0,), (0,)), ((), ())), preferred_element_type=F32)
                o = o * lax.rsqrt(jnp.mean(o * o, axis=-1, keepdims=True) + RMS_EPS)
                g = g_ref[rows, hd * DV_A:(hd + 1) * DV_A]
                z_ref[rows, hd * DV_A:(hd + 1) * DV_A] = (o * _silu(g)).astype(z_ref.dtype)
            s_scr[bi, hd] = s

    @pl.when(c == pl.num_programs(1) - 1)
    def _():
        _state_store(sout_ref, (Ellipsis,), s_scr[...], layer, has_prev)


def _retention_tables(pos0, t, chunk):
    half = DK_A // 2
    inv = ROPE_BASE ** (-jnp.arange(half, dtype=F32) / half)
    pos = pos0 + jnp.arange(t, dtype=F32)
    ang = pos[:, None] * inv[None, :]
    cos, sin = jnp.cos(ang), jnp.sin(ang)
    cos2 = jnp.concatenate([cos, cos], -1)
    sin2 = jnp.concatenate([-sin, sin], -1)
    lg = jnp.log1p(-jnp.exp2(-5.0 - jnp.arange(H_A, dtype=F32)))
    idx = jnp.arange(chunk, dtype=F32)
    diff = idx[:, None] - idx[None, :]
    dmask = jnp.where(diff >= 0, jnp.exp(lg[:, None, None] * jnp.maximum(diff, 0.0)), 0.0)
    qdec = jnp.exp(lg[:, None] * (idx[None, :] + 1.0))
    kdec = jnp.exp(lg[:, None] * (chunk - 1.0 - idx[None, :]))
    cdec = jnp.exp(lg * chunk)
    qdec = jnp.broadcast_to(qdec[:, :, None], (H_A, chunk, DV_A))
    kdec = jnp.broadcast_to(kdec[:, :, None], (H_A, chunk, DK_A))
    cdec = jnp.broadcast_to(cdec[:, None, None], (H_A, 1, DV_A))
    return cos2, sin2, dmask, qdec, kdec, cdec


def _retention(h_ab, batch, t, pos0, state, layer, prev_outs, *, name):
    chunk = min(RET_CHUNK, t)
    if t > chunk:
        nbat, nchunk = 1, RET_ROWS // chunk
    else:
        nbat, nchunk = SMALL_T_RET_BATCH, 1
    rows = nbat * nchunk * chunk
    nsteps = t // (nchunk * chunk)
    has_state = state is not None
    tables = _retention_tables(pos0, t, chunk)
    row = lambda b, c: b * nsteps + c
    in_specs = [
        pl.BlockSpec((rows, H_A * DK_A), lambda b, c: (row(b, c), OFF_QA // (H_A * DK_A))),
        pl.BlockSpec((rows, H_A * DK_A), lambda b, c: (row(b, c), OFF_KA // (H_A * DK_A))),
        pl.BlockSpec((rows, W_A), lambda b, c: (row(b, c), OFF_VA // W_A)),
        pl.BlockSpec((rows, W_A), lambda b, c: (row(b, c), OFF_GA // W_A)),
        pl.BlockSpec((nchunk * chunk, DK_A), lambda b, c: (c, 0)),
        pl.BlockSpec((nchunk * chunk, DK_A), lambda b, c: (c, 0)),
        pl.BlockSpec((H_A, chunk, chunk), lambda b, c: (0, 0, 0)),
        pl.BlockSpec((H_A, chunk, DV_A), lambda b, c: (0, 0, 0)),
        pl.BlockSpec((H_A, chunk, DK_A), lambda b, c: (0, 0, 0)),
        pl.BlockSpec((H_A, 1, DV_A), lambda b, c: (0, 0, 0)),
    ]
    args = [h_ab, h_ab, h_ab, h_ab, *tables]
    if has_state:
        in_specs.append(pl.BlockSpec((None, nbat, H_A, DK_A, DV_A), lambda b, c: (layer, b, 0, 0, 0)))
        args.append(state)
    a_args, a_specs, aliases = _alias_args(prev_outs, len(args))
    z, s_out = pl.pallas_call(
        functools.partial(_retention_kernel, chunk=chunk, nchunk=nchunk, nbat=nbat,
                          has_state=has_state, layer=layer, has_prev=prev_outs is not None),
        out_shape=(jax.ShapeDtypeStruct((batch * t, W_A), BF16),
                   jax.ShapeDtypeStruct((DEPTH, batch, H_A, DK_A, DV_A), F32)),
        grid=(batch // nbat, nsteps),
        in_specs=in_specs + a_specs,
        out_specs=(pl.BlockSpec((rows, W_A), lambda b, c: (row(b, c), 0)),
                   _state_out_spec((nbat, H_A, DK_A, DV_A), lambda b, c: (b, 0, 0, 0), layer,
                                   prev_outs is not None)),
        scratch_shapes=[pltpu.VMEM((nbat, H_A, DK_A, DV_A), F32)],
        input_output_aliases=aliases,
        compiler_params=_params("arbitrary", "arbitrary"),
        name=name,
    )(*args, *a_args)
    return z, (s_out,)


def _pool_kernel(*refs, nb, tb, pos0, has_state, layer, has_prev):
    refs = list(refs)
    u_ref, g_ref, halo_ref, wmap_ref, scale_ref = refs[:5]
    refs = refs[5:]
    if has_prev:
        refs.pop(0)
    z_ref, pout_ref, ext_scr, p_scr = refs
    ti = pl.program_id(1)
    for bi in range(nb):
        rows = pl.ds(bi * tb, tb)
        if has_state:
            ext_scr[0:1, :] = jnp.zeros((1, W_B), F32)
            ext_scr[1:POOL_HALO, :] = halo_ref[bi]
        else:
            ext_scr[0:POOL_HALO, :] = jnp.where(ti == 0, 0.0, halo_ref[...])
        ext_scr[POOL_HALO:POOL_HALO + tb, :] = u_ref[rows, :]
        t_idx = pos0 + ti * tb + lax.broadcasted_iota(jnp.int32, (tb, GW_B), 0)
        for gi, w in enumerate(POOL_WINDOWS):
            cols = slice(gi * GW_B, (gi + 1) * GW_B)
            acc = ext_scr[:, cols]
            shift = 1
            while shift < w:
                acc = acc + pltpu.roll(acc, shift, 0)
                shift *= 2
            cnt = jnp.minimum(t_idx + 1, w).astype(F32)
            u = ext_scr[POOL_HALO:POOL_HALO + tb, cols]
            p_scr[rows, cols] = acc[POOL_HALO:POOL_HALO + tb] / cnt - u

        @pl.when(ti == pl.num_programs(1) - 1)
        def _():
            _state_store(pout_ref, (bi,), ext_scr[tb + 1:tb + POOL_HALO, :], layer, has_prev)

    for gi in range(N_POOL_GROUPS):
        cols = slice(gi * GW_B, (gi + 1) * GW_B)
        pm = jnp.dot(p_scr[:, cols].astype(BF16), wmap_ref[gi], preferred_element_type=F32)
        pm = pm * scale_ref[:, cols]
        z_ref[:, cols] = (pm * _silu(g_ref[:, cols])).astype(z_ref.dtype)


def _pool(h_ab, batch, t, pos0, state, layer, prev_outs, wmap, scale, *, name):
    has_state = state is not None
    if t >= POOL_ROWS:
        nb, tb = 1, POOL_ROWS
    else:
        nb, tb = SMALL_T_POOL_BATCH, t
    nt = t // tb
    rows = nb * tb
    rblk = lambda b, i: b * nt + i
    in_specs = [pl.BlockSpec((rows, W_B), lambda b, i: (rblk(b, i), OFF_UB // W_B)),
                pl.BlockSpec((rows, W_B), lambda b, i: (rblk(b, i), OFF_GB // W_B))]
    args = [h_ab, h_ab]
    if has_state:
        assert nt == 1
        in_specs.append(pl.BlockSpec((None, nb, POOL_BUF, W_B), lambda b, i: (layer, b, 0, 0)))
        args.append(state)
    else:
        assert nb == 1
        per = tb // POOL_HALO
        in_specs.append(pl.BlockSpec(
            (POOL_HALO, W_B), lambda b, i: (jnp.maximum(rblk(b, i) * per - 1, 0), OFF_UB // W_B)))
        args.append(h_ab)
    in_specs += [pl.BlockSpec((N_POOL_GROUPS, GW_B, GW_B), lambda b, i: (0, 0, 0)),
                 pl.BlockSpec((1, W_B), lambda b, i: (0, 0))]
    args += [wmap, scale]
    a_args, a_specs, aliases = _alias_args(prev_outs, len(args))
    z, p_out = pl.pallas_call(
        functools.partial(_pool_kernel, nb=nb, tb=tb, pos0=pos0, has_state=has_state,
                          layer=layer, has_prev=prev_outs is not None),
        out_shape=(jax.ShapeDtypeStruct((batch * t, W_B), BF16),
                   jax.ShapeDtypeStruct((DEPTH, batch, POOL_BUF, W_B), F32)),
        grid=(batch // nb, nt),
        in_specs=in_specs + a_specs,
        out_specs=(pl.BlockSpec((rows, W_B), lambda b, i: (rblk(b, i), 0)),
                   _state_out_spec((nb, POOL_BUF, W_B), lambda b, i: (b, 0, 0), layer,
                                   prev_outs is not None)),
        scratch_shapes=[pltpu.VMEM((POOL_HALO + tb, W_B), F32),
                        pltpu.VMEM((rows, W_B), F32)],
        input_output_aliases=aliases,
        compiler_params=_params("arbitrary", "arbitrary"),
        name=name,
    )(*args, *a_args)
    return z, (p_out,)


def _t5_bucket_table(bq):
    span = WINDOW + bq
    dist = np.arange(bq)[:, None] + WINDOW - np.arange(span)[None, :]
    max_exact = NUM_BUCKETS // 2
    d = np.maximum(dist, 0).astype(np.float32)
    large = max_exact + (np.log(np.maximum(d, np.float32(1.0)) / np.float32(max_exact))
                         / np.float32(math.log(MAX_DISTANCE / max_exact))
                         * np.float32(NUM_BUCKETS - max_exact)).astype(np.int32)
    large = np.minimum(large, NUM_BUCKETS - 1)
    bucket = np.where(dist < max_exact, np.maximum(dist, 0), large)
    valid = (dist >= 0) & (dist < WINDOW)
    return np.where(valid, bucket, -1).astype(np.int32)


def _attention_kernel(*refs, bq, nbat, has_cache, mask_first, layer, has_prev):
    refs = list(refs)
    (sinks_ref, relb_ref, q_ref, kc_ref, vc_ref, kp_ref, vp_ref, g_ref, bucket_ref) = refs[:9]
    refs = refs[9:]
    if has_prev:
        del refs[:2]
    z_ref, wk_ref, wv_ref, bias_scr, qbd_scr, s_scr, p_scr, o_scr, l_scr = refs
    span = WINDOW + bq
    hrows = H_C * bq
    grows = G_C * bq
    chunk = min(hrows, SOFTMAX_ROWS)
    one_matmul = hrows <= SOFTMAX_ROWS
    fmin = jnp.finfo(F32).min
    n = pl.program_id(1)

    @pl.when((pl.program_id(0) == 0) & (n == 0))
    def _():
        bucket = bucket_ref[...]
        col = lax.broadcasted_iota(jnp.int32, (bq, span), 1)
        for h in range(H_C):
            acc = jnp.where(bucket < 0, fmin, 0.0)
            for b in range(NUM_BUCKETS):
                acc = jnp.where(bucket == b, relb_ref[b, h], acc)
            acc = jnp.where(col == 0, sinks_ref[h], acc)
            bias_scr[0, h * bq:(h + 1) * bq, :] = acc
            if mask_first:
                bias_scr[1, h * bq:(h + 1) * bq, :] = jnp.where((col < WINDOW) & (col > 0), fmin, acc)

    lane_kv = lax.broadcasted_iota(jnp.int32, (1, KVW), 1) // HD_C
    low_half = lax.broadcasted_iota(jnp.int32, (1, 2 * HD_C), 1) < HD_C
    row0 = lax.broadcasted_iota(jnp.int32, (WINDOW, KVW), 0) == 0
    ones = jnp.ones((span, DENOM_W), BF16)
    kc = vc = None
    for bi in range(nbat):
        rows = slice(bi * bq, (bi + 1) * bq)
        base = bi * hrows
        if has_cache:
            kp, vp = kp_ref[bi], vp_ref[bi]
        elif bi == 0:
            kp, vp = kp_ref[...], vp_ref[...]
        else:
            kp, vp = kc, vc
        bsel = jnp.where(n == 0, 1, 0) if (mask_first and bi == 0) else 0
        kc = kc_ref[rows, :]
        vc = vc_ref[rows, :]
        kx = jnp.concatenate([jnp.where(row0, 0.0, kp), kc], axis=0).astype(BF16)
        vx = jnp.concatenate([jnp.where(row0, 0.0, vp), vc], axis=0).astype(BF16)
        q = q_ref[rows, :] * (HD_C ** -0.5)
        for kk in range(KV_C):
            qbd_scr[base + kk * grows:base + (kk + 1) * grows, :] = jnp.concatenate(
                [jnp.where(lane_kv == kk, q[:, g * KVW:(g + 1) * KVW], 0.0) for g in range(G_C)],
                axis=0).astype(BF16)
        s_scr[base:base + hrows, :] = lax.dot_general(
            qbd_scr[base:base + hrows, :], kx, (((1,), (1,)), ((), ())), preferred_element_type=F32)
        for c in range(hrows // chunk):
            rs = slice(base + c * chunk, base + (c + 1) * chunk)
            s = s_scr[rs, :] + bias_scr[bsel, c * chunk:(c + 1) * chunk, :]
            m = jnp.max(s, axis=-1, keepdims=True)
            p_scr[rs, :] = jnp.exp(s - m).astype(BF16)
        if one_matmul:
            o3 = jnp.dot(p_scr[base:base + hrows, :], jnp.concatenate([vx, ones], axis=1),
                         preferred_element_type=F32)
            for kk in range(KV_C):
                o_scr[base + kk * grows:base + (kk + 1) * grows, :] = (
                    o3[kk * grows:(kk + 1) * grows, (kk // 2) * 2 * HD_C:(kk // 2 + 1) * 2 * HD_C])
            l_scr[base:base + hrows, :] = o3[:, KVW:KVW + DENOM_W]
        else:
            for kk in range(KV_C):
                rk = slice(base + kk * grows, base + (kk + 1) * grows)
                slab = vx[:, (kk // 2) * 2 * HD_C:(kk // 2 + 1) * 2 * HD_C]
                vk = jnp.where(low_half if kk % 2 == 0 else ~low_half, slab, jnp.zeros_like(slab))
                ol = jnp.dot(p_scr[rk, :], jnp.concatenate([vk, ones], axis=1),
                             preferred_element_type=F32)
                o_scr[rk, :] = ol[:, :2 * HD_C]
                l_scr[rk, :] = ol[:, 2 * HD_C:]
        for g in range(G_C):
            halves = []
            for half in range(2):
                ra, rb = (slice(base + kk * grows + g * bq, base + kk * grows + (g + 1) * bq)
                          for kk in (2 * half, 2 * half + 1))
                num = jnp.where(low_half, o_scr[ra, :], o_scr[rb, :])
                den = jnp.where(low_half, l_scr[ra, :], l_scr[rb, :])
                halves.append(num * (1.0 / den))
            og = jnp.concatenate(halves, axis=1)
            cols = slice(g * KVW, (g + 1) * KVW)
            z_ref[rows, cols] = (og * _silu(g_ref[rows, cols])).astype(z_ref.dtype)

        if has_cache or bi == nbat - 1:
            @pl.when(n == pl.num_programs(1) - 1)
            def _():
                wk = kc if bq == WINDOW else jnp.concatenate([kp[bq:, :], kc], axis=0)
                wv = vc if bq == WINDOW else jnp.concatenate([vp[bq:, :], vc], axis=0)
                _state_store(wk_ref, (bi if has_cache else 0,), wk, layer, has_prev)
                _state_store(wv_ref, (bi if has_cache else 0,), wv, layer, has_prev)


def _attention(h_att, batch, t, pos0, cache_k, cache_v, layer, prev_outs, sinks, rel_bias, *, name):
    bq = min(ATT_BLOCK, t)
    nb = t // bq
    has_cache = cache_k is not None
    nbat = SMALL_T_ATT_BATCH if has_cache else ATT_BLOCKS_PER_STEP
    nsteps = 1 if has_cache else nb // nbat
    wb = nbat if has_cache else 1
    span = WINDOW + bq
    assert pos0 == 0 or pos0 >= WINDOW
    mask_first = pos0 == 0
    rows = nbat * bq
    row = lambda b, n, *_: b * nsteps + n
    in_specs = [
        pl.BlockSpec((rows, W_C), lambda b, n, *_: (row(b, n), OFF_QC // W_C)),
        pl.BlockSpec((rows, KVW), lambda b, n, *_: (row(b, n), OFF_KC // KVW)),
        pl.BlockSpec((rows, KVW), lambda b, n, *_: (row(b, n), OFF_VC // KVW)),
    ]
    args = [h_att, h_att, h_att]
    if has_cache:
        assert nb == 1
        in_specs += [pl.BlockSpec((None, nbat, WINDOW, KVW), lambda b, n, *_: (layer, b, 0, 0)),
                     pl.BlockSpec((None, nbat, WINDOW, KVW), lambda b, n, *_: (layer, b, 0, 0))]
        args += [cache_k, cache_v]
    else:
        assert bq == WINDOW and nb % nbat == 0
        prev = lambda b, n, *_: jnp.maximum(row(b, n) * nbat - 1, 0)
        in_specs += [pl.BlockSpec((WINDOW, KVW), lambda b, n, *_: (prev(b, n), OFF_KC // KVW)),
                     pl.BlockSpec((WINDOW, KVW), lambda b, n, *_: (prev(b, n), OFF_VC // KVW))]
        args += [h_att, h_att]
    in_specs += [
        pl.BlockSpec((rows, W_C), lambda b, n, *_: (row(b, n), OFF_GC // W_C)),
        pl.BlockSpec((bq, span), lambda b, n, *_: (0, 0)),
    ]
    args += [h_att, jnp.asarray(_t5_bucket_table(bq))]
    a_args, a_specs, aliases = _alias_args(prev_outs, 2 + len(args))
    srows = nbat * H_C * bq
    z, wk, wv = pl.pallas_call(
        functools.partial(_attention_kernel, bq=bq, nbat=nbat, has_cache=has_cache,
                          mask_first=mask_first, layer=layer, has_prev=prev_outs is not None),
        out_shape=(jax.ShapeDtypeStruct((batch * t, W_C), BF16),
                   jax.ShapeDtypeStruct((DEPTH, batch, WINDOW, KVW), F32),
                   jax.ShapeDtypeStruct((DEPTH, batch, WINDOW, KVW), F32)),
        grid_spec=pltpu.PrefetchScalarGridSpec(
            num_scalar_prefetch=2,
            grid=(batch // wb, nsteps),
            in_specs=in_specs + a_specs,
            out_specs=(pl.BlockSpec((rows, W_C), lambda b, n, *_: (row(b, n), 0)),
                       _state_out_spec((wb, WINDOW, KVW), lambda b, n, *_: (b, 0, 0), layer,
                                       prev_outs is not None),
                       _state_out_spec((wb, WINDOW, KVW), lambda b, n, *_: (b, 0, 0), layer,
                                       prev_outs is not None)),
            scratch_shapes=[pltpu.VMEM((2 if mask_first else 1, H_C * bq, span), F32),
                            pltpu.VMEM((srows, KVW), BF16),
                            pltpu.VMEM((srows, span), F32),
                            pltpu.VMEM((srows, span), BF16),
                            pltpu.VMEM((srows, 2 * HD_C), F32),
                            pltpu.VMEM((srows, DENOM_W), F32)]),
        input_output_aliases=aliases,
        compiler_params=_params("arbitrary", "arbitrary"),
        name=name,
    )(sinks, rel_bias, *args, *a_args)
    return z, (wk, wv)


def _out_kernel(x_ref, za_ref, zb_ref, zc_ref, ma_ref, mb_ref, mc_ref, wa_ref, wb_ref, wc_ref,
                wo_ref, lng_ref, lnb_ref, *rest, alpha, emit_bf16):
    if emit_bf16:
        y_ref, yb_ref, merged_scr, r_scr = rest
    else:
        y_ref, merged_scr, r_scr = rest

    @pl.when(pl.program_id(0) == 0)
    def _():
        r_scr[...] = jnp.zeros_like(r_scr)

    lng, lnb = lng_ref[...], lnb_ref[...]
    for rc in range(r_scr.shape[0] // LN_ROWS):
        rs = slice(rc * LN_ROWS, (rc + 1) * LN_ROWS)
        r = r_scr[rs, :]
        mu = jnp.mean(r, axis=-1, keepdims=True)
        d = r - mu
        var = jnp.mean(d * d, axis=-1, keepdims=True)
        out = d * lax.rsqrt(var + LN_EPS) * lng + lnb
        y_ref[rs, :] = out
        if emit_bf16:
            yb_ref[rs, :] = out.astype(BF16)

    za, zb, zc = za_ref[...], zb_ref[...], zc_ref[...]
    for c in range(D_MODEL // OUT_STAGE_TN):
        cs = slice(c * OUT_STAGE_TN, (c + 1) * OUT_STAGE_TN)
        merged = (_sigmoid(ma_ref[:, cs].astype(F32))
                  * jnp.dot(za, wa_ref[:, cs], preferred_element_type=F32))
        merged += (_sigmoid(mb_ref[:, cs].astype(F32))
                   * jnp.dot(zb, wb_ref[:, cs], preferred_element_type=F32))
        merged += (_sigmoid(mc_ref[:, cs].astype(F32))
                   * jnp.dot(zc, wc_ref[:, cs], preferred_element_type=F32))
        merged_scr[:, cs] = merged.astype(BF16)
    r_scr[...] = alpha * x_ref[...] + jnp.dot(merged_scr[...], wo_ref[...],
                                              preferred_element_type=F32)


def _out_stage(x, za, zb, zc, h_gate, wa, wb, wc, wo, lng, lnb, alpha, *, emit_bf16, name):
    m = x.shape[0]
    tm = OUT_STAGE_TM
    nsteps = m // tm
    cur = lambda i: jnp.minimum(i, nsteps - 1)
    rowblk = lambda w: pl.BlockSpec((tm, w), lambda i: (cur(i), 0))
    gate = lambda j: pl.BlockSpec((tm, D_MODEL), lambda i: (cur(i), j))
    outblk = pl.BlockSpec((tm, D_MODEL), lambda i: (jnp.maximum(i - 1, 0), 0))
    const = lambda a: pl.BlockSpec(a.shape, lambda i: (0,) * a.ndim, pipeline_mode=pl.Buffered(1))
    out_shape = [jax.ShapeDtypeStruct((m, D_MODEL), F32)]
    if emit_bf16:
        out_shape.append(jax.ShapeDtypeStruct((m, D_MODEL), BF16))
    outs = pl.pallas_call(
        functools.partial(_out_kernel, alpha=alpha, emit_bf16=emit_bf16),
        out_shape=out_shape,
        grid=(nsteps + 1,),
        in_specs=[rowblk(D_MODEL), rowblk(W_A), rowblk(W_B), rowblk(W_C),
                  gate(0), gate(1), gate(2),
                  const(wa), const(wb), const(wc), const(wo), const(lng), const(lnb)],
        out_specs=[outblk] * len(out_shape),
        scratch_shapes=[pltpu.VMEM((tm, D_MODEL), BF16), pltpu.VMEM((tm, D_MODEL), F32)],
        compiler_params=_params("arbitrary"),
        name=name,
    )(x, za, zb, zc, h_gate, h_gate, h_gate, wa, wb, wc, wo, lng, lnb)
    return (outs[0], outs[1]) if emit_bf16 else (outs[0], None)


def _mixer_layer(x, xb, batch, t, pos0, states, layer, prev, w, w_in, alpha, tag):
    st_ret, st_k, st_v, st_pool = states if states is not None else (None,) * 4
    pv_ret, pv_win, pv_pool = prev if prev is not None else (None,) * 3
    if xb is None:
        h_att, xb = _in_proj(x, w["w_att"], N_ATT, emit_xb=True, name=f"in_proj_att_{tag}")
    else:
        h_att = _in_proj(xb, w["w_att"], N_ATT, name=f"in_proj_att_{tag}")
    h_ab = _in_proj(xb, w_in, N_AB, layer=layer, col0=0, name=f"in_proj_ab_{tag}")
    h_gate = _in_proj(xb, w_in, N_GATE, layer=layer, col0=REF_GATE, out_dtype=BF16,
                      name=f"in_proj_gate_{tag}")
    za, o_ret = _retention(h_ab, batch, t, pos0, st_ret, layer, pv_ret, name=f"retention_{tag}")
    zb, o_pool = _pool(h_ab, batch, t, pos0, st_pool, layer, pv_pool, w["w_pool_map"],
                       w["pool_scale"], name=f"pool_{tag}")
    zc, o_win = _attention(h_att, batch, t, pos0, st_k, st_v, layer, pv_win, w["sinks"],
                           w["rel_bias"], name=f"attention_{tag}")
    y, yb = _out_stage(x, za, zb, zc, h_gate, w["w_ret_o"], w["w_pool_o"], w["w_att_o"], w["w_out"],
                       w["ln_g"], w["ln_b"], alpha, emit_bf16=layer + 1 < DEPTH,
                       name=f"out_stage_{tag}")
    return y, yb, (o_ret, o_win, o_pool)


def _regroup_heads(w, axis):
    shp = w.shape
    w = w.reshape(shp[:axis] + (KV_C, G_C, HD_C) + shp[axis + 1:])
    w = jnp.swapaxes(w, axis, axis + 1)
    return w.reshape(shp)


def _prep_w_att(w_in, l):
    cols = lambda c0, n: lax.slice(w_in, (l, 0, c0), (l + 1, D_MODEL, c0 + n)).reshape(D_MODEL, n)
    w = jnp.concatenate([
        _regroup_heads(cols(REF_QC, W_C), 1),
        _regroup_heads(cols(REF_GC, W_C), 1),
        cols(REF_KC, KVW),
        cols(REF_VC, KVW)], axis=1)
    return _to_bf16(w, name=f"w_att_bf16_{l}")


def _cast_kernel(x_ref, o_ref):
    o_ref[...] = x_ref[...].astype(o_ref.dtype)


def _to_bf16(w, *, name):
    r, c = w.shape
    return pl.pallas_call(
        _cast_kernel,
        out_shape=jax.ShapeDtypeStruct((r, c), BF16),
        grid=(r // CAST_ROWS,),
        in_specs=[pl.BlockSpec((CAST_ROWS, c), lambda i: (i, 0))],
        out_specs=pl.BlockSpec((CAST_ROWS, c), lambda i: (i, 0)),
        compiler_params=_params("parallel"),
        name=name,
    )(w)


def kernel(x_prompt, x_sample, state_ret, cache_win_k, cache_win_v, state_pool, w_in, w_ret_o,
           w_pool_map, pool_scale, w_pool_o, attn_sinks, w_att_o, w_out, ln_g, ln_b, rel_bias):
    alpha = (2.0 * DEPTH) ** 0.25
    bp, tp, _ = x_prompt.shape
    bs, ts, _ = x_sample.shape
    xp = x_prompt.reshape(bp * tp, D_MODEL)
    xs = x_sample.reshape(bs * ts, D_MODEL)
    xpb = xsb = None
    ck = cache_win_k.reshape(DEPTH, bs, WINDOW, KVW)
    cv = cache_win_v.reshape(DEPTH, bs, WINDOW, KVW)
    sample_states = (state_ret, ck, cv, state_pool)
    out_p = out_s = None
    for l in range(DEPTH):
        w = {
            "w_att": _prep_w_att(w_in, l),
            "w_ret_o": w_ret_o[l].astype(BF16),
            "w_pool_map": w_pool_map[l].astype(BF16),
            "pool_scale": pool_scale[l].reshape(1, W_B),
            "w_pool_o": w_pool_o[l].astype(BF16),
            "sinks": attn_sinks[l],
            "w_att_o": _regroup_heads(w_att_o[l], 0).astype(BF16),
            "w_out": w_out[l].astype(BF16),
            "ln_g": ln_g[l].reshape(1, D_MODEL),
            "ln_b": ln_b[l].reshape(1, D_MODEL),
            "rel_bias": rel_bias,
        }
        xp, xpb, out_p = _mixer_layer(xp, xpb, bp, tp, 0, None, l, out_p, w, w_in, alpha, f"p{l}")
        xs, xsb, out_s = _mixer_layer(xs, xsb, bs, ts, PAST_LEN, sample_states, l, out_s, w, w_in,
                                      alpha, f"s{l}")
    (ret_p,), (kp, vp), (pp,) = out_p
    (ret_s,), (ks, vs), (ps,) = out_s
    win = lambda a, b: a.reshape(DEPTH, b, WINDOW, KV_C, HD_C)
    return (xp.reshape(bp, tp, D_MODEL), xs.reshape(bs, ts, D_MODEL), ret_p, ret_s,
            win(kp, bp), win(ks, bs), win(vp, bp), win(vs, bs), pp, ps)
```

```python
import functools
import math

import jax
import jax.numpy as jnp
import numpy as np
from jax import lax
from jax.experimental import pallas as pl
from jax.experimental.pallas import tpu as pltpu

D_MODEL = 2048
DEPTH = 2
PAST_LEN = 8192
H_A, DK_A, DV_A = 4, 128, 256
W_A = H_A * DV_A
RET_CHUNK = 128
ROPE_BASE = 10000.0
N_POOL_GROUPS = 4
W_B = 1024
GW_B = W_B // N_POOL_GROUPS
POOL_WINDOWS = (2, 4, 8, 16)
POOL_BUF = 15
POOL_HALO = 16
H_C, KV_C, HD_C = 16, 4, 64
G_C = H_C // KV_C
W_C = H_C * HD_C
KVW = KV_C * HD_C
WINDOW = 128
ATT_BLOCK = 128
SOFTMAX_ROWS = 128
DENOM_W = 2 * HD_C
NUM_BUCKETS = 32
MAX_DISTANCE = 128
LN_EPS = 1e-5
RMS_EPS = 1e-6

REF_QC, REF_KC, REF_VC, REF_GC = 5120, 6144, 6400, 6656
N_AB = 5120
N_ATT = 2560
REF_GATE = 7680
N_GATE = 3 * D_MODEL
OFF_QA, OFF_KA, OFF_VA, OFF_GA, OFF_UB, OFF_GB = 0, 512, 1024, 2048, 3072, 4096
OFF_QC, OFF_GC, OFF_KC, OFF_VC = 0, 1024, 2048, 2304

F32 = jnp.float32
BF16 = jnp.bfloat16
VMEM_LIMIT = 56 * 1024 * 1024

IN_PROJ_TM = 1024
IN_PROJ_TM_F32_X = 512
IN_PROJ_TN = {N_AB: 1280, N_ATT: 2560, N_GATE: 1536}
OUT_STAGE_TM = 256
OUT_STAGE_TN = 512
LN_ROWS = 16
POOL_ROWS = 1024
SMALL_T_POOL_BATCH = 16
CAST_ROWS = 256
ATT_BLOCKS_PER_STEP = 2
SMALL_T_ATT_BATCH = 8
RET_ROWS = 512
SMALL_T_RET_BATCH = 4


def _sigmoid(x):
    return 1.0 / (1.0 + jnp.exp(-x))


def _silu(x):
    return x * _sigmoid(x)


def _params(*sem):
    return pltpu.CompilerParams(dimension_semantics=sem, vmem_limit_bytes=VMEM_LIMIT)


def _alias_args(prev_outs, n_in):
    if prev_outs is None:
        return [], [], {}
    specs = [pl.BlockSpec(memory_space=pl.ANY) for _ in prev_outs]
    return list(prev_outs), specs, {n_in + i: 1 + i for i in range(len(prev_outs))}


def _block_offset(row0, rows):
    assert row0 % rows == 0
    return row0 // rows


def _state_out_spec(block, index_map, layer, has_prev):
    if has_prev:
        return pl.BlockSpec((None,) + block, lambda *a: (layer,) + index_map(*a))
    return pl.BlockSpec((DEPTH,) + block, lambda *a: (0,) + index_map(*a))


def _state_store(ref, idx, value, layer, has_prev):
    if has_prev:
        ref[idx] = value
    else:
        for l in range(DEPTH):
            ref[(l,) + idx] = value if l == layer else jnp.zeros_like(value)


def _matmul_kernel(*refs, n_x, first_blocks, cast_w, emit_xb):
    refs = list(refs)
    x_refs, (w_ref, o_ref), rest = refs[:n_x], refs[n_x:n_x + 2], refs[n_x + 2:]
    xb_ref = rest.pop(0) if emit_xb else None
    if cast_w:
        wb_scr, = rest

        @pl.when(pl.program_id(1) == 0)
        def _():
            wb_scr[...] = w_ref[...].astype(BF16)
        w_ref = wb_scr

    def project(x_ref):
        x = x_ref[...].astype(BF16)
        if emit_xb:
            xb_ref[...] = x
        o_ref[...] = jnp.dot(x, w_ref[...], preferred_element_type=F32).astype(o_ref.dtype)

    if n_x == 1:
        project(x_refs[0])
    else:
        i = pl.program_id(1)
        pl.when(i < first_blocks)(lambda: project(x_refs[0]))
        pl.when(i >= first_blocks)(lambda: project(x_refs[1]))


def _in_proj(xs, w, n, *, layer=None, col0=0, emit_xb=False, out_dtype=F32, name):
    assert len(xs) in (1, 2)
    k = xs[0].shape[1]
    m = sum(x.shape[0] for x in xs)
    tm, tn = (IN_PROJ_TM_F32_X if xs[0].dtype == F32 else IN_PROJ_TM), IN_PROJ_TN[n]
    first = xs[0].shape[0] // tm
    assert all(x.shape[0] % tm == 0 for x in xs)
    cast_w = w.dtype == F32
    if cast_w:
        w_spec = pl.BlockSpec((None, k, tn), lambda j, i: (layer, 0, col0 // tn + j))
    else:
        w_spec = pl.BlockSpec((k, tn), lambda j, i: (0, j),
                              pipeline_mode=pl.Buffered(1) if n == tn else None)
    if len(xs) == 1:
        x_specs = [pl.BlockSpec((tm, k), lambda j, i: (i, 0))]
    else:
        last = xs[1].shape[0] // tm - 1
        x_specs = [pl.BlockSpec((tm, k), lambda j, i: (jnp.minimum(i, first - 1), 0)),
                   pl.BlockSpec((tm, k), lambda j, i: (jnp.clip(i - first, 0, last), 0),
                                pipeline_mode=None if last else pl.Buffered(1))]
    assert not emit_xb or n == tn
    out_shape = [jax.ShapeDtypeStruct((m, n), out_dtype)]
    out_specs = [pl.BlockSpec((tm, tn), lambda j, i: (i, j))]
    if emit_xb:
        out_shape.append(jax.ShapeDtypeStruct((m, k), BF16))
        out_specs.append(pl.BlockSpec((tm, k), lambda j, i: (i, 0)))
    outs = pl.pallas_call(
        functools.partial(_matmul_kernel, n_x=len(xs), first_blocks=first, cast_w=cast_w,
                          emit_xb=emit_xb),
        out_shape=out_shape,
        grid=(n // tn, m // tm),
        in_specs=x_specs + [w_spec],
        out_specs=out_specs,
        scratch_shapes=[pltpu.VMEM((k, tn), BF16)] if cast_w else [],
        compiler_params=_params("arbitrary", "arbitrary"),
        name=name,
    )(*xs, w)
    return outs if emit_xb else outs[0]


def _retention_kernel(*refs, chunk, nchunk, nbat, has_state, layer, has_prev):
    refs = list(refs)
    (q_ref, k_ref, v_ref, g_ref, cos_ref, sin_ref, dmask_ref, qdec_ref, kdec_ref, cdec_ref) = refs[:10]
    refs = refs[10:]
    s0_ref = refs.pop(0) if has_state else None
    if has_prev:
        refs.pop(0)
    z_ref, sout_ref, s_scr = refs
    c = pl.program_id(1)

    @pl.when(c == 0)
    def _():
        if has_state:
            s_scr[...] = s0_ref[...]
        else:
            s_scr[...] = jnp.zeros_like(s_scr)

    for bi in range(nbat):
        for hd in range(H_A):
            s = s_scr[bi, hd]
            for ci in range(nchunk):
                rows = slice((bi * nchunk + ci) * chunk, (bi * nchunk + ci + 1) * chunk)
                trow = slice(ci * chunk, (ci + 1) * chunk)
                cos = cos_ref[trow, :]
                sin = sin_ref[trow, :]
                q = q_ref[rows, hd * DK_A:(hd + 1) * DK_A]
                k = k_ref[rows, hd * DK_A:(hd + 1) * DK_A]
                v = v_ref[rows, hd * DV_A:(hd + 1) * DV_A].astype(BF16)
                qr = q * cos + pltpu.roll(q, DK_A // 2, 1) * sin
                kr = (k * cos + pltpu.roll(k, DK_A // 2, 1) * sin) * (DK_A ** -0.5)
                qb = qr.astype(BF16)
                sc = lax.dot_general(qb, kr.astype(BF16), (((1,), (1,)), ((), ())),
                                     preferred_element_type=F32) * dmask_ref[hd]
                o = (jnp.dot(sc.astype(BF16), v, preferred_element_type=F32)
                     + jnp.dot(qb, s.astype(BF16), preferred_element_type=F32) * qdec_ref[hd])
                kd = (kr * kdec_ref[hd]).astype(BF16)
                s = s * cdec_ref[hd] + lax.dot_general(
                    kd, v, (((0,), (0,)), ((), ())), preferred_element_type=F32)
                o = o * lax.rsqrt(jnp.mean(o * o, axis=-1, keepdims=True) + RMS_EPS)
                g = g_ref[rows, hd * DV_A:(hd + 1) * DV_A]
                z_ref[rows, hd * DV_A:(hd + 1) * DV_A] = (o * _silu(g)).astype(z_ref.dtype)
            s_scr[bi, hd] = s

    @pl.when(c == pl.num_programs(1) - 1)
    def _():
        _state_store(sout_ref, (Ellipsis,), s_scr[...], layer, has_prev)


def _retention_tables(pos0, t, chunk):
    half = DK_A // 2
    inv = ROPE_BASE ** (-jnp.arange(half, dtype=F32) / half)
    pos = pos0 + jnp.arange(t, dtype=F32)
    ang = pos[:, None] * inv[None, :]
    cos, sin = jnp.cos(ang), jnp.sin(ang)
    cos2 = jnp.concatenate([cos, cos], -1)
    sin2 = jnp.concatenate([-sin, sin], -1)
    lg = jnp.log1p(-jnp.exp2(-5.0 - jnp.arange(H_A, dtype=F32)))
    idx = jnp.arange(chunk, dtype=F32)
    diff = idx[:, None] - idx[None, :]
    dmask = jnp.where(diff >= 0, jnp.exp(lg[:, None, None] * jnp.maximum(diff, 0.0)), 0.0)
    qdec = jnp.exp(lg[:, None] * (idx[None, :] + 1.0))
    kdec = jnp.exp(lg[:, None] * (chunk - 1.0 - idx[None, :]))
    cdec = jnp.exp(lg * chunk)
    qdec = jnp.broadcast_to(qdec[:, :, None], (H_A, chunk, DV_A))
    kdec = jnp.broadcast_to(kdec[:, :, None], (H_A, chunk, DK_A))
    cdec = jnp.broadcast_to(cdec[:, None, None], (H_A, 1, DV_A))
    return cos2, sin2, dmask, qdec, kdec, cdec


def _retention(h_ab, row0, batch, t, pos0, state, layer, prev_outs, *, name):
    chunk = min(RET_CHUNK, t)
    if t > chunk:
        nbat, nchunk = 1, RET_ROWS // chunk
    else:
        nbat, nchunk = SMALL_T_RET_BATCH, 1
    rows = nbat * nchunk * chunk
    nsteps = t // (nchunk * chunk)
    has_state = state is not None
    tables = _retention_tables(pos0, t, chunk)
    row = lambda b, c: b * nsteps + c
    hrow = lambda b, c: row(b, c) + _block_offset(row0, rows)
    in_specs = [
        pl.BlockSpec((rows, H_A * DK_A), lambda b, c: (hrow(b, c), OFF_QA // (H_A * DK_A))),
        pl.BlockSpec((rows, H_A * DK_A), lambda b, c: (hrow(b, c), OFF_KA // (H_A * DK_A))),
        pl.BlockSpec((rows, W_A), lambda b, c: (hrow(b, c), OFF_VA // W_A)),
        pl.BlockSpec((rows, W_A), lambda b, c: (hrow(b, c), OFF_GA // W_A)),
        pl.BlockSpec((nchunk * chunk, DK_A), lambda b, c: (c, 0)),
        pl.BlockSpec((nchunk * chunk, DK_A), lambda b, c: (c, 0)),
        pl.BlockSpec((H_A, chunk, chunk), lambda b, c: (0, 0, 0)),
        pl.BlockSpec((H_A, chunk, DV_A), lambda b, c: (0, 0, 0)),
        pl.BlockSpec((H_A, chunk, DK_A), lambda b, c: (0, 0, 0)),
        pl.BlockSpec((H_A, 1, DV_A), lambda b, c: (0, 0, 0)),
    ]
    args = [h_ab, h_ab, h_ab, h_ab, *tables]
    if has_state:
        in_specs.append(pl.BlockSpec((None, nbat, H_A, DK_A, DV_A), lambda b, c: (layer, b, 0, 0, 0)))
        args.append(state)
    a_args, a_specs, aliases = _alias_args(prev_outs, len(args))
    z, s_out = pl.pallas_call(
        functools.partial(_retention_kernel, chunk=chunk, nchunk=nchunk, nbat=nbat,
                          has_state=has_state, layer=layer, has_prev=prev_outs is not None),
        out_shape=(jax.ShapeDtypeStruct((batch * t, W_A), BF16),
                   jax.ShapeDtypeStruct((DEPTH, batch, H_A, DK_A, DV_A), F32)),
        grid=(batch // nbat, nsteps),
        in_specs=in_specs + a_specs,
        out_specs=(pl.BlockSpec((rows, W_A), lambda b, c: (row(b, c), 0)),
                   _state_out_spec((nbat, H_A, DK_A, DV_A), lambda b, c: (b, 0, 0, 0), layer,
                                   prev_outs is not None)),
        scratch_shapes=[pltpu.VMEM((nbat, H_A, DK_A, DV_A), F32)],
        input_output_aliases=aliases,
        compiler_params=_params("arbitrary", "arbitrary"),
        name=name,
    )(*args, *a_args)
    return z, (s_out,)


def _pool_kernel(*refs, nb, tb, pos0, has_state, layer, has_prev):
    refs = list(refs)
    u_ref, g_ref, halo_ref, wmap_ref, scale_ref = refs[:5]
    refs = refs[5:]
    if has_prev:
        refs.pop(0)
    z_ref, pout_ref, ext_scr, p_scr = refs
    ti = pl.program_id(1)
    for bi in range(nb):
        rows = pl.ds(bi * tb, tb)
        if has_state:
            ext_scr[0:1, :] = jnp.zeros((1, W_B), F32)
            ext_scr[1:POOL_HALO, :] = halo_ref[bi]
        else:
            ext_scr[0:POOL_HALO, :] = jnp.where(ti == 0, 0.0, halo_ref[...])
        ext_scr[POOL_HALO:POOL_HALO + tb, :] = u_ref[rows, :]
        t_idx = pos0 + ti * tb + lax.broadcasted_iota(jnp.int32, (tb, GW_B), 0)
        for gi, w in enumerate(POOL_WINDOWS):
            cols = slice(gi * GW_B, (gi + 1) * GW_B)
            acc = ext_scr[:, cols]
            shift = 1
            while shift < w:
                acc = acc + pltpu.roll(acc, shift, 0)
                shift *= 2
            cnt = jnp.minimum(t_idx + 1, w).astype(F32)
            u = ext_scr[POOL_HALO:POOL_HALO + tb, cols]
            p_scr[rows, cols] = acc[POOL_HALO:POOL_HALO + tb] / cnt - u

        @pl.when(ti == pl.num_programs(1) - 1)
        def _():
            _state_store(pout_ref, (bi,), ext_scr[tb + 1:tb + POOL_HALO, :], layer, has_prev)

    for gi in range(N_POOL_GROUPS):
        cols = slice(gi * GW_B, (gi + 1) * GW_B)
        pm = jnp.dot(p_scr[:, cols].astype(BF16), wmap_ref[gi], preferred_element_type=F32)
        pm = pm * scale_ref[:, cols]
        z_ref[:, cols] = (pm * _silu(g_ref[:, cols])).astype(z_ref.dtype)


def _pool(h_ab, row0, batch, t, pos0, state, layer, prev_outs, wmap, scale, *, name):
    has_state = state is not None
    if t >= POOL_ROWS:
        nb, tb = 1, POOL_ROWS
    else:
        nb, tb = SMALL_T_POOL_BATCH, t
    nt = t // tb
    rows = nb * tb
    rblk = lambda b, i: b * nt + i
    hblk = lambda b, i: rblk(b, i) + _block_offset(row0, rows)
    in_specs = [pl.BlockSpec((rows, W_B), lambda b, i: (hblk(b, i), OFF_UB // W_B)),
                pl.BlockSpec((rows, W_B), lambda b, i: (hblk(b, i), OFF_GB // W_B))]
    args = [h_ab, h_ab]
    if has_state:
        assert nt == 1
        in_specs.append(pl.BlockSpec((None, nb, POOL_BUF, W_B), lambda b, i: (layer, b, 0, 0)))
        args.append(state)
    else:
        assert nb == 1
        per = tb // POOL_HALO
        in_specs.append(pl.BlockSpec(
            (POOL_HALO, W_B), lambda b, i: (jnp.maximum(hblk(b, i) * per - 1, 0), OFF_UB // W_B)))
        args.append(h_ab)
    in_specs += [pl.BlockSpec((N_POOL_GROUPS, GW_B, GW_B), lambda b, i: (0, 0, 0)),
                 pl.BlockSpec((1, W_B), lambda b, i: (0, 0))]
    args += [wmap, scale]
    a_args, a_specs, aliases = _alias_args(prev_outs, len(args))
    z, p_out = pl.pallas_call(
        functools.partial(_pool_kernel, nb=nb, tb=tb, pos0=pos0, has_state=has_state,
                          layer=layer, has_prev=prev_outs is not None),
        out_shape=(jax.ShapeDtypeStruct((batch * t, W_B), BF16),
                   jax.ShapeDtypeStruct((DEPTH, batch, POOL_BUF, W_B), F32)),
        grid=(batch // nb, nt),
        in_specs=in_specs + a_specs,
        out_specs=(pl.BlockSpec((rows, W_B), lambda b, i: (rblk(b, i), 0)),
                   _state_out_spec((nb, POOL_BUF, W_B), lambda b, i: (b, 0, 0), layer,
                                   prev_outs is not None)),
        scratch_shapes=[pltpu.VMEM((POOL_HALO + tb, W_B), F32),
                        pltpu.VMEM((rows, W_B), F32)],
        input_output_aliases=aliases,
        compiler_params=_params("arbitrary", "arbitrary"),
        name=name,
    )(*args, *a_args)
    return z, (p_out,)


def _t5_bucket_table(bq):
    span = WINDOW + bq
    dist = np.arange(bq)[:, None] + WINDOW - np.arange(span)[None, :]
    max_exact = NUM_BUCKETS // 2
    d = np.maximum(dist, 0).astype(np.float32)
    large = max_exact + (np.log(np.maximum(d, np.float32(1.0)) / np.float32(max_exact))
                         / np.float32(math.log(MAX_DISTANCE / max_exact))
                         * np.float32(NUM_BUCKETS - max_exact)).astype(np.int32)
    large = np.minimum(large, NUM_BUCKETS - 1)
    bucket = np.where(dist < max_exact, np.maximum(dist, 0), large)
    valid = (dist >= 0) & (dist < WINDOW)
    return np.where(valid, bucket, -1).astype(np.int32)


def _attention_kernel(*refs, bq, nbat, has_cache, mask_first, layer, has_prev):
    refs = list(refs)
    (sinks_ref, relb_ref, q_ref, kc_ref, vc_ref, kp_ref, vp_ref, g_ref, bucket_ref) = refs[:9]
    refs = refs[9:]
    if has_prev:
        del refs[:2]
    z_ref, wk_ref, wv_ref, bias_scr, qbd_scr, s_scr, p_scr, o_scr, l_scr = refs
    span = WINDOW + bq
    hrows = H_C * bq
    grows = G_C * bq
    chunk = min(hrows, SOFTMAX_ROWS)
    one_matmul = hrows <= SOFTMAX_ROWS
    fmin = jnp.finfo(F32).min
    n = pl.program_id(1)

    @pl.when((pl.program_id(0) == 0) & (n == 0))
    def _():
        bucket = bucket_ref[...]
        col = lax.broadcasted_iota(jnp.int32, (bq, span), 1)
        for h in range(H_C):
            acc = jnp.where(bucket < 0, fmin, 0.0)
            for b in range(NUM_BUCKETS):
                acc = jnp.where(bucket == b, relb_ref[b, h], acc)
            acc = jnp.where(col == 0, sinks_ref[h], acc)
            bias_scr[0, h * bq:(h + 1) * bq, :] = acc
            if mask_first:
                bias_scr[1, h * bq:(h + 1) * bq, :] = jnp.where((col < WINDOW) & (col > 0), fmin, acc)

    lane_kv = lax.broadcasted_iota(jnp.int32, (1, KVW), 1) // HD_C
    low_half = lax.broadcasted_iota(jnp.int32, (1, 2 * HD_C), 1) < HD_C
    row0 = lax.broadcasted_iota(jnp.int32, (WINDOW, KVW), 0) == 0
    ones = jnp.ones((span, DENOM_W), BF16)
    kc = vc = None
    for bi in range(nbat):
        rows = slice(bi * bq, (bi + 1) * bq)
        base = bi * hrows
        if has_cache:
            kp, vp = kp_ref[bi], vp_ref[bi]
        elif bi == 0:
            kp, vp = kp_ref[...], vp_ref[...]
        else:
            kp, vp = kc, vc
        bsel = jnp.where(n == 0, 1, 0) if (mask_first and bi == 0) else 0
        kc = kc_ref[rows, :]
        vc = vc_ref[rows, :]
        kx = jnp.concatenate([jnp.where(row0, 0.0, kp), kc], axis=0).astype(BF16)
        vx = jnp.concatenate([jnp.where(row0, 0.0, vp), vc], axis=0).astype(BF16)
        q = q_ref[rows, :] * (HD_C ** -0.5)
        for kk in range(KV_C):
            qbd_scr[base + kk * grows:base + (kk + 1) * grows, :] = jnp.concatenate(
                [jnp.where(lane_kv == kk, q[:, g * KVW:(g + 1) * KVW], 0.0) for g in range(G_C)],
                axis=0).astype(BF16)
        s_scr[base:base + hrows, :] = lax.dot_general(
            qbd_scr[base:base + hrows, :], kx, (((1,), (1,)), ((), ())), preferred_element_type=F32)
        for c in range(hrows // chunk):
            rs = slice(base + c * chunk, base + (c + 1) * chunk)
            s = s_scr[rs, :] + bias_scr[bsel, c * chunk:(c + 1) * chunk, :]
            m = jnp.max(s, axis=-1, keepdims=True)
            p_scr[rs, :] = jnp.exp(s - m).astype(BF16)
        if one_matmul:
            o3 = jnp.dot(p_scr[base:base + hrows, :], jnp.concatenate([vx, ones], axis=1),
                         preferred_element_type=F32)
            for kk in range(KV_C):
                o_scr[base + kk * grows:base + (kk + 1) * grows, :] = (
                    o3[kk * grows:(kk + 1) * grows, (kk // 2) * 2 * HD_C:(kk // 2 + 1) * 2 * HD_C])
            l_scr[base:base + hrows, :] = o3[:, KVW:KVW + DENOM_W]
        else:
            for kk in range(KV_C):
                rk = slice(base + kk * grows, base + (kk + 1) * grows)
                slab = vx[:, (kk // 2) * 2 * HD_C:(kk // 2 + 1) * 2 * HD_C]
                vk = jnp.where(low_half if kk % 2 == 0 else ~low_half, slab, jnp.zeros_like(slab))
                ol = jnp.dot(p_scr[rk, :], jnp.concatenate([vk, ones], axis=1),
                             preferred_element_type=F32)
                o_scr[rk, :] = ol[:, :2 * HD_C]
                l_scr[rk, :] = ol[:, 2 * HD_C:]
        for g in range(G_C):
            halves = []
            for half in range(2):
                ra, rb = (slice(base + kk * grows + g * bq, base + kk * grows + (g + 1) * bq)
                          for kk in (2 * half, 2 * half + 1))
                num = jnp.where(low_half, o_scr[ra, :], o_scr[rb, :])
                den = jnp.where(low_half, l_scr[ra, :], l_scr[rb, :])
                halves.append(num * (1.0 / den))
            og = jnp.concatenate(halves, axis=1)
            cols = slice(g * KVW, (g + 1) * KVW)
            z_ref[rows, cols] = (og * _silu(g_ref[rows, cols])).astype(z_ref.dtype)

        if has_cache or bi == nbat - 1:
            @pl.when(n == pl.num_programs(1) - 1)
            def _():
                wk = kc if bq == WINDOW else jnp.concatenate([kp[bq:, :], kc], axis=0)
                wv = vc if bq == WINDOW else jnp.concatenate([vp[bq:, :], vc], axis=0)
                _state_store(wk_ref, (bi if has_cache else 0,), wk, layer, has_prev)
                _state_store(wv_ref, (bi if has_cache else 0,), wv, layer, has_prev)


def _attention(h_att, row0, batch, t, pos0, cache_k, cache_v, layer, prev_outs, sinks, rel_bias, *,
               name):
    bq = min(ATT_BLOCK, t)
    nb = t // bq
    has_cache = cache_k is not None
    nbat = SMALL_T_ATT_BATCH if has_cache else ATT_BLOCKS_PER_STEP
    nsteps = 1 if has_cache else nb // nbat
    wb = nbat if has_cache else 1
    span = WINDOW + bq
    assert pos0 == 0 or pos0 >= WINDOW
    mask_first = pos0 == 0
    rows = nbat * bq
    row = lambda b, n, *_: b * nsteps + n
    hrow = lambda b, n, *_: row(b, n) + _block_offset(row0, rows)
    in_specs = [
        pl.BlockSpec((rows, W_C), lambda b, n, *_: (hrow(b, n), OFF_QC // W_C)),
        pl.BlockSpec((rows, KVW), lambda b, n, *_: (hrow(b, n), OFF_KC // KVW)),
        pl.BlockSpec((rows, KVW), lambda b, n, *_: (hrow(b, n), OFF_VC // KVW)),
    ]
    args = [h_att, h_att, h_att]
    if has_cache:
        assert nb == 1
        in_specs += [pl.BlockSpec((None, nbat, WINDOW, KVW), lambda b, n, *_: (layer, b, 0, 0)),
                     pl.BlockSpec((None, nbat, WINDOW, KVW), lambda b, n, *_: (layer, b, 0, 0))]
        args += [cache_k, cache_v]
    else:
        assert bq == WINDOW and nb % nbat == 0
        prev = lambda b, n, *_: jnp.maximum(hrow(b, n) * nbat - 1, 0)
        in_specs += [pl.BlockSpec((WINDOW, KVW), lambda b, n, *_: (prev(b, n), OFF_KC // KVW)),
                     pl.BlockSpec((WINDOW, KVW), lambda b, n, *_: (prev(b, n), OFF_VC // KVW))]
        args += [h_att, h_att]
    in_specs += [
        pl.BlockSpec((rows, W_C), lambda b, n, *_: (hrow(b, n), OFF_GC // W_C)),
        pl.BlockSpec((bq, span), lambda b, n, *_: (0, 0)),
    ]
    args += [h_att, jnp.asarray(_t5_bucket_table(bq))]
    a_args, a_specs, aliases = _alias_args(prev_outs, 2 + len(args))
    srows = nbat * H_C * bq
    z, wk, wv = pl.pallas_call(
        functools.partial(_attention_kernel, bq=bq, nbat=nbat, has_cache=has_cache,
                          mask_first=mask_first, layer=layer, has_prev=prev_outs is not None),
        out_shape=(jax.ShapeDtypeStruct((batch * t, W_C), BF16),
                   jax.ShapeDtypeStruct((DEPTH, batch, WINDOW, KVW), F32),
                   jax.ShapeDtypeStruct((DEPTH, batch, WINDOW, KVW), F32)),
        grid_spec=pltpu.PrefetchScalarGridSpec(
            num_scalar_prefetch=2,
            grid=(batch // wb, nsteps),
            in_specs=in_specs + a_specs,
            out_specs=(pl.BlockSpec((rows, W_C), lambda b, n, *_: (row(b, n), 0)),
                       _state_out_spec((wb, WINDOW, KVW), lambda b, n, *_: (b, 0, 0), layer,
                                       prev_outs is not None),
                       _state_out_spec((wb, WINDOW, KVW), lambda b, n, *_: (b, 0, 0), layer,
                                       prev_outs is not None)),
            scratch_shapes=[pltpu.VMEM((2 if mask_first else 1, H_C * bq, span), F32),
                            pltpu.VMEM((srows, KVW), BF16),
                            pltpu.VMEM((srows, span), F32),
                            pltpu.VMEM((srows, span), BF16),
                            pltpu.VMEM((srows, 2 * HD_C), F32),
                            pltpu.VMEM((srows, DENOM_W), F32)]),
        input_output_aliases=aliases,
        compiler_params=_params("arbitrary", "arbitrary"),
        name=name,
    )(sinks, rel_bias, *args, *a_args)
    return z, (wk, wv)


def _out_kernel(x_ref, za_ref, zb_ref, zc_ref, ma_ref, mb_ref, mc_ref, wa_ref, wb_ref, wc_ref,
                wo_ref, lng_ref, lnb_ref, *rest, alpha, emit_bf16):
    if emit_bf16:
        y_ref, yb_ref, merged_scr, r_scr = rest
    else:
        y_ref, merged_scr, r_scr = rest

    @pl.when(pl.program_id(0) == 0)
    def _():
        r_scr[...] = jnp.zeros_like(r_scr)

    lng, lnb = lng_ref[...], lnb_ref[...]
    for rc in range(r_scr.shape[0] // LN_ROWS):
        rs = slice(rc * LN_ROWS, (rc + 1) * LN_ROWS)
        r = r_scr[rs, :]
        mu = jnp.mean(r, axis=-1, keepdims=True)
        d = r - mu
        var = jnp.mean(d * d, axis=-1, keepdims=True)
        out = d * lax.rsqrt(var + LN_EPS) * lng + lnb
        y_ref[rs, :] = out
        if emit_bf16:
            yb_ref[rs, :] = out.astype(BF16)

    za, zb, zc = za_ref[...], zb_ref[...], zc_ref[...]
    for c in range(D_MODEL // OUT_STAGE_TN):
        cs = slice(c * OUT_STAGE_TN, (c + 1) * OUT_STAGE_TN)
        merged = (_sigmoid(ma_ref[:, cs].astype(F32))
                  * jnp.dot(za, wa_ref[:, cs], preferred_element_type=F32))
        merged += (_sigmoid(mb_ref[:, cs].astype(F32))
                   * jnp.dot(zb, wb_ref[:, cs], preferred_element_type=F32))
        merged += (_sigmoid(mc_ref[:, cs].astype(F32))
                   * jnp.dot(zc, wc_ref[:, cs], preferred_element_type=F32))
        merged_scr[:, cs] = merged.astype(BF16)
    r_scr[...] = alpha * x_ref[...] + jnp.dot(merged_scr[...], wo_ref[...],
                                              preferred_element_type=F32)


def _out_stage(x, za, zb, zc, h_gate, row0, wa, wb, wc, wo, lng, lnb, alpha, *, emit_bf16, name):
    m = x.shape[0]
    tm = OUT_STAGE_TM
    nsteps = m // tm
    cur = lambda i: jnp.minimum(i, nsteps - 1)
    rowblk = lambda w: pl.BlockSpec((tm, w), lambda i: (cur(i), 0))
    gate = lambda j: pl.BlockSpec((tm, D_MODEL), lambda i: (cur(i) + _block_offset(row0, tm), j))
    outblk = pl.BlockSpec((tm, D_MODEL), lambda i: (jnp.maximum(i - 1, 0), 0))
    const = lambda a: pl.BlockSpec(a.shape, lambda i: (0,) * a.ndim, pipeline_mode=pl.Buffered(1))
    out_shape = [jax.ShapeDtypeStruct((m, D_MODEL), F32)]
    if emit_bf16:
        out_shape.append(jax.ShapeDtypeStruct((m, D_MODEL), BF16))
    outs = pl.pallas_call(
        functools.partial(_out_kernel, alpha=alpha, emit_bf16=emit_bf16),
        out_shape=out_shape,
        grid=(nsteps + 1,),
        in_specs=[rowblk(D_MODEL), rowblk(W_A), rowblk(W_B), rowblk(W_C),
                  gate(0), gate(1), gate(2),
                  const(wa), const(wb), const(wc), const(wo), const(lng), const(lnb)],
        out_specs=[outblk] * len(out_shape),
        scratch_shapes=[pltpu.VMEM((tm, D_MODEL), BF16), pltpu.VMEM((tm, D_MODEL), F32)],
        compiler_params=_params("arbitrary"),
        name=name,
    )(x, za, zb, zc, h_gate, h_gate, h_gate, wa, wb, wc, wo, lng, lnb)
    return (outs[0], outs[1]) if emit_bf16 else (outs[0], None)


def _layer(groups, xs, xbs, layer, prevs, w, w_in, alpha):
    if xbs is None:
        h_att, xb_all = _in_proj(xs, w["w_att"], N_ATT, emit_xb=True, name=f"in_proj_att_{layer}")
        xbs = (xb_all,)
    else:
        h_att = _in_proj(xbs, w["w_att"], N_ATT, name=f"in_proj_att_{layer}")
    h_ab = _in_proj(xbs, w_in, N_AB, layer=layer, col0=0, name=f"in_proj_ab_{layer}")
    h_gate = _in_proj(xbs, w_in, N_GATE, layer=layer, col0=REF_GATE, out_dtype=BF16,
                      name=f"in_proj_gate_{layer}")
    ys, ybs, outs, row0 = [], [], [], 0
    for (batch, t, pos0, states, tag), x, prev in zip(groups, xs, prevs):
        st_ret, st_k, st_v, st_pool = states if states is not None else (None,) * 4
        pv_ret, pv_win, pv_pool = prev if prev is not None else (None,) * 3
        za, o_ret = _retention(h_ab, row0, batch, t, pos0, st_ret, layer, pv_ret,
                               name=f"retention_{tag}{layer}")
        zb, o_pool = _pool(h_ab, row0, batch, t, pos0, st_pool, layer, pv_pool, w["w_pool_map"],
                           w["pool_scale"], name=f"pool_{tag}{layer}")
        zc, o_win = _attention(h_att, row0, batch, t, pos0, st_k, st_v, layer, pv_win, w["sinks"],
                               w["rel_bias"], name=f"attention_{tag}{layer}")
        y, yb = _out_stage(x, za, zb, zc, h_gate, row0, w["w_ret_o"], w["w_pool_o"], w["w_att_o"],
                           w["w_out"], w["ln_g"], w["ln_b"], alpha, emit_bf16=layer + 1 < DEPTH,
                           name=f"out_stage_{tag}{layer}")
        ys.append(y)
        ybs.append(yb)
        outs.append((o_ret, o_win, o_pool))
        row0 += batch * t
    return ys, (tuple(ybs) if layer + 1 < DEPTH else None), outs


def _regroup_heads(w, axis):
    shp = w.shape
    w = w.reshape(shp[:axis] + (KV_C, G_C, HD_C) + shp[axis + 1:])
    w = jnp.swapaxes(w, axis, axis + 1)
    return w.reshape(shp)


def _prep_w_att(w_in, l):
    cols = lambda c0, n: lax.slice(w_in, (l, 0, c0), (l + 1, D_MODEL, c0 + n)).reshape(D_MODEL, n)
    w = jnp.concatenate([
        _regroup_heads(cols(REF_QC, W_C), 1),
        _regroup_heads(cols(REF_GC, W_C), 1),
        cols(REF_KC, KVW),
        cols(REF_VC, KVW)], axis=1)
    return _to_bf16(w, name=f"w_att_bf16_{l}")


def _cast_kernel(x_ref, o_ref):
    o_ref[...] = x_ref[...].astype(o_ref.dtype)


def _to_bf16(w, *, name):
    r, c = w.shape
    return pl.pallas_call(
        _cast_kernel,
        out_shape=jax.ShapeDtypeStruct((r, c), BF16),
        grid=(r // CAST_ROWS,),
        in_specs=[pl.BlockSpec((CAST_ROWS, c), lambda i: (i, 0))],
        out_specs=pl.BlockSpec((CAST_ROWS, c), lambda i: (i, 0)),
        compiler_params=_params("parallel"),
        name=name,
    )(w)


def kernel(x_prompt, x_sample, state_ret, cache_win_k, cache_win_v, state_pool, w_in, w_ret_o,
           w_pool_map, pool_scale, w_pool_o, attn_sinks, w_att_o, w_out, ln_g, ln_b, rel_bias):
    alpha = (2.0 * DEPTH) ** 0.25
    bp, tp, _ = x_prompt.shape
    bs, ts, _ = x_sample.shape
    xp = x_prompt.reshape(bp * tp, D_MODEL)
    xs = x_sample.reshape(bs * ts, D_MODEL)
    ck = cache_win_k.reshape(DEPTH, bs, WINDOW, KVW)
    cv = cache_win_v.reshape(DEPTH, bs, WINDOW, KVW)
    groups = ((bp, tp, 0, None, "p"), (bs, ts, PAST_LEN, (state_ret, ck, cv, state_pool), "s"))
    xs, xbs, outs = (xp, xs), None, (None, None)
    for l in range(DEPTH):
        w = {
            "w_att": _prep_w_att(w_in, l),
            "w_ret_o": w_ret_o[l].astype(BF16),
            "w_pool_map": w_pool_map[l].astype(BF16),
            "pool_scale": pool_scale[l].reshape(1, W_B),
            "w_pool_o": w_pool_o[l].astype(BF16),
            "sinks": attn_sinks[l],
            "w_att_o": _regroup_heads(w_att_o[l], 0).astype(BF16),
            "w_out": w_out[l].astype(BF16),
            "ln_g": ln_g[l].reshape(1, D_MODEL),
            "ln_b": ln_b[l].reshape(1, D_MODEL),
            "rel_bias": rel_bias,
        }
        xs, xbs, outs = _layer(groups, xs, xbs, l, outs, w, w_in, alpha)
    ((ret_p,), (kp, vp), (pp,)), ((ret_s,), (ks, vs), (ps,)) = outs
    win = lambda a, b: a.reshape(DEPTH, b, WINDOW, KV_C, HD_C)
    return (xs[0].reshape(bp, tp, D_MODEL), xs[1].reshape(bs, ts, D_MODEL), ret_p, ret_s,
            win(kp, bp), win(ks, bs), win(vp, bp), win(vs, bs), pp, ps)
```

```python
import functools
import math

import jax
import jax.numpy as jnp
import numpy as np
from jax import lax
from jax.experimental import pallas as pl
from jax.experimental.pallas import tpu as pltpu

D_MODEL = 2048
DEPTH = 2
PAST_LEN = 8192
H_A, DK_A, DV_A = 4, 128, 256
W_A = H_A * DV_A
RET_CHUNK = 128
ROPE_BASE = 10000.0
N_POOL_GROUPS = 4
W_B = 1024
GW_B = W_B // N_POOL_GROUPS
POOL_WINDOWS = (2, 4, 8, 16)
POOL_BUF = 15
POOL_HALO = 16
H_C, KV_C, HD_C = 16, 4, 64
G_C = H_C // KV_C
W_C = H_C * HD_C
KVW = KV_C * HD_C
WINDOW = 128
ATT_BLOCK = 128
SOFTMAX_ROWS = 128
DENOM_W = 2 * HD_C
NUM_BUCKETS = 32
MAX_DISTANCE = 128
LN_EPS = 1e-5
RMS_EPS = 1e-6

REF_QC, REF_KC, REF_VC, REF_GC = 5120, 6144, 6400, 6656
N_AB = 5120
N_ATT = 2560
REF_GATE = 7680
N_GATE = 3 * D_MODEL
OFF_QA, OFF_KA, OFF_VA, OFF_GA, OFF_UB, OFF_GB = 0, 512, 1024, 2048, 3072, 4096
OFF_QC, OFF_GC, OFF_KC, OFF_VC = 0, 1024, 2048, 2304

F32 = jnp.float32
BF16 = jnp.bfloat16
VMEM_LIMIT = 56 * 1024 * 1024

IN_PROJ_TM = 1024
IN_PROJ_TM_F32_X = 512
IN_PROJ_TN = {N_AB: 1280, N_ATT: 2560, N_GATE: 1536}
OUT_STAGE_TM = 256
OUT_STAGE_TN = 512
LN_ROWS = 16
POOL_ROWS = 1024
SMALL_T_POOL_BATCH = 16
W_PREP_ROWS = 512
ATT_BLOCKS_PER_STEP = 2
SMALL_T_ATT_BATCH = 8
RET_ROWS = 512
SMALL_T_RET_BATCH = 4


def _sigmoid(x):
    return 1.0 / (1.0 + jnp.exp(-x))


def _silu(x):
    return x * _sigmoid(x)


def _params(*sem):
    return pltpu.CompilerParams(dimension_semantics=sem, vmem_limit_bytes=VMEM_LIMIT)


def _alias_args(prev_outs, n_in):
    if prev_outs is None:
        return [], [], {}
    specs = [pl.BlockSpec(memory_space=pl.ANY) for _ in prev_outs]
    return list(prev_outs), specs, {n_in + i: 1 + i for i in range(len(prev_outs))}


def _block_offset(row0, rows):
    assert row0 % rows == 0
    return row0 // rows


def _state_out_spec(block, index_map, layer, has_prev):
    if has_prev:
        return pl.BlockSpec((None,) + block, lambda *a: (layer,) + index_map(*a))
    return pl.BlockSpec((DEPTH,) + block, lambda *a: (0,) + index_map(*a))


def _state_store(ref, idx, value, layer, has_prev):
    if has_prev:
        ref[idx] = value
    else:
        for l in range(DEPTH):
            ref[(l,) + idx] = value if l == layer else jnp.zeros_like(value)


def _matmul_kernel(*refs, n_x, first_blocks, cast_w, emit_xb):
    refs = list(refs)
    x_refs, (w_ref, o_ref), rest = refs[:n_x], refs[n_x:n_x + 2], refs[n_x + 2:]
    xb_ref = rest.pop(0) if emit_xb else None
    if cast_w:
        wb_scr, = rest

        @pl.when(pl.program_id(1) == 0)
        def _():
            wb_scr[...] = w_ref[...].astype(BF16)
        w_ref = wb_scr

    def project(x_ref):
        x = x_ref[...].astype(BF16)
        if emit_xb:
            xb_ref[...] = x
        o_ref[...] = jnp.dot(x, w_ref[...], preferred_element_type=F32).astype(o_ref.dtype)

    if n_x == 1:
        project(x_refs[0])
    else:
        i = pl.program_id(1)
        pl.when(i < first_blocks)(lambda: project(x_refs[0]))
        pl.when(i >= first_blocks)(lambda: project(x_refs[1]))


def _in_proj(xs, w, n, *, layer=None, col0=0, emit_xb=False, out_dtype=F32, name):
    assert len(xs) in (1, 2)
    k = xs[0].shape[1]
    m = sum(x.shape[0] for x in xs)
    tm, tn = (IN_PROJ_TM_F32_X if xs[0].dtype == F32 else IN_PROJ_TM), IN_PROJ_TN[n]
    first = xs[0].shape[0] // tm
    assert all(x.shape[0] % tm == 0 for x in xs)
    cast_w = w.dtype == F32
    if cast_w:
        w_spec = pl.BlockSpec((None, k, tn), lambda j, i: (layer, 0, col0 // tn + j))
    else:
        w_spec = pl.BlockSpec((k, tn), lambda j, i: (0, j),
                              pipeline_mode=pl.Buffered(1) if n == tn else None)
    if len(xs) == 1:
        x_specs = [pl.BlockSpec((tm, k), lambda j, i: (i, 0))]
    else:
        last = xs[1].shape[0] // tm - 1
        x_specs = [pl.BlockSpec((tm, k), lambda j, i: (jnp.minimum(i, first - 1), 0)),
                   pl.BlockSpec((tm, k), lambda j, i: (jnp.clip(i - first, 0, last), 0),
                                pipeline_mode=None if last else pl.Buffered(1))]
    assert not emit_xb or n == tn
    out_shape = [jax.ShapeDtypeStruct((m, n), out_dtype)]
    out_specs = [pl.BlockSpec((tm, tn), lambda j, i: (i, j))]
    if emit_xb:
        out_shape.append(jax.ShapeDtypeStruct((m, k), BF16))
        out_specs.append(pl.BlockSpec((tm, k), lambda j, i: (i, 0)))
    outs = pl.pallas_call(
        functools.partial(_matmul_kernel, n_x=len(xs), first_blocks=first, cast_w=cast_w,
                          emit_xb=emit_xb),
        out_shape=out_shape,
        grid=(n // tn, m // tm),
        in_specs=x_specs + [w_spec],
        out_specs=out_specs,
        scratch_shapes=[pltpu.VMEM((k, tn), BF16)] if cast_w else [],
        compiler_params=_params("arbitrary", "arbitrary"),
        name=name,
    )(*xs, w)
    return outs if emit_xb else outs[0]


def _retention_kernel(*refs, chunk, nchunk, nbat, has_state, layer, has_prev):
    refs = list(refs)
    (q_ref, k_ref, v_ref, g_ref, cos_ref, sin_ref, dmask_ref, qdec_ref, kdec_ref, cdec_ref) = refs[:10]
    refs = refs[10:]
    s0_ref = refs.pop(0) if has_state else None
    if has_prev:
        refs.pop(0)
    z_ref, sout_ref, s_scr = refs
    c = pl.program_id(1)

    @pl.when(c == 0)
    def _():
        if has_state:
            s_scr[...] = s0_ref[...]
        else:
            s_scr[...] = jnp.zeros_like(s_scr)

    for bi in range(nbat):
        for hd in range(H_A):
            s = s_scr[bi, hd]
            for ci in range(nchunk):
                rows = slice((bi * nchunk + ci) * chunk, (bi * nchunk + ci + 1) * chunk)
                trow = slice(ci * chunk, (ci + 1) * chunk)
                cos = cos_ref[trow, :]
                sin = sin_ref[trow, :]
                q = q_ref[rows, hd * DK_A:(hd + 1) * DK_A]
                k = k_ref[rows, hd * DK_A:(hd + 1) * DK_A]
                v = v_ref[rows, hd * DV_A:(hd + 1) * DV_A].astype(BF16)
                qr = q * cos + pltpu.roll(q, DK_A // 2, 1) * sin
                kr = (k * cos + pltpu.roll(k, DK_A // 2, 1) * sin) * (DK_A ** -0.5)
                qb = qr.astype(BF16)
                sc = lax.dot_general(qb, kr.astype(BF16), (((1,), (1,)), ((), ())),
                                     preferred_element_type=F32) * dmask_ref[hd]
                o = (jnp.dot(sc.astype(BF16), v, preferred_element_type=F32)
                     + jnp.dot(qb, s.astype(BF16), preferred_element_type=F32) * qdec_ref[hd])
                kd = (kr * kdec_ref[hd]).astype(BF16)
                s = s * cdec_ref[hd] + lax.dot_general(
                    kd, v, (((0,), (0,)), ((), ())), preferred_element_type=F32)
                o = o * lax.rsqrt(jnp.mean(o * o, axis=-1, keepdims=True) + RMS_EPS)
                g = g_ref[rows, hd * DV_A:(hd + 1) * DV_A]
                z_ref[rows, hd * DV_A:(hd + 1) * DV_A] = (o * _silu(g)).astype(z_ref.dtype)
            s_scr[bi, hd] = s

    @pl.when(c == pl.num_programs(1) - 1)
    def _():
        _state_store(sout_ref, (Ellipsis,), s_scr[...], layer, has_prev)


def _retention_tables(pos0, t, chunk):
    half = DK_A // 2
    inv = ROPE_BASE ** (-jnp.arange(half, dtype=F32) / half)
    pos = pos0 + jnp.arange(t, dtype=F32)
    ang = pos[:, None] * inv[None, :]
    cos, sin = jnp.cos(ang), jnp.sin(ang)
    cos2 = jnp.concatenate([cos, cos], -1)
    sin2 = jnp.concatenate([-sin, sin], -1)
    lg = jnp.log1p(-jnp.exp2(-5.0 - jnp.arange(H_A, dtype=F32)))
    idx = jnp.arange(chunk, dtype=F32)
    diff = idx[:, None] - idx[None, :]
    dmask = jnp.where(diff >= 0, jnp.exp(lg[:, None, None] * jnp.maximum(diff, 0.0)), 0.0)
    qdec = jnp.exp(lg[:, None] * (idx[None, :] + 1.0))
    kdec = jnp.exp(lg[:, None] * (chunk - 1.0 - idx[None, :]))
    cdec = jnp.exp(lg * chunk)
    qdec = jnp.broadcast_to(qdec[:, :, None], (H_A, chunk, DV_A))
    kdec = jnp.broadcast_to(kdec[:, :, None], (H_A, chunk, DK_A))
    cdec = jnp.broadcast_to(cdec[:, None, None], (H_A, 1, DV_A))
    return cos2, sin2, dmask, qdec, kdec, cdec


def _retention(h_ab, row0, batch, t, pos0, state, layer, prev_outs, *, name):
    chunk = min(RET_CHUNK, t)
    if t > chunk:
        nbat, nchunk = 1, RET_ROWS // chunk
    else:
        nbat, nchunk = SMALL_T_RET_BATCH, 1
    rows = nbat * nchunk * chunk
    nsteps = t // (nchunk * chunk)
    has_state = state is not None
    tables = _retention_tables(pos0, t, chunk)
    row = lambda b, c: b * nsteps + c
    hrow = lambda b, c: row(b, c) + _block_offset(row0, rows)
    in_specs = [
        pl.BlockSpec((rows, H_A * DK_A), lambda b, c: (hrow(b, c), OFF_QA // (H_A * DK_A))),
        pl.BlockSpec((rows, H_A * DK_A), lambda b, c: (hrow(b, c), OFF_KA // (H_A * DK_A))),
        pl.BlockSpec((rows, W_A), lambda b, c: (hrow(b, c), OFF_VA // W_A)),
        pl.BlockSpec((rows, W_A), lambda b, c: (hrow(b, c), OFF_GA // W_A)),
        pl.BlockSpec((nchunk * chunk, DK_A), lambda b, c: (c, 0)),
        pl.BlockSpec((nchunk * chunk, DK_A), lambda b, c: (c, 0)),
        pl.BlockSpec((H_A, chunk, chunk), lambda b, c: (0, 0, 0)),
        pl.BlockSpec((H_A, chunk, DV_A), lambda b, c: (0, 0, 0)),
        pl.BlockSpec((H_A, chunk, DK_A), lambda b, c: (0, 0, 0)),
        pl.BlockSpec((H_A, 1, DV_A), lambda b, c: (0, 0, 0)),
    ]
    args = [h_ab, h_ab, h_ab, h_ab, *tables]
    if has_state:
        in_specs.append(pl.BlockSpec((None, nbat, H_A, DK_A, DV_A), lambda b, c: (layer, b, 0, 0, 0)))
        args.append(state)
    a_args, a_specs, aliases = _alias_args(prev_outs, len(args))
    z, s_out = pl.pallas_call(
        functools.partial(_retention_kernel, chunk=chunk, nchunk=nchunk, nbat=nbat,
                          has_state=has_state, layer=layer, has_prev=prev_outs is not None),
        out_shape=(jax.ShapeDtypeStruct((batch * t, W_A), BF16),
                   jax.ShapeDtypeStruct((DEPTH, batch, H_A, DK_A, DV_A), F32)),
        grid=(batch // nbat, nsteps),
        in_specs=in_specs + a_specs,
        out_specs=(pl.BlockSpec((rows, W_A), lambda b, c: (row(b, c), 0)),
                   _state_out_spec((nbat, H_A, DK_A, DV_A), lambda b, c: (b, 0, 0, 0), layer,
                                   prev_outs is not None)),
        scratch_shapes=[pltpu.VMEM((nbat, H_A, DK_A, DV_A), F32)],
        input_output_aliases=aliases,
        compiler_params=_params("arbitrary", "arbitrary"),
        name=name,
    )(*args, *a_args)
    return z, (s_out,)


def _pool_kernel(*refs, nb, tb, pos0, has_state, layer, has_prev):
    refs = list(refs)
    u_ref, g_ref, halo_ref, wmap_ref, scale_ref = refs[:5]
    refs = refs[5:]
    if has_prev:
        refs.pop(0)
    z_ref, pout_ref, ext_scr, p_scr = refs
    ti = pl.program_id(1)
    for bi in range(nb):
        rows = pl.ds(bi * tb, tb)
        if has_state:
            ext_scr[0:1, :] = jnp.zeros((1, W_B), F32)
            ext_scr[1:POOL_HALO, :] = halo_ref[bi]
        else:
            ext_scr[0:POOL_HALO, :] = jnp.where(ti == 0, 0.0, halo_ref[...])
        ext_scr[POOL_HALO:POOL_HALO + tb, :] = u_ref[rows, :]
        t_idx = pos0 + ti * tb + lax.broadcasted_iota(jnp.int32, (tb, GW_B), 0)
        for gi, w in enumerate(POOL_WINDOWS):
            cols = slice(gi * GW_B, (gi + 1) * GW_B)
            acc = ext_scr[:, cols]
            shift = 1
            while shift < w:
                acc = acc + pltpu.roll(acc, shift, 0)
                shift *= 2
            cnt = jnp.minimum(t_idx + 1, w).astype(F32)
            u = ext_scr[POOL_HALO:POOL_HALO + tb, cols]
            p_scr[rows, cols] = acc[POOL_HALO:POOL_HALO + tb] / cnt - u

        @pl.when(ti == pl.num_programs(1) - 1)
        def _():
            _state_store(pout_ref, (bi,), ext_scr[tb + 1:tb + POOL_HALO, :], layer, has_prev)

    for gi in range(N_POOL_GROUPS):
        cols = slice(gi * GW_B, (gi + 1) * GW_B)
        pm = jnp.dot(p_scr[:, cols].astype(BF16), wmap_ref[gi], preferred_element_type=F32)
        pm = pm * scale_ref[:, cols]
        z_ref[:, cols] = (pm * _silu(g_ref[:, cols])).astype(z_ref.dtype)


def _pool(h_ab, row0, batch, t, pos0, state, layer, prev_outs, wmap, scale, *, name):
    has_state = state is not None
    if t >= POOL_ROWS:
        nb, tb = 1, POOL_ROWS
    else:
        nb, tb = SMALL_T_POOL_BATCH, t
    nt = t // tb
    rows = nb * tb
    rblk = lambda b, i: b * nt + i
    hblk = lambda b, i: rblk(b, i) + _block_offset(row0, rows)
    in_specs = [pl.BlockSpec((rows, W_B), lambda b, i: (hblk(b, i), OFF_UB // W_B)),
                pl.BlockSpec((rows, W_B), lambda b, i: (hblk(b, i), OFF_GB // W_B))]
    args = [h_ab, h_ab]
    if has_state:
        assert nt == 1
        in_specs.append(pl.BlockSpec((None, nb, POOL_BUF, W_B), lambda b, i: (layer, b, 0, 0)))
        args.append(state)
    else:
        assert nb == 1
        per = tb // POOL_HALO
        in_specs.append(pl.BlockSpec(
            (POOL_HALO, W_B), lambda b, i: (jnp.maximum(hblk(b, i) * per - 1, 0), OFF_UB // W_B)))
        args.append(h_ab)
    in_specs += [pl.BlockSpec((N_POOL_GROUPS, GW_B, GW_B), lambda b, i: (0, 0, 0)),
                 pl.BlockSpec((1, W_B), lambda b, i: (0, 0))]
    args += [wmap, scale]
    a_args, a_specs, aliases = _alias_args(prev_outs, len(args))
    z, p_out = pl.pallas_call(
        functools.partial(_pool_kernel, nb=nb, tb=tb, pos0=pos0, has_state=has_state,
                          layer=layer, has_prev=prev_outs is not None),
        out_shape=(jax.ShapeDtypeStruct((batch * t, W_B), BF16),
                   jax.ShapeDtypeStruct((DEPTH, batch, POOL_BUF, W_B), F32)),
        grid=(batch // nb, nt),
        in_specs=in_specs + a_specs,
        out_specs=(pl.BlockSpec((rows, W_B), lambda b, i: (rblk(b, i), 0)),
                   _state_out_spec((nb, POOL_BUF, W_B), lambda b, i: (b, 0, 0), layer,
                                   prev_outs is not None)),
        scratch_shapes=[pltpu.VMEM((POOL_HALO + tb, W_B), F32),
                        pltpu.VMEM((rows, W_B), F32)],
        input_output_aliases=aliases,
        compiler_params=_params("arbitrary", "arbitrary"),
        name=name,
    )(*args, *a_args)
    return z, (p_out,)


def _t5_bucket_table(bq):
    span = WINDOW + bq
    dist = np.arange(bq)[:, None] + WINDOW - np.arange(span)[None, :]
    max_exact = NUM_BUCKETS // 2
    d = np.maximum(dist, 0).astype(np.float32)
    large = max_exact + (np.log(np.maximum(d, np.float32(1.0)) / np.float32(max_exact))
                         / np.float32(math.log(MAX_DISTANCE / max_exact))
                         * np.float32(NUM_BUCKETS - max_exact)).astype(np.int32)
    large = np.minimum(large, NUM_BUCKETS - 1)
    bucket = np.where(dist < max_exact, np.maximum(dist, 0), large)
    valid = (dist >= 0) & (dist < WINDOW)
    return np.where(valid, bucket, -1).astype(np.int32)


def _attention_kernel(*refs, bq, nbat, has_cache, mask_first, layer, has_prev):
    refs = list(refs)
    (sinks_ref, relb_ref, q_ref, kc_ref, vc_ref, kp_ref, vp_ref, g_ref, bucket_ref) = refs[:9]
    refs = refs[9:]
    if has_prev:
        del refs[:2]
    z_ref, wk_ref, wv_ref, bias_scr, qbd_scr, s_scr, p_scr, o_scr, l_scr = refs
    span = WINDOW + bq
    hrows = H_C * bq
    grows = G_C * bq
    chunk = min(hrows, SOFTMAX_ROWS)
    one_matmul = hrows <= SOFTMAX_ROWS
    fmin = jnp.finfo(F32).min
    n = pl.program_id(1)

    @pl.when((pl.program_id(0) == 0) & (n == 0))
    def _():
        bucket = bucket_ref[...]
        col = lax.broadcasted_iota(jnp.int32, (bq, span), 1)
        for h in range(H_C):
            acc = jnp.where(bucket < 0, fmin, 0.0)
            for b in range(NUM_BUCKETS):
                acc = jnp.where(bucket == b, relb_ref[b, h], acc)
            acc = jnp.where(col == 0, sinks_ref[h], acc)
            bias_scr[0, h * bq:(h + 1) * bq, :] = acc
            if mask_first:
                bias_scr[1, h * bq:(h + 1) * bq, :] = jnp.where((col < WINDOW) & (col > 0), fmin, acc)

    lane_kv = lax.broadcasted_iota(jnp.int32, (1, KVW), 1) // HD_C
    low_half = lax.broadcasted_iota(jnp.int32, (1, 2 * HD_C), 1) < HD_C
    row0 = lax.broadcasted_iota(jnp.int32, (WINDOW, KVW), 0) == 0
    ones = jnp.ones((span, DENOM_W), BF16)
    kc = vc = None
    for bi in range(nbat):
        rows = slice(bi * bq, (bi + 1) * bq)
        base = bi * hrows
        if has_cache:
            kp, vp = kp_ref[bi], vp_ref[bi]
        elif bi == 0:
            kp, vp = kp_ref[...], vp_ref[...]
        else:
            kp, vp = kc, vc
        bsel = jnp.where(n == 0, 1, 0) if (mask_first and bi == 0) else 0
        kc = kc_ref[rows, :]
        vc = vc_ref[rows, :]
        kx = jnp.concatenate([jnp.where(row0, 0.0, kp), kc], axis=0).astype(BF16)
        vx = jnp.concatenate([jnp.where(row0, 0.0, vp), vc], axis=0).astype(BF16)
        q = q_ref[rows, :] * (HD_C ** -0.5)
        for kk in range(KV_C):
            qbd_scr[base + kk * grows:base + (kk + 1) * grows, :] = jnp.concatenate(
                [jnp.where(lane_kv == kk, q[:, g * KVW:(g + 1) * KVW], 0.0) for g in range(G_C)],
                axis=0).astype(BF16)
        s_scr[base:base + hrows, :] = lax.dot_general(
            qbd_scr[base:base + hrows, :], kx, (((1,), (1,)), ((), ())), preferred_element_type=F32)
        for c in range(hrows // chunk):
            rs = slice(base + c * chunk, base + (c + 1) * chunk)
            s = s_scr[rs, :] + bias_scr[bsel, c * chunk:(c + 1) * chunk, :]
            m = jnp.max(s, axis=-1, keepdims=True)
            p_scr[rs, :] = jnp.exp(s - m).astype(BF16)
        if one_matmul:
            o3 = jnp.dot(p_scr[base:base + hrows, :], jnp.concatenate([vx, ones], axis=1),
                         preferred_element_type=F32)
            for kk in range(KV_C):
                o_scr[base + kk * grows:base + (kk + 1) * grows, :] = (
                    o3[kk * grows:(kk + 1) * grows, (kk // 2) * 2 * HD_C:(kk // 2 + 1) * 2 * HD_C])
            l_scr[base:base + hrows, :] = o3[:, KVW:KVW + DENOM_W]
        else:
            for kk in range(KV_C):
                rk = slice(base + kk * grows, base + (kk + 1) * grows)
                slab = vx[:, (kk // 2) * 2 * HD_C:(kk // 2 + 1) * 2 * HD_C]
                vk = jnp.where(low_half if kk % 2 == 0 else ~low_half, slab, jnp.zeros_like(slab))
                ol = jnp.dot(p_scr[rk, :], jnp.concatenate([vk, ones], axis=1),
                             preferred_element_type=F32)
                o_scr[rk, :] = ol[:, :2 * HD_C]
                l_scr[rk, :] = ol[:, 2 * HD_C:]
        for g in range(G_C):
            halves = []
            for half in range(2):
                ra, rb = (slice(base + kk * grows + g * bq, base + kk * grows + (g + 1) * bq)
                          for kk in (2 * half, 2 * half + 1))
                num = jnp.where(low_half, o_scr[ra, :], o_scr[rb, :])
                den = jnp.where(low_half, l_scr[ra, :], l_scr[rb, :])
                halves.append(num * (1.0 / den))
            og = jnp.concatenate(halves, axis=1)
            cols = slice(g * KVW, (g + 1) * KVW)
            z_ref[rows, cols] = (og * _silu(g_ref[rows, cols])).astype(z_ref.dtype)

        if has_cache or bi == nbat - 1:
            @pl.when(n == pl.num_programs(1) - 1)
            def _():
                wk = kc if bq == WINDOW else jnp.concatenate([kp[bq:, :], kc], axis=0)
                wv = vc if bq == WINDOW else jnp.concatenate([vp[bq:, :], vc], axis=0)
                _state_store(wk_ref, (bi if has_cache else 0,), wk, layer, has_prev)
                _state_store(wv_ref, (bi if has_cache else 0,), wv, layer, has_prev)


def _attention(h_att, row0, batch, t, pos0, cache_k, cache_v, layer, prev_outs, sinks, rel_bias, *,
               name):
    bq = min(ATT_BLOCK, t)
    nb = t // bq
    has_cache = cache_k is not None
    nbat = SMALL_T_ATT_BATCH if has_cache else ATT_BLOCKS_PER_STEP
    nsteps = 1 if has_cache else nb // nbat
    wb = nbat if has_cache else 1
    span = WINDOW + bq
    assert pos0 == 0 or pos0 >= WINDOW
    mask_first = pos0 == 0
    rows = nbat * bq
    row = lambda b, n, *_: b * nsteps + n
    hrow = lambda b, n, *_: row(b, n) + _block_offset(row0, rows)
    in_specs = [
        pl.BlockSpec((rows, W_C), lambda b, n, *_: (hrow(b, n), OFF_QC // W_C)),
        pl.BlockSpec((rows, KVW), lambda b, n, *_: (hrow(b, n), OFF_KC // KVW)),
        pl.BlockSpec((rows, KVW), lambda b, n, *_: (hrow(b, n), OFF_VC // KVW)),
    ]
    args = [h_att, h_att, h_att]
    if has_cache:
        assert nb == 1
        in_specs += [pl.BlockSpec((None, nbat, WINDOW, KVW), lambda b, n, *_: (layer, b, 0, 0)),
                     pl.BlockSpec((None, nbat, WINDOW, KVW), lambda b, n, *_: (layer, b, 0, 0))]
        args += [cache_k, cache_v]
    else:
        assert bq == WINDOW and nb % nbat == 0
        prev = lambda b, n, *_: jnp.maximum(hrow(b, n) * nbat - 1, 0)
        in_specs += [pl.BlockSpec((WINDOW, KVW), lambda b, n, *_: (prev(b, n), OFF_KC // KVW)),
                     pl.BlockSpec((WINDOW, KVW), lambda b, n, *_: (prev(b, n), OFF_VC // KVW))]
        args += [h_att, h_att]
    in_specs += [
        pl.BlockSpec((rows, W_C), lambda b, n, *_: (hrow(b, n), OFF_GC // W_C)),
        pl.BlockSpec((bq, span), lambda b, n, *_: (0, 0)),
    ]
    args += [h_att, jnp.asarray(_t5_bucket_table(bq))]
    a_args, a_specs, aliases = _alias_args(prev_outs, 2 + len(args))
    srows = nbat * H_C * bq
    z, wk, wv = pl.pallas_call(
        functools.partial(_attention_kernel, bq=bq, nbat=nbat, has_cache=has_cache,
                          mask_first=mask_first, layer=layer, has_prev=prev_outs is not None),
        out_shape=(jax.ShapeDtypeStruct((batch * t, W_C), BF16),
                   jax.ShapeDtypeStruct((DEPTH, batch, WINDOW, KVW), F32),
                   jax.ShapeDtypeStruct((DEPTH, batch, WINDOW, KVW), F32)),
        grid_spec=pltpu.PrefetchScalarGridSpec(
            num_scalar_prefetch=2,
            grid=(batch // wb, nsteps),
            in_specs=in_specs + a_specs,
            out_specs=(pl.BlockSpec((rows, W_C), lambda b, n, *_: (row(b, n), 0)),
                       _state_out_spec((wb, WINDOW, KVW), lambda b, n, *_: (b, 0, 0), layer,
                                       prev_outs is not None),
                       _state_out_spec((wb, WINDOW, KVW), lambda b, n, *_: (b, 0, 0), layer,
                                       prev_outs is not None)),
            scratch_shapes=[pltpu.VMEM((2 if mask_first else 1, H_C * bq, span), F32),
                            pltpu.VMEM((srows, KVW), BF16),
                            pltpu.VMEM((srows, span), F32),
                            pltpu.VMEM((srows, span), BF16),
                            pltpu.VMEM((srows, 2 * HD_C), F32),
                            pltpu.VMEM((srows, DENOM_W), F32)]),
        input_output_aliases=aliases,
        compiler_params=_params("arbitrary", "arbitrary"),
        name=name,
    )(sinks, rel_bias, *args, *a_args)
    return z, (wk, wv)


def _out_kernel(x_ref, za_ref, zb_ref, zc_ref, ma_ref, mb_ref, mc_ref, wa_ref, wb_ref, wc_ref,
                wo_ref, lng_ref, lnb_ref, *rest, alpha, emit_bf16):
    if emit_bf16:
        y_ref, yb_ref, merged_scr, r_scr = rest
    else:
        y_ref, merged_scr, r_scr = rest

    @pl.when(pl.program_id(0) == 0)
    def _():
        r_scr[...] = jnp.zeros_like(r_scr)

    lng, lnb = lng_ref[...], lnb_ref[...]
    for rc in range(r_scr.shape[0] // LN_ROWS):
        rs = slice(rc * LN_ROWS, (rc + 1) * LN_ROWS)
        r = r_scr[rs, :]
        mu = jnp.mean(r, axis=-1, keepdims=True)
        d = r - mu
        var = jnp.mean(d * d, axis=-1, keepdims=True)
        out = d * lax.rsqrt(var + LN_EPS) * lng + lnb
        y_ref[rs, :] = out
        if emit_bf16:
            yb_ref[rs, :] = out.astype(BF16)

    za, zb, zc = za_ref[...], zb_ref[...], zc_ref[...]
    for c in range(D_MODEL // OUT_STAGE_TN):
        cs = slice(c * OUT_STAGE_TN, (c + 1) * OUT_STAGE_TN)
        merged = (_sigmoid(ma_ref[:, cs].astype(F32))
                  * jnp.dot(za, wa_ref[:, cs], preferred_element_type=F32))
        merged += (_sigmoid(mb_ref[:, cs].astype(F32))
                   * jnp.dot(zb, wb_ref[:, cs], preferred_element_type=F32))
        merged += (_sigmoid(mc_ref[:, cs].astype(F32))
                   * jnp.dot(zc, wc_ref[:, cs], preferred_element_type=F32))
        merged_scr[:, cs] = merged.astype(BF16)
    r_scr[...] = alpha * x_ref[...] + jnp.dot(merged_scr[...], wo_ref[...],
                                              preferred_element_type=F32)


def _out_stage(x, za, zb, zc, h_gate, row0, wa, wb, wc, wo, lng, lnb, alpha, *, emit_bf16, name):
    m = x.shape[0]
    tm = OUT_STAGE_TM
    nsteps = m // tm
    cur = lambda i: jnp.minimum(i, nsteps - 1)
    rowblk = lambda w: pl.BlockSpec((tm, w), lambda i: (cur(i), 0))
    gate = lambda j: pl.BlockSpec((tm, D_MODEL), lambda i: (cur(i) + _block_offset(row0, tm), j))
    outblk = pl.BlockSpec((tm, D_MODEL), lambda i: (jnp.maximum(i - 1, 0), 0))
    const = lambda a: pl.BlockSpec(a.shape, lambda i: (0,) * a.ndim, pipeline_mode=pl.Buffered(1))
    out_shape = [jax.ShapeDtypeStruct((m, D_MODEL), F32)]
    if emit_bf16:
        out_shape.append(jax.ShapeDtypeStruct((m, D_MODEL), BF16))
    outs = pl.pallas_call(
        functools.partial(_out_kernel, alpha=alpha, emit_bf16=emit_bf16),
        out_shape=out_shape,
        grid=(nsteps + 1,),
        in_specs=[rowblk(D_MODEL), rowblk(W_A), rowblk(W_B), rowblk(W_C),
                  gate(0), gate(1), gate(2),
                  const(wa), const(wb), const(wc), const(wo), const(lng), const(lnb)],
        out_specs=[outblk] * len(out_shape),
        scratch_shapes=[pltpu.VMEM((tm, D_MODEL), BF16), pltpu.VMEM((tm, D_MODEL), F32)],
        compiler_params=_params("arbitrary"),
        name=name,
    )(x, za, zb, zc, h_gate, h_gate, h_gate, wa, wb, wc, wo, lng, lnb)
    return (outs[0], outs[1]) if emit_bf16 else (outs[0], None)


def _layer(groups, xs, xbs, layer, prevs, w, w_in, alpha):
    if xbs is None:
        h_att, xb_all = _in_proj(xs, w["w_att"], N_ATT, emit_xb=True, name=f"in_proj_att_{layer}")
        xbs = (xb_all,)
    else:
        h_att = _in_proj(xbs, w["w_att"], N_ATT, name=f"in_proj_att_{layer}")
    h_ab = _in_proj(xbs, w_in, N_AB, layer=layer, col0=0, name=f"in_proj_ab_{layer}")
    h_gate = _in_proj(xbs, w_in, N_GATE, layer=layer, col0=REF_GATE, out_dtype=BF16,
                      name=f"in_proj_gate_{layer}")
    ys, ybs, outs, row0 = [], [], [], 0
    for (batch, t, pos0, states, tag), x, prev in zip(groups, xs, prevs):
        st_ret, st_k, st_v, st_pool = states if states is not None else (None,) * 4
        pv_ret, pv_win, pv_pool = prev if prev is not None else (None,) * 3
        za, o_ret = _retention(h_ab, row0, batch, t, pos0, st_ret, layer, pv_ret,
                               name=f"retention_{tag}{layer}")
        zb, o_pool = _pool(h_ab, row0, batch, t, pos0, st_pool, layer, pv_pool, w["w_pool_map"],
                           w["pool_scale"], name=f"pool_{tag}{layer}")
        zc, o_win = _attention(h_att, row0, batch, t, pos0, st_k, st_v, layer, pv_win, w["sinks"],
                               w["rel_bias"], name=f"attention_{tag}{layer}")
        y, yb = _out_stage(x, za, zb, zc, h_gate, row0, w["w_ret_o"], w["w_pool_o"], w["w_att_o"],
                           w["w_out"], w["ln_g"], w["ln_b"], alpha, emit_bf16=layer + 1 < DEPTH,
                           name=f"out_stage_{tag}{layer}")
        ys.append(y)
        ybs.append(yb)
        outs.append((o_ret, o_win, o_pool))
        row0 += batch * t
    return ys, (tuple(ybs) if layer + 1 < DEPTH else None), outs


def _regroup_heads(w, axis):
    shp = w.shape
    w = w.reshape(shp[:axis] + (KV_C, G_C, HD_C) + shp[axis + 1:])
    w = jnp.swapaxes(w, axis, axis + 1)
    return w.reshape(shp)


def _regroup_matrix():
    new = np.arange(W_C)
    g, kv, d = new // KVW, (new // HD_C) % KV_C, new % HD_C
    p = np.zeros((W_C, W_C), np.float32)
    p[(kv * G_C + g) * HD_C + d, new] = 1.0
    return jnp.asarray(p, BF16)


def _w_att_kernel(q_ref, kv_ref, glo_ref, ghi_ref, perm_ref, o_ref):
    perm = perm_ref[...]
    regroup = lambda w: jnp.dot(w.astype(BF16), perm, preferred_element_type=F32).astype(BF16)
    o_ref[:, OFF_QC:OFF_QC + W_C] = regroup(q_ref[...])
    o_ref[:, OFF_GC:OFF_GC + W_C] = regroup(jnp.concatenate([glo_ref[...], ghi_ref[...]], axis=1))
    o_ref[:, OFF_KC:OFF_KC + 2 * KVW] = kv_ref[...].astype(BF16)


def _prep_w_att(w_in, l):
    half = W_C // 2
    assert REF_QC % W_C == 0 and REF_KC % half == 0 and REF_GC % half == 0
    assert REF_VC == REF_KC + KVW and OFF_VC == OFF_KC + KVW
    blk = lambda width, c0: pl.BlockSpec((None, W_PREP_ROWS, width), lambda i: (l, i, c0 // width))
    return pl.pallas_call(
        _w_att_kernel,
        out_shape=jax.ShapeDtypeStruct((D_MODEL, N_ATT), BF16),
        grid=(D_MODEL // W_PREP_ROWS,),
        in_specs=[blk(W_C, REF_QC), blk(half, REF_KC), blk(half, REF_GC), blk(half, REF_GC + half),
                  pl.BlockSpec((W_C, W_C), lambda i: (0, 0))],
        out_specs=pl.BlockSpec((W_PREP_ROWS, N_ATT), lambda i: (i, 0)),
        compiler_params=_params("parallel"),
        name=f"w_att_prep_{l}",
    )(w_in, w_in, w_in, w_in, _regroup_matrix())


def kernel(x_prompt, x_sample, state_ret, cache_win_k, cache_win_v, state_pool, w_in, w_ret_o,
           w_pool_map, pool_scale, w_pool_o, attn_sinks, w_att_o, w_out, ln_g, ln_b, rel_bias):
    alpha = (2.0 * DEPTH) ** 0.25
    bp, tp, _ = x_prompt.shape
    bs, ts, _ = x_sample.shape
    xp = x_prompt.reshape(bp * tp, D_MODEL)
    xs = x_sample.reshape(bs * ts, D_MODEL)
    ck = cache_win_k.reshape(DEPTH, bs, WINDOW, KVW)
    cv = cache_win_v.reshape(DEPTH, bs, WINDOW, KVW)
    groups = ((bp, tp, 0, None, "p"), (bs, ts, PAST_LEN, (state_ret, ck, cv, state_pool), "s"))
    xs, xbs, outs = (xp, xs), None, (None, None)
    for l in range(DEPTH):
        w = {
            "w_att": _prep_w_att(w_in, l),
            "w_ret_o": w_ret_o[l].astype(BF16),
            "w_pool_map": w_pool_map[l].astype(BF16),
            "pool_scale": pool_scale[l].reshape(1, W_B),
            "w_pool_o": w_pool_o[l].astype(BF16),
            "sinks": attn_sinks[l],
            "w_att_o": _regroup_heads(w_att_o[l], 0).astype(BF16),
            "w_out": w_out[l].astype(BF16),
            "ln_g": ln_g[l].reshape(1, D_MODEL),
            "ln_b": ln_b[l].reshape(1, D_MODEL),
            "rel_bias": rel_bias,
        }
        xs, xbs, outs = _layer(groups, xs, xbs, l, outs, w, w_in, alpha)
    ((ret_p,), (kp, vp), (pp,)), ((ret_s,), (ks, vs), (ps,)) = outs
    win = lambda a, b: a.reshape(DEPTH, b, WINDOW, KV_C, HD_C)
    return (xs[0].reshape(bp, tp, D_MODEL), xs[1].reshape(bs, ts, D_MODEL), ret_p, ret_s,
            win(kp, bp), win(ks, bs), win(vp, bp), win(vs, bs), pp, ps)
```

```python
import functools
import math

import jax
import jax.numpy as jnp
import numpy as np
from jax import lax
from jax.experimental import pallas as pl
from jax.experimental.pallas import tpu as pltpu

D_MODEL = 2048
DEPTH = 2
PAST_LEN = 8192
H_A, DK_A, DV_A = 4, 128, 256
W_A = H_A * DV_A
RET_CHUNK = 128
ROPE_BASE = 10000.0
N_POOL_GROUPS = 4
W_B = 1024
GW_B = W_B // N_POOL_GROUPS
POOL_WINDOWS = (2, 4, 8, 16)
POOL_BUF = 15
POOL_HALO = 16
H_C, KV_C, HD_C = 16, 4, 64
G_C = H_C // KV_C
W_C = H_C * HD_C
KVW = KV_C * HD_C
WINDOW = 128
ATT_BLOCK = 128
SOFTMAX_ROWS = 128
DENOM_W = 2 * HD_C
NUM_BUCKETS = 32
MAX_DISTANCE = 128
LN_EPS = 1e-5
RMS_EPS = 1e-6

REF_QC, REF_KC, REF_VC, REF_GC = 5120, 6144, 6400, 6656
N_AB = 5120
N_ATT = 2560
REF_GATE = 7680
N_GATE = 3 * D_MODEL
OFF_QA, OFF_KA, OFF_VA, OFF_GA, OFF_UB, OFF_GB = 0, 512, 1024, 2048, 3072, 4096
OFF_QC, OFF_GC, OFF_KC, OFF_VC = 0, 1024, 2048, 2304

F32 = jnp.float32
BF16 = jnp.bfloat16
VMEM_LIMIT = 56 * 1024 * 1024

IN_PROJ_TM = 1024
IN_PROJ_TM_F32_X = 512
IN_PROJ_TN = {N_AB: 1280, N_ATT: 2560, N_GATE: 1536}
OUT_STAGE_TM = 256
OUT_STAGE_TN = 512
LN_ROWS = 16
POOL_ROWS = 1024
SMALL_T_POOL_BATCH = 16
W_PREP_ROWS = 512
ATT_BLOCKS_PER_STEP = 2
SMALL_T_ATT_BATCH = 8
RET_ROWS = 512
SMALL_T_RET_BATCH = 4


def _sigmoid(x):
    return 1.0 / (1.0 + jnp.exp(-x))


def _silu(x):
    return x * _sigmoid(x)


def _params(*sem):
    return pltpu.CompilerParams(dimension_semantics=sem, vmem_limit_bytes=VMEM_LIMIT)


def _alias_args(prev_outs, n_in):
    if prev_outs is None:
        return [], [], {}
    specs = [pl.BlockSpec(memory_space=pl.ANY) for _ in prev_outs]
    return list(prev_outs), specs, {n_in + i: 1 + i for i in range(len(prev_outs))}


def _block_offset(row0, rows):
    assert row0 % rows == 0
    return row0 // rows


def _state_out_spec(block, index_map, layer, has_prev):
    if has_prev:
        return pl.BlockSpec((None,) + block, lambda *a: (layer,) + index_map(*a))
    return pl.BlockSpec((DEPTH,) + block, lambda *a: (0,) + index_map(*a))


def _state_store(ref, idx, value, layer, has_prev):
    if has_prev:
        ref[idx] = value
    else:
        for l in range(DEPTH):
            ref[(l,) + idx] = value if l == layer else jnp.zeros_like(value)


def _matmul_kernel(*refs, n_x, first_blocks, cast_w, emit_xb):
    refs = list(refs)
    x_refs, (w_ref, o_ref), rest = refs[:n_x], refs[n_x:n_x + 2], refs[n_x + 2:]
    xb_ref = rest.pop(0) if emit_xb else None
    if cast_w:
        wb_scr, = rest

        @pl.when(pl.program_id(1) == 0)
        def _():
            wb_scr[...] = w_ref[...].astype(BF16)
        w_ref = wb_scr

    def project(x_ref):
        x = x_ref[...].astype(BF16)
        if emit_xb:
            xb_ref[...] = x
        o_ref[...] = jnp.dot(x, w_ref[...], preferred_element_type=F32).astype(o_ref.dtype)

    if n_x == 1:
        project(x_refs[0])
    else:
        i = pl.program_id(1)
        pl.when(i < first_blocks)(lambda: project(x_refs[0]))
        pl.when(i >= first_blocks)(lambda: project(x_refs[1]))


def _in_proj(xs, w, n, *, layer=None, col0=0, emit_xb=False, out_dtype=F32, name):
    assert len(xs) in (1, 2)
    k = xs[0].shape[1]
    m = sum(x.shape[0] for x in xs)
    tm, tn = (IN_PROJ_TM_F32_X if xs[0].dtype == F32 else IN_PROJ_TM), IN_PROJ_TN[n]
    first = xs[0].shape[0] // tm
    assert all(x.shape[0] % tm == 0 for x in xs)
    cast_w = w.dtype == F32
    if cast_w:
        w_spec = pl.BlockSpec((None, k, tn), lambda j, i: (layer, 0, col0 // tn + j))
    else:
        w_spec = pl.BlockSpec((k, tn), lambda j, i: (0, j),
                              pipeline_mode=pl.Buffered(1) if n == tn else None)
    if len(xs) == 1:
        x_specs = [pl.BlockSpec((tm, k), lambda j, i: (i, 0))]
    else:
        last = xs[1].shape[0] // tm - 1
        x_specs = [pl.BlockSpec((tm, k), lambda j, i: (jnp.minimum(i, first - 1), 0)),
                   pl.BlockSpec((tm, k), lambda j, i: (jnp.clip(i - first, 0, last), 0),
                                pipeline_mode=None if last else pl.Buffered(1))]
    assert not emit_xb or n == tn
    out_shape = [jax.ShapeDtypeStruct((m, n), out_dtype)]
    out_specs = [pl.BlockSpec((tm, tn), lambda j, i: (i, j))]
    if emit_xb:
        out_shape.append(jax.ShapeDtypeStruct((m, k), BF16))
        out_specs.append(pl.BlockSpec((tm, k), lambda j, i: (i, 0)))
    outs = pl.pallas_call(
        functools.partial(_matmul_kernel, n_x=len(xs), first_blocks=first, cast_w=cast_w,
                          emit_xb=emit_xb),
        out_shape=out_shape,
        grid=(n // tn, m // tm),
        in_specs=x_specs + [w_spec],
        out_specs=out_specs,
        scratch_shapes=[pltpu.VMEM((k, tn), BF16)] if cast_w else [],
        compiler_params=_params("arbitrary", "arbitrary"),
        name=name,
    )(*xs, w)
    return outs if emit_xb else outs[0]


def _retention_kernel(*refs, chunk, nchunk, nbat, has_state, layer, has_prev):
    refs = list(refs)
    (q_ref, k_ref, v_ref, g_ref, cos_ref, sin_ref, dmask_ref, qdec_ref, kdec_ref, cdec_ref) = refs[:10]
    refs = refs[10:]
    s0_ref = refs.pop(0) if has_state else None
    if has_prev:
        refs.pop(0)
    z_ref, sout_ref, s_scr = refs
    c = pl.program_id(1)

    @pl.when(c == 0)
    def _():
        if has_state:
            s_scr[...] = s0_ref[...]
        else:
            s_scr[...] = jnp.zeros_like(s_scr)

    for bi in range(nbat):
        for hd in range(H_A):
            s = s_scr[bi, hd]
            for ci in range(nchunk):
                rows = slice((bi * nchunk + ci) * chunk, (bi * nchunk + ci + 1) * chunk)
                trow = slice(ci * chunk, (ci + 1) * chunk)
                cos = cos_ref[trow, :]
                sin = sin_ref[trow, :]
                q = q_ref[rows, hd * DK_A:(hd + 1) * DK_A].astype(F32)
                k = k_ref[rows, hd * DK_A:(hd + 1) * DK_A].astype(F32)
                v = v_ref[rows, hd * DV_A:(hd + 1) * DV_A].astype(BF16)
                qr = q * cos + pltpu.roll(q, DK_A // 2, 1) * sin
                kr = (k * cos + pltpu.roll(k, DK_A // 2, 1) * sin) * (DK_A ** -0.5)
                qb = qr.astype(BF16)
                sc = lax.dot_general(qb, kr.astype(BF16), (((1,), (1,)), ((), ())),
                                     preferred_element_type=F32) * dmask_ref[hd]
                o = (jnp.dot(sc.astype(BF16), v, preferred_element_type=F32)
                     + jnp.dot(qb, s.astype(BF16), preferred_element_type=F32) * qdec_ref[hd])
                kd = (kr * kdec_ref[hd]).astype(BF16)
                s = s * cdec_ref[hd] + lax.dot_general(
                    kd, v, (((0,), (0,)), ((), ())), preferred_element_type=F32)
                o = o * lax.rsqrt(jnp.mean(o * o, axis=-1, keepdims=True) + RMS_EPS)
                g = g_ref[rows, hd * DV_A:(hd + 1) * DV_A].astype(F32)
                z_ref[rows, hd * DV_A:(hd + 1) * DV_A] = (o * _silu(g)).astype(z_ref.dtype)
            s_scr[bi, hd] = s

    @pl.when(c == pl.num_programs(1) - 1)
    def _():
        _state_store(sout_ref, (Ellipsis,), s_scr[...], layer, has_prev)


def _retention_tables(pos0, t, chunk):
    half = DK_A // 2
    inv = ROPE_BASE ** (-jnp.arange(half, dtype=F32) / half)
    pos = pos0 + jnp.arange(t, dtype=F32)
    ang = pos[:, None] * inv[None, :]
    cos, sin = jnp.cos(ang), jnp.sin(ang)
    cos2 = jnp.concatenate([cos, cos], -1)
    sin2 = jnp.concatenate([-sin, sin], -1)
    lg = jnp.log1p(-jnp.exp2(-5.0 - jnp.arange(H_A, dtype=F32)))
    idx = jnp.arange(chunk, dtype=F32)
    diff = idx[:, None] - idx[None, :]
    dmask = jnp.where(diff >= 0, jnp.exp(lg[:, None, None] * jnp.maximum(diff, 0.0)), 0.0)
    qdec = jnp.exp(lg[:, None] * (idx[None, :] + 1.0))
    kdec = jnp.exp(lg[:, None] * (chunk - 1.0 - idx[None, :]))
    cdec = jnp.exp(lg * chunk)
    qdec = jnp.broadcast_to(qdec[:, :, None], (H_A, chunk, DV_A))
    kdec = jnp.broadcast_to(kdec[:, :, None], (H_A, chunk, DK_A))
    cdec = jnp.broadcast_to(cdec[:, None, None], (H_A, 1, DV_A))
    return cos2, sin2, dmask, qdec, kdec, cdec


def _retention(h_ab, row0, batch, t, pos0, state, layer, prev_outs, *, name):
    chunk = min(RET_CHUNK, t)
    if t > chunk:
        nbat, nchunk = 1, RET_ROWS // chunk
    else:
        nbat, nchunk = SMALL_T_RET_BATCH, 1
    rows = nbat * nchunk * chunk
    nsteps = t // (nchunk * chunk)
    has_state = state is not None
    tables = _retention_tables(pos0, t, chunk)
    row = lambda b, c: b * nsteps + c
    hrow = lambda b, c: row(b, c) + _block_offset(row0, rows)
    in_specs = [
        pl.BlockSpec((rows, H_A * DK_A), lambda b, c: (hrow(b, c), OFF_QA // (H_A * DK_A))),
        pl.BlockSpec((rows, H_A * DK_A), lambda b, c: (hrow(b, c), OFF_KA // (H_A * DK_A))),
        pl.BlockSpec((rows, W_A), lambda b, c: (hrow(b, c), OFF_VA // W_A)),
        pl.BlockSpec((rows, W_A), lambda b, c: (hrow(b, c), OFF_GA // W_A)),
        pl.BlockSpec((nchunk * chunk, DK_A), lambda b, c: (c, 0)),
        pl.BlockSpec((nchunk * chunk, DK_A), lambda b, c: (c, 0)),
        pl.BlockSpec((H_A, chunk, chunk), lambda b, c: (0, 0, 0)),
        pl.BlockSpec((H_A, chunk, DV_A), lambda b, c: (0, 0, 0)),
        pl.BlockSpec((H_A, chunk, DK_A), lambda b, c: (0, 0, 0)),
        pl.BlockSpec((H_A, 1, DV_A), lambda b, c: (0, 0, 0)),
    ]
    args = [h_ab, h_ab, h_ab, h_ab, *tables]
    if has_state:
        in_specs.append(pl.BlockSpec((None, nbat, H_A, DK_A, DV_A), lambda b, c: (layer, b, 0, 0, 0)))
        args.append(state)
    a_args, a_specs, aliases = _alias_args(prev_outs, len(args))
    z, s_out = pl.pallas_call(
        functools.partial(_retention_kernel, chunk=chunk, nchunk=nchunk, nbat=nbat,
                          has_state=has_state, layer=layer, has_prev=prev_outs is not None),
        out_shape=(jax.ShapeDtypeStruct((batch * t, W_A), BF16),
                   jax.ShapeDtypeStruct((DEPTH, batch, H_A, DK_A, DV_A), F32)),
        grid=(batch // nbat, nsteps),
        in_specs=in_specs + a_specs,
        out_specs=(pl.BlockSpec((rows, W_A), lambda b, c: (row(b, c), 0)),
                   _state_out_spec((nbat, H_A, DK_A, DV_A), lambda b, c: (b, 0, 0, 0), layer,
                                   prev_outs is not None)),
        scratch_shapes=[pltpu.VMEM((nbat, H_A, DK_A, DV_A), F32)],
        input_output_aliases=aliases,
        compiler_params=_params("arbitrary", "arbitrary"),
        name=name,
    )(*args, *a_args)
    return z, (s_out,)


def _pool_kernel(*refs, nb, tb, pos0, has_state, layer, has_prev):
    refs = list(refs)
    u_ref, g_ref, halo_ref, wmap_ref, scale_ref = refs[:5]
    refs = refs[5:]
    if has_prev:
        refs.pop(0)
    z_ref, pout_ref, ext_scr, p_scr = refs
    ti = pl.program_id(1)
    for bi in range(nb):
        rows = pl.ds(bi * tb, tb)
        if has_state:
            ext_scr[0:1, :] = jnp.zeros((1, W_B), F32)
            ext_scr[1:POOL_HALO, :] = halo_ref[bi]
        else:
            ext_scr[0:POOL_HALO, :] = jnp.where(ti == 0, 0.0, halo_ref[...].astype(F32))
        ext_scr[POOL_HALO:POOL_HALO + tb, :] = u_ref[rows, :].astype(F32)
        t_idx = pos0 + ti * tb + lax.broadcasted_iota(jnp.int32, (tb, GW_B), 0)
        for gi, w in enumerate(POOL_WINDOWS):
            cols = slice(gi * GW_B, (gi + 1) * GW_B)
            acc = ext_scr[:, cols]
            shift = 1
            while shift < w:
                acc = acc + pltpu.roll(acc, shift, 0)
                shift *= 2
            cnt = jnp.minimum(t_idx + 1, w).astype(F32)
            u = ext_scr[POOL_HALO:POOL_HALO + tb, cols]
            p_scr[rows, cols] = acc[POOL_HALO:POOL_HALO + tb] / cnt - u

        @pl.when(ti == pl.num_programs(1) - 1)
        def _():
            _state_store(pout_ref, (bi,), ext_scr[tb + 1:tb + POOL_HALO, :], layer, has_prev)

    for gi in range(N_POOL_GROUPS):
        cols = slice(gi * GW_B, (gi + 1) * GW_B)
        pm = jnp.dot(p_scr[:, cols].astype(BF16), wmap_ref[gi], preferred_element_type=F32)
        pm = pm * scale_ref[:, cols]
        z_ref[:, cols] = (pm * _silu(g_ref[:, cols].astype(F32))).astype(z_ref.dtype)


def _pool(h_ab, row0, batch, t, pos0, state, layer, prev_outs, wmap, scale, *, name):
    has_state = state is not None
    if t >= POOL_ROWS:
        nb, tb = 1, POOL_ROWS
    else:
        nb, tb = SMALL_T_POOL_BATCH, t
    nt = t // tb
    rows = nb * tb
    rblk = lambda b, i: b * nt + i
    hblk = lambda b, i: rblk(b, i) + _block_offset(row0, rows)
    in_specs = [pl.BlockSpec((rows, W_B), lambda b, i: (hblk(b, i), OFF_UB // W_B)),
                pl.BlockSpec((rows, W_B), lambda b, i: (hblk(b, i), OFF_GB // W_B))]
    args = [h_ab, h_ab]
    if has_state:
        assert nt == 1
        in_specs.append(pl.BlockSpec((None, nb, POOL_BUF, W_B), lambda b, i: (layer, b, 0, 0)))
        args.append(state)
    else:
        assert nb == 1
        per = tb // POOL_HALO
        in_specs.append(pl.BlockSpec(
            (POOL_HALO, W_B), lambda b, i: (jnp.maximum(hblk(b, i) * per - 1, 0), OFF_UB // W_B)))
        args.append(h_ab)
    in_specs += [pl.BlockSpec((N_POOL_GROUPS, GW_B, GW_B), lambda b, i: (0, 0, 0)),
                 pl.BlockSpec((1, W_B), lambda b, i: (0, 0))]
    args += [wmap, scale]
    a_args, a_specs, aliases = _alias_args(prev_outs, len(args))
    z, p_out = pl.pallas_call(
        functools.partial(_pool_kernel, nb=nb, tb=tb, pos0=pos0, has_state=has_state,
                          layer=layer, has_prev=prev_outs is not None),
        out_shape=(jax.ShapeDtypeStruct((batch * t, W_B), BF16),
                   jax.ShapeDtypeStruct((DEPTH, batch, POOL_BUF, W_B), F32)),
        grid=(batch // nb, nt),
        in_specs=in_specs + a_specs,
        out_specs=(pl.BlockSpec((rows, W_B), lambda b, i: (rblk(b, i), 0)),
                   _state_out_spec((nb, POOL_BUF, W_B), lambda b, i: (b, 0, 0), layer,
                                   prev_outs is not None)),
        scratch_shapes=[pltpu.VMEM((POOL_HALO + tb, W_B), F32),
                        pltpu.VMEM((rows, W_B), F32)],
        input_output_aliases=aliases,
        compiler_params=_params("arbitrary", "arbitrary"),
        name=name,
    )(*args, *a_args)
    return z, (p_out,)


def _t5_bucket_table(bq):
    span = WINDOW + bq
    dist = np.arange(bq)[:, None] + WINDOW - np.arange(span)[None, :]
    max_exact = NUM_BUCKETS // 2
    d = np.maximum(dist, 0).astype(np.float32)
    large = max_exact + (np.log(np.maximum(d, np.float32(1.0)) / np.float32(max_exact))
                         / np.float32(math.log(MAX_DISTANCE / max_exact))
                         * np.float32(NUM_BUCKETS - max_exact)).astype(np.int32)
    large = np.minimum(large, NUM_BUCKETS - 1)
    bucket = np.where(dist < max_exact, np.maximum(dist, 0), large)
    valid = (dist >= 0) & (dist < WINDOW)
    return np.where(valid, bucket, -1).astype(np.int32)


def _attention_kernel(*refs, bq, nbat, has_cache, mask_first, layer, has_prev):
    refs = list(refs)
    (sinks_ref, relb_ref, q_ref, kc_ref, vc_ref, kp_ref, vp_ref, g_ref, bucket_ref) = refs[:9]
    refs = refs[9:]
    if has_prev:
        del refs[:2]
    z_ref, wk_ref, wv_ref, bias_scr, qbd_scr, s_scr, p_scr, o_scr, l_scr = refs
    span = WINDOW + bq
    hrows = H_C * bq
    grows = G_C * bq
    chunk = min(hrows, SOFTMAX_ROWS)
    one_matmul = hrows <= SOFTMAX_ROWS
    fmin = jnp.finfo(F32).min
    n = pl.program_id(1)

    @pl.when((pl.program_id(0) == 0) & (n == 0))
    def _():
        bucket = bucket_ref[...]
        col = lax.broadcasted_iota(jnp.int32, (bq, span), 1)
        for h in range(H_C):
            acc = jnp.where(bucket < 0, fmin, 0.0)
            for b in range(NUM_BUCKETS):
                acc = jnp.where(bucket == b, relb_ref[b, h], acc)
            acc = jnp.where(col == 0, sinks_ref[h], acc)
            bias_scr[0, h * bq:(h + 1) * bq, :] = acc
            if mask_first:
                bias_scr[1, h * bq:(h + 1) * bq, :] = jnp.where((col < WINDOW) & (col > 0), fmin, acc)

    lane_kv = lax.broadcasted_iota(jnp.int32, (1, KVW), 1) // HD_C
    low_half = lax.broadcasted_iota(jnp.int32, (1, 2 * HD_C), 1) < HD_C
    row0 = lax.broadcasted_iota(jnp.int32, (WINDOW, KVW), 0) == 0
    ones = jnp.ones((span, DENOM_W), BF16)
    kc = vc = None
    for bi in range(nbat):
        rows = slice(bi * bq, (bi + 1) * bq)
        base = bi * hrows
        if has_cache:
            kp, vp = kp_ref[bi], vp_ref[bi]
        elif bi == 0:
            kp, vp = kp_ref[...], vp_ref[...]
        else:
            kp, vp = kc, vc
        bsel = jnp.where(n == 0, 1, 0) if (mask_first and bi == 0) else 0
        kc = kc_ref[rows, :]
        vc = vc_ref[rows, :]
        kx = jnp.concatenate([jnp.where(row0, 0.0, kp), kc], axis=0).astype(BF16)
        vx = jnp.concatenate([jnp.where(row0, 0.0, vp), vc], axis=0).astype(BF16)
        q = q_ref[rows, :] * (HD_C ** -0.5)
        for kk in range(KV_C):
            qbd_scr[base + kk * grows:base + (kk + 1) * grows, :] = jnp.concatenate(
                [jnp.where(lane_kv == kk, q[:, g * KVW:(g + 1) * KVW], 0.0) for g in range(G_C)],
                axis=0).astype(BF16)
        s_scr[base:base + hrows, :] = lax.dot_general(
            qbd_scr[base:base + hrows, :], kx, (((1,), (1,)), ((), ())), preferred_element_type=F32)
        for c in range(hrows // chunk):
            rs = slice(base + c * chunk, base + (c + 1) * chunk)
            s = s_scr[rs, :] + bias_scr[bsel, c * chunk:(c + 1) * chunk, :]
            m = jnp.max(s, axis=-1, keepdims=True)
            p_scr[rs, :] = jnp.exp(s - m).astype(BF16)
        if one_matmul:
            o3 = jnp.dot(p_scr[base:base + hrows, :], jnp.concatenate([vx, ones], axis=1),
                         preferred_element_type=F32)
            for kk in range(KV_C):
                o_scr[base + kk * grows:base + (kk + 1) * grows, :] = (
                    o3[kk * grows:(kk + 1) * grows, (kk // 2) * 2 * HD_C:(kk // 2 + 1) * 2 * HD_C])
            l_scr[base:base + hrows, :] = o3[:, KVW:KVW + DENOM_W]
        else:
            for kk in range(KV_C):
                rk = slice(base + kk * grows, base + (kk + 1) * grows)
                slab = vx[:, (kk // 2) * 2 * HD_C:(kk // 2 + 1) * 2 * HD_C]
                vk = jnp.where(low_half if kk % 2 == 0 else ~low_half, slab, jnp.zeros_like(slab))
                ol = jnp.dot(p_scr[rk, :], jnp.concatenate([vk, ones], axis=1),
                             preferred_element_type=F32)
                o_scr[rk, :] = ol[:, :2 * HD_C]
                l_scr[rk, :] = ol[:, 2 * HD_C:]
        for g in range(G_C):
            halves = []
            for half in range(2):
                ra, rb = (slice(base + kk * grows + g * bq, base + kk * grows + (g + 1) * bq)
                          for kk in (2 * half, 2 * half + 1))
                num = jnp.where(low_half, o_scr[ra, :], o_scr[rb, :])
                den = jnp.where(low_half, l_scr[ra, :], l_scr[rb, :])
                halves.append(num * (1.0 / den))
            og = jnp.concatenate(halves, axis=1)
            cols = slice(g * KVW, (g + 1) * KVW)
            z_ref[rows, cols] = (og * _silu(g_ref[rows, cols])).astype(z_ref.dtype)

        if has_cache or bi == nbat - 1:
            @pl.when(n == pl.num_programs(1) - 1)
            def _():
                wk = kc if bq == WINDOW else jnp.concatenate([kp[bq:, :], kc], axis=0)
                wv = vc if bq == WINDOW else jnp.concatenate([vp[bq:, :], vc], axis=0)
                _state_store(wk_ref, (bi if has_cache else 0,), wk, layer, has_prev)
                _state_store(wv_ref, (bi if has_cache else 0,), wv, layer, has_prev)


def _attention(h_att, row0, batch, t, pos0, cache_k, cache_v, layer, prev_outs, sinks, rel_bias, *,
               name):
    bq = min(ATT_BLOCK, t)
    nb = t // bq
    has_cache = cache_k is not None
    nbat = SMALL_T_ATT_BATCH if has_cache else ATT_BLOCKS_PER_STEP
    nsteps = 1 if has_cache else nb // nbat
    wb = nbat if has_cache else 1
    span = WINDOW + bq
    assert pos0 == 0 or pos0 >= WINDOW
    mask_first = pos0 == 0
    rows = nbat * bq
    row = lambda b, n, *_: b * nsteps + n
    hrow = lambda b, n, *_: row(b, n) + _block_offset(row0, rows)
    in_specs = [
        pl.BlockSpec((rows, W_C), lambda b, n, *_: (hrow(b, n), OFF_QC // W_C)),
        pl.BlockSpec((rows, KVW), lambda b, n, *_: (hrow(b, n), OFF_KC // KVW)),
        pl.BlockSpec((rows, KVW), lambda b, n, *_: (hrow(b, n), OFF_VC // KVW)),
    ]
    args = [h_att, h_att, h_att]
    if has_cache:
        assert nb == 1
        in_specs += [pl.BlockSpec((None, nbat, WINDOW, KVW), lambda b, n, *_: (layer, b, 0, 0)),
                     pl.BlockSpec((None, nbat, WINDOW, KVW), lambda b, n, *_: (layer, b, 0, 0))]
        args += [cache_k, cache_v]
    else:
        assert bq == WINDOW and nb % nbat == 0
        prev = lambda b, n, *_: jnp.maximum(hrow(b, n) * nbat - 1, 0)
        in_specs += [pl.BlockSpec((WINDOW, KVW), lambda b, n, *_: (prev(b, n), OFF_KC // KVW)),
                     pl.BlockSpec((WINDOW, KVW), lambda b, n, *_: (prev(b, n), OFF_VC // KVW))]
        args += [h_att, h_att]
    in_specs += [
        pl.BlockSpec((rows, W_C), lambda b, n, *_: (hrow(b, n), OFF_GC // W_C)),
        pl.BlockSpec((bq, span), lambda b, n, *_: (0, 0)),
    ]
    args += [h_att, jnp.asarray(_t5_bucket_table(bq))]
    a_args, a_specs, aliases = _alias_args(prev_outs, 2 + len(args))
    srows = nbat * H_C * bq
    z, wk, wv = pl.pallas_call(
        functools.partial(_attention_kernel, bq=bq, nbat=nbat, has_cache=has_cache,
                          mask_first=mask_first, layer=layer, has_prev=prev_outs is not None),
        out_shape=(jax.ShapeDtypeStruct((batch * t, W_C), BF16),
                   jax.ShapeDtypeStruct((DEPTH, batch, WINDOW, KVW), F32),
                   jax.ShapeDtypeStruct((DEPTH, batch, WINDOW, KVW), F32)),
        grid_spec=pltpu.PrefetchScalarGridSpec(
            num_scalar_prefetch=2,
            grid=(batch // wb, nsteps),
            in_specs=in_specs + a_specs,
            out_specs=(pl.BlockSpec((rows, W_C), lambda b, n, *_: (row(b, n), 0)),
                       _state_out_spec((wb, WINDOW, KVW), lambda b, n, *_: (b, 0, 0), layer,
                                       prev_outs is not None),
                       _state_out_spec((wb, WINDOW, KVW), lambda b, n, *_: (b, 0, 0), layer,
                                       prev_outs is not None)),
            scratch_shapes=[pltpu.VMEM((2 if mask_first else 1, H_C * bq, span), F32),
                            pltpu.VMEM((srows, KVW), BF16),
                            pltpu.VMEM((srows, span), F32),
                            pltpu.VMEM((srows, span), BF16),
                            pltpu.VMEM((srows, 2 * HD_C), F32),
                            pltpu.VMEM((srows, DENOM_W), F32)]),
        input_output_aliases=aliases,
        compiler_params=_params("arbitrary", "arbitrary"),
        name=name,
    )(sinks, rel_bias, *args, *a_args)
    return z, (wk, wv)


def _out_kernel(x_ref, za_ref, zb_ref, zc_ref, ma_ref, mb_ref, mc_ref, wa_ref, wb_ref, wc_ref,
                wo_ref, lng_ref, lnb_ref, *rest, alpha, emit_bf16):
    if emit_bf16:
        y_ref, yb_ref, merged_scr, r_scr = rest
    else:
        y_ref, merged_scr, r_scr = rest

    @pl.when(pl.program_id(0) == 0)
    def _():
        r_scr[...] = jnp.zeros_like(r_scr)

    lng, lnb = lng_ref[...], lnb_ref[...]
    for rc in range(r_scr.shape[0] // LN_ROWS):
        rs = slice(rc * LN_ROWS, (rc + 1) * LN_ROWS)
        r = r_scr[rs, :]
        mu = jnp.mean(r, axis=-1, keepdims=True)
        d = r - mu
        var = jnp.mean(d * d, axis=-1, keepdims=True)
        out = d * lax.rsqrt(var + LN_EPS) * lng + lnb
        y_ref[rs, :] = out
        if emit_bf16:
            yb_ref[rs, :] = out.astype(BF16)

    za, zb, zc = za_ref[...], zb_ref[...], zc_ref[...]
    for c in range(D_MODEL // OUT_STAGE_TN):
        cs = slice(c * OUT_STAGE_TN, (c + 1) * OUT_STAGE_TN)
        merged = (_sigmoid(ma_ref[:, cs].astype(F32))
                  * jnp.dot(za, wa_ref[:, cs], preferred_element_type=F32))
        merged += (_sigmoid(mb_ref[:, cs].astype(F32))
                   * jnp.dot(zb, wb_ref[:, cs], preferred_element_type=F32))
        merged += (_sigmoid(mc_ref[:, cs].astype(F32))
                   * jnp.dot(zc, wc_ref[:, cs], preferred_element_type=F32))
        merged_scr[:, cs] = merged.astype(BF16)
    r_scr[...] = alpha * x_ref[...] + jnp.dot(merged_scr[...], wo_ref[...],
                                              preferred_element_type=F32)


def _out_stage(x, za, zb, zc, h_gate, row0, wa, wb, wc, wo, lng, lnb, alpha, *, emit_bf16, name):
    m = x.shape[0]
    tm = OUT_STAGE_TM
    nsteps = m // tm
    cur = lambda i: jnp.minimum(i, nsteps - 1)
    rowblk = lambda w: pl.BlockSpec((tm, w), lambda i: (cur(i), 0))
    gate = lambda j: pl.BlockSpec((tm, D_MODEL), lambda i: (cur(i) + _block_offset(row0, tm), j))
    outblk = pl.BlockSpec((tm, D_MODEL), lambda i: (jnp.maximum(i - 1, 0), 0))
    const = lambda a: pl.BlockSpec(a.shape, lambda i: (0,) * a.ndim, pipeline_mode=pl.Buffered(1))
    out_shape = [jax.ShapeDtypeStruct((m, D_MODEL), F32)]
    if emit_bf16:
        out_shape.append(jax.ShapeDtypeStruct((m, D_MODEL), BF16))
    outs = pl.pallas_call(
        functools.partial(_out_kernel, alpha=alpha, emit_bf16=emit_bf16),
        out_shape=out_shape,
        grid=(nsteps + 1,),
        in_specs=[rowblk(D_MODEL), rowblk(W_A), rowblk(W_B), rowblk(W_C),
                  gate(0), gate(1), gate(2),
                  const(wa), const(wb), const(wc), const(wo), const(lng), const(lnb)],
        out_specs=[outblk] * len(out_shape),
        scratch_shapes=[pltpu.VMEM((tm, D_MODEL), BF16), pltpu.VMEM((tm, D_MODEL), F32)],
        compiler_params=_params("arbitrary"),
        name=name,
    )(x, za, zb, zc, h_gate, h_gate, h_gate, wa, wb, wc, wo, lng, lnb)
    return (outs[0], outs[1]) if emit_bf16 else (outs[0], None)


def _layer(groups, xs, xbs, layer, prevs, w, w_in, alpha):
    if xbs is None:
        h_att, xb_all = _in_proj(xs, w["w_att"], N_ATT, emit_xb=True, name=f"in_proj_att_{layer}")
        xbs = (xb_all,)
    else:
        h_att = _in_proj(xbs, w["w_att"], N_ATT, name=f"in_proj_att_{layer}")
    h_ab = _in_proj(xbs, w_in, N_AB, layer=layer, col0=0, out_dtype=BF16,
                    name=f"in_proj_ab_{layer}")
    h_gate = _in_proj(xbs, w_in, N_GATE, layer=layer, col0=REF_GATE, out_dtype=BF16,
                      name=f"in_proj_gate_{layer}")
    ys, ybs, outs, row0 = [], [], [], 0
    for (batch, t, pos0, states, tag), x, prev in zip(groups, xs, prevs):
        st_ret, st_k, st_v, st_pool = states if states is not None else (None,) * 4
        pv_ret, pv_win, pv_pool = prev if prev is not None else (None,) * 3
        za, o_ret = _retention(h_ab, row0, batch, t, pos0, st_ret, layer, pv_ret,
                               name=f"retention_{tag}{layer}")
        zb, o_pool = _pool(h_ab, row0, batch, t, pos0, st_pool, layer, pv_pool, w["w_pool_map"],
                           w["pool_scale"], name=f"pool_{tag}{layer}")
        zc, o_win = _attention(h_att, row0, batch, t, pos0, st_k, st_v, layer, pv_win, w["sinks"],
                               w["rel_bias"], name=f"attention_{tag}{layer}")
        y, yb = _out_stage(x, za, zb, zc, h_gate, row0, w["w_ret_o"], w["w_pool_o"], w["w_att_o"],
                           w["w_out"], w["ln_g"], w["ln_b"], alpha, emit_bf16=layer + 1 < DEPTH,
                           name=f"out_stage_{tag}{layer}")
        ys.append(y)
        ybs.append(yb)
        outs.append((o_ret, o_win, o_pool))
        row0 += batch * t
    return ys, (tuple(ybs) if layer + 1 < DEPTH else None), outs


def _regroup_heads(w, axis):
    shp = w.shape
    w = w.reshape(shp[:axis] + (KV_C, G_C, HD_C) + shp[axis + 1:])
    w = jnp.swapaxes(w, axis, axis + 1)
    return w.reshape(shp)


def _regroup_matrix():
    new = np.arange(W_C)
    g, kv, d = new // KVW, (new // HD_C) % KV_C, new % HD_C
    p = np.zeros((W_C, W_C), np.float32)
    p[(kv * G_C + g) * HD_C + d, new] = 1.0
    return jnp.asarray(p, BF16)


def _w_att_kernel(q_ref, kv_ref, glo_ref, ghi_ref, perm_ref, o_ref):
    perm = perm_ref[...]
    regroup = lambda w: jnp.dot(w.astype(BF16), perm, preferred_element_type=F32).astype(BF16)
    o_ref[:, OFF_QC:OFF_QC + W_C] = regroup(q_ref[...])
    o_ref[:, OFF_GC:OFF_GC + W_C] = regroup(jnp.concatenate([glo_ref[...], ghi_ref[...]], axis=1))
    o_ref[:, OFF_KC:OFF_KC + 2 * KVW] = kv_ref[...].astype(BF16)


def _prep_w_att(w_in, l):
    half = W_C // 2
    assert REF_QC % W_C == 0 and REF_KC % half == 0 and REF_GC % half == 0
    assert REF_VC == REF_KC + KVW and OFF_VC == OFF_KC + KVW
    blk = lambda width, c0: pl.BlockSpec((None, W_PREP_ROWS, width), lambda i: (l, i, c0 // width))
    return pl.pallas_call(
        _w_att_kernel,
        out_shape=jax.ShapeDtypeStruct((D_MODEL, N_ATT), BF16),
        grid=(D_MODEL // W_PREP_ROWS,),
        in_specs=[blk(W_C, REF_QC), blk(half, REF_KC), blk(half, REF_GC), blk(half, REF_GC + half),
                  pl.BlockSpec((W_C, W_C), lambda i: (0, 0))],
        out_specs=pl.BlockSpec((W_PREP_ROWS, N_ATT), lambda i: (i, 0)),
        compiler_params=_params("parallel"),
        name=f"w_att_prep_{l}",
    )(w_in, w_in, w_in, w_in, _regroup_matrix())


def kernel(x_prompt, x_sample, state_ret, cache_win_k, cache_win_v, state_pool, w_in, w_ret_o,
           w_pool_map, pool_scale, w_pool_o, attn_sinks, w_att_o, w_out, ln_g, ln_b, rel_bias):
    alpha = (2.0 * DEPTH) ** 0.25
    bp, tp, _ = x_prompt.shape
    bs, ts, _ = x_sample.shape
    xp = x_prompt.reshape(bp * tp, D_MODEL)
    xs = x_sample.reshape(bs * ts, D_MODEL)
    ck = cache_win_k.reshape(DEPTH, bs, WINDOW, KVW)
    cv = cache_win_v.reshape(DEPTH, bs, WINDOW, KVW)
    groups = ((bp, tp, 0, None, "p"), (bs, ts, PAST_LEN, (state_ret, ck, cv, state_pool), "s"))
    xs, xbs, outs = (xp, xs), None, (None, None)
    for l in range(DEPTH):
        w = {
            "w_att": _prep_w_att(w_in, l),
            "w_ret_o": w_ret_o[l].astype(BF16),
            "w_pool_map": w_pool_map[l].astype(BF16),
            "pool_scale": pool_scale[l].reshape(1, W_B),
            "w_pool_o": w_pool_o[l].astype(BF16),
            "sinks": attn_sinks[l],
            "w_att_o": _regroup_heads(w_att_o[l], 0).astype(BF16),
            "w_out": w_out[l].astype(BF16),
            "ln_g": ln_g[l].reshape(1, D_MODEL),
            "ln_b": ln_b[l].reshape(1, D_MODEL),
            "rel_bias": rel_bias,
        }
        xs, xbs, outs = _layer(groups, xs, xbs, l, outs, w, w_in, alpha)
    ((ret_p,), (kp, vp), (pp,)), ((ret_s,), (ks, vs), (ps,)) = outs
    win = lambda a, b: a.reshape(DEPTH, b, WINDOW, KV_C, HD_C)
    return (xs[0].reshape(bp, tp, D_MODEL), xs[1].reshape(bs, ts, D_MODEL), ret_p, ret_s,
            win(kp, bp), win(ks, bs), win(vp, bp), win(vs, bs), pp, ps)
```

```python
import functools
import math

import jax
import jax.numpy as jnp
import numpy as np
from jax import lax
from jax.experimental import pallas as pl
from jax.experimental.pallas import tpu as pltpu

D_MODEL = 2048
DEPTH = 2
PAST_LEN = 8192
H_A, DK_A, DV_A = 4, 128, 256
W_A = H_A * DV_A
RET_CHUNK = 128
ROPE_BASE = 10000.0
N_POOL_GROUPS = 4
W_B = 1024
GW_B = W_B // N_POOL_GROUPS
POOL_WINDOWS = (2, 4, 8, 16)
POOL_BUF = 15
POOL_HALO = 16
H_C, KV_C, HD_C = 16, 4, 64
G_C = H_C // KV_C
W_C = H_C * HD_C
KVW = KV_C * HD_C
WINDOW = 128
ATT_BLOCK = 128
SOFTMAX_ROWS = 128
DENOM_W = 2 * HD_C
NUM_BUCKETS = 32
MAX_DISTANCE = 128
LN_EPS = 1e-5
RMS_EPS = 1e-6

REF_QC, REF_KC, REF_VC, REF_GC = 5120, 6144, 6400, 6656
N_AB = 5120
N_ATT = 2560
REF_GATE = 7680
N_GATE = 3 * D_MODEL
OFF_QA, OFF_KA, OFF_VA, OFF_GA, OFF_UB, OFF_GB = 0, 512, 1024, 2048, 3072, 4096
OFF_QC, OFF_GC, OFF_KC, OFF_VC = 0, 1024, 2048, 2304

F32 = jnp.float32
BF16 = jnp.bfloat16
VMEM_LIMIT = 56 * 1024 * 1024

IN_PROJ_TM = 1024
IN_PROJ_TM_F32_X = 512
IN_PROJ_TN = {N_AB: 1280, N_ATT: 2560, N_GATE: 1536}
OUT_STAGE_TM = 256
OUT_STAGE_TN = 512
LN_ROWS = 16
POOL_ROWS = 1024
SMALL_T_POOL_BATCH = 16
W_PREP_ROWS = 512
ATT_BLOCKS_PER_STEP = 2
SMALL_T_ATT_BATCH = 8
RET_ROWS = 512
SMALL_T_RET_BATCH = 4


def _sigmoid(x):
    return 1.0 / (1.0 + jnp.exp(-x))


def _silu(x):
    return x * _sigmoid(x)


def _params(*sem):
    return pltpu.CompilerParams(dimension_semantics=sem, vmem_limit_bytes=VMEM_LIMIT)


def _alias_args(prev_outs, n_in):
    if prev_outs is None:
        return [], [], {}
    specs = [pl.BlockSpec(memory_space=pl.ANY) for _ in prev_outs]
    return list(prev_outs), specs, {n_in + i: 1 + i for i in range(len(prev_outs))}


def _block_offset(row0, rows):
    assert row0 % rows == 0
    return row0 // rows


def _state_out_spec(block, index_map, layer, has_prev):
    if has_prev:
        return pl.BlockSpec((None,) + block, lambda *a: (layer,) + index_map(*a))
    return pl.BlockSpec((DEPTH,) + block, lambda *a: (0,) + index_map(*a))


def _state_store(ref, idx, value, layer, has_prev):
    if has_prev:
        ref[idx] = value
    else:
        for l in range(DEPTH):
            ref[(l,) + idx] = value if l == layer else jnp.zeros_like(value)


def _matmul_kernel(*refs, n_x, first_blocks, cast_w, emit_xb):
    refs = list(refs)
    x_refs, (w_ref, o_ref), rest = refs[:n_x], refs[n_x:n_x + 2], refs[n_x + 2:]
    xb_ref = rest.pop(0) if emit_xb else None
    if cast_w:
        wb_scr, = rest

        @pl.when(pl.program_id(1) == 0)
        def _():
            wb_scr[...] = w_ref[...].astype(BF16)
        w_ref = wb_scr

    x = x_refs[0][...]
    if n_x == 2:
        x = jnp.where(pl.program_id(1) < first_blocks, x, x_refs[1][...])
    x = x.astype(BF16)
    if emit_xb:
        xb_ref[...] = x
    o_ref[...] = jnp.dot(x, w_ref[...], preferred_element_type=F32).astype(o_ref.dtype)


def _in_proj(xs, w, n, *, layer=None, col0=0, emit_xb=False, out_dtype=F32, name):
    assert len(xs) in (1, 2)
    k = xs[0].shape[1]
    m = sum(x.shape[0] for x in xs)
    tm, tn = (IN_PROJ_TM_F32_X if xs[0].dtype == F32 else IN_PROJ_TM), IN_PROJ_TN[n]
    first = xs[0].shape[0] // tm
    assert all(x.shape[0] % tm == 0 for x in xs)
    cast_w = w.dtype == F32
    if cast_w:
        w_spec = pl.BlockSpec((None, k, tn), lambda j, i: (layer, 0, col0 // tn + j))
    else:
        w_spec = pl.BlockSpec((k, tn), lambda j, i: (0, j),
                              pipeline_mode=pl.Buffered(1) if n == tn else None)
    if len(xs) == 1:
        x_specs = [pl.BlockSpec((tm, k), lambda j, i: (i, 0))]
    else:
        last = xs[1].shape[0] // tm - 1
        x_specs = [pl.BlockSpec((tm, k), lambda j, i: (jnp.minimum(i, first - 1), 0)),
                   pl.BlockSpec((tm, k), lambda j, i: (jnp.clip(i - first, 0, last), 0),
                                pipeline_mode=None if last else pl.Buffered(1))]
    assert not emit_xb or n == tn
    out_shape = [jax.ShapeDtypeStruct((m, n), out_dtype)]
    out_specs = [pl.BlockSpec((tm, tn), lambda j, i: (i, j))]
    if emit_xb:
        out_shape.append(jax.ShapeDtypeStruct((m, k), BF16))
        out_specs.append(pl.BlockSpec((tm, k), lambda j, i: (i, 0)))
    outs = pl.pallas_call(
        functools.partial(_matmul_kernel, n_x=len(xs), first_blocks=first, cast_w=cast_w,
                          emit_xb=emit_xb),
        out_shape=out_shape,
        grid=(n // tn, m // tm),
        in_specs=x_specs + [w_spec],
        out_specs=out_specs,
        scratch_shapes=[pltpu.VMEM((k, tn), BF16)] if cast_w else [],
        compiler_params=_params("arbitrary", "arbitrary"),
        name=name,
    )(*xs, w)
    return outs if emit_xb else outs[0]


def _retention_kernel(*refs, chunk, nchunk, nbat, has_state, layer, has_prev):
    refs = list(refs)
    (q_ref, k_ref, v_ref, g_ref, cos_ref, sin_ref, dmask_ref, qdec_ref, kdec_ref, cdec_ref) = refs[:10]
    refs = refs[10:]
    s0_ref = refs.pop(0) if has_state else None
    if has_prev:
        refs.pop(0)
    z_ref, sout_ref, s_scr = refs
    c = pl.program_id(1)

    @pl.when(c == 0)
    def _():
        if has_state:
            s_scr[...] = s0_ref[...]
        else:
            s_scr[...] = jnp.zeros_like(s_scr)

    for bi in range(nbat):
        for hd in range(H_A):
            s = s_scr[bi, hd]
            for ci in range(nchunk):
                rows = slice((bi * nchunk + ci) * chunk, (bi * nchunk + ci + 1) * chunk)
                trow = slice(ci * chunk, (ci + 1) * chunk)
                cos = cos_ref[trow, :]
                sin = sin_ref[trow, :]
                q = q_ref[rows, hd * DK_A:(hd + 1) * DK_A]
                k = k_ref[rows, hd * DK_A:(hd + 1) * DK_A]
                v = v_ref[rows, hd * DV_A:(hd + 1) * DV_A].astype(BF16)
                qr = q * cos + pltpu.roll(q, DK_A // 2, 1) * sin
                kr = (k * cos + pltpu.roll(k, DK_A // 2, 1) * sin) * (DK_A ** -0.5)
                qb = qr.astype(BF16)
                sc = lax.dot_general(qb, kr.astype(BF16), (((1,), (1,)), ((), ())),
                                     preferred_element_type=F32) * dmask_ref[hd]
                o = (jnp.dot(sc.astype(BF16), v, preferred_element_type=F32)
                     + jnp.dot(qb, s.astype(BF16), preferred_element_type=F32) * qdec_ref[hd])
                kd = (kr * kdec_ref[hd]).astype(BF16)
                s = s * cdec_ref[hd] + lax.dot_general(
                    kd, v, (((0,), (0,)), ((), ())), preferred_element_type=F32)
                o = o * lax.rsqrt(jnp.mean(o * o, axis=-1, keepdims=True) + RMS_EPS)
                g = g_ref[rows, hd * DV_A:(hd + 1) * DV_A]
                z_ref[rows, hd * DV_A:(hd + 1) * DV_A] = (o * _silu(g)).astype(z_ref.dtype)
            s_scr[bi, hd] = s

    @pl.when(c == pl.num_programs(1) - 1)
    def _():
        _state_store(sout_ref, (Ellipsis,), s_scr[...], layer, has_prev)


def _retention_tables(pos0, t, chunk):
    half = DK_A // 2
    inv = ROPE_BASE ** (-jnp.arange(half, dtype=F32) / half)
    pos = pos0 + jnp.arange(t, dtype=F32)
    ang = pos[:, None] * inv[None, :]
    cos, sin = jnp.cos(ang), jnp.sin(ang)
    cos2 = jnp.concatenate([cos, cos], -1)
    sin2 = jnp.concatenate([-sin, sin], -1)
    lg = jnp.log1p(-jnp.exp2(-5.0 - jnp.arange(H_A, dtype=F32)))
    idx = jnp.arange(chunk, dtype=F32)
    diff = idx[:, None] - idx[None, :]
    dmask = jnp.where(diff >= 0, jnp.exp(lg[:, None, None] * jnp.maximum(diff, 0.0)), 0.0)
    qdec = jnp.exp(lg[:, None] * (idx[None, :] + 1.0))
    kdec = jnp.exp(lg[:, None] * (chunk - 1.0 - idx[None, :]))
    cdec = jnp.exp(lg * chunk)
    qdec = jnp.broadcast_to(qdec[:, :, None], (H_A, chunk, DV_A))
    kdec = jnp.broadcast_to(kdec[:, :, None], (H_A, chunk, DK_A))
    cdec = jnp.broadcast_to(cdec[:, None, None], (H_A, 1, DV_A))
    return cos2, sin2, dmask, qdec, kdec, cdec


def _retention(h_ab, row0, batch, t, pos0, state, layer, prev_outs, *, name):
    chunk = min(RET_CHUNK, t)
    if t > chunk:
        nbat, nchunk = 1, RET_ROWS // chunk
    else:
        nbat, nchunk = SMALL_T_RET_BATCH, 1
    rows = nbat * nchunk * chunk
    nsteps = t // (nchunk * chunk)
    has_state = state is not None
    tables = _retention_tables(pos0, t, chunk)
    row = lambda b, c: b * nsteps + c
    hrow = lambda b, c: row(b, c) + _block_offset(row0, rows)
    in_specs = [
        pl.BlockSpec((rows, H_A * DK_A), lambda b, c: (hrow(b, c), OFF_QA // (H_A * DK_A))),
        pl.BlockSpec((rows, H_A * DK_A), lambda b, c: (hrow(b, c), OFF_KA // (H_A * DK_A))),
        pl.BlockSpec((rows, W_A), lambda b, c: (hrow(b, c), OFF_VA // W_A)),
        pl.BlockSpec((rows, W_A), lambda b, c: (hrow(b, c), OFF_GA // W_A)),
        pl.BlockSpec((nchunk * chunk, DK_A), lambda b, c: (c, 0)),
        pl.BlockSpec((nchunk * chunk, DK_A), lambda b, c: (c, 0)),
        pl.BlockSpec((H_A, chunk, chunk), lambda b, c: (0, 0, 0)),
        pl.BlockSpec((H_A, chunk, DV_A), lambda b, c: (0, 0, 0)),
        pl.BlockSpec((H_A, chunk, DK_A), lambda b, c: (0, 0, 0)),
        pl.BlockSpec((H_A, 1, DV_A), lambda b, c: (0, 0, 0)),
    ]
    args = [h_ab, h_ab, h_ab, h_ab, *tables]
    if has_state:
        in_specs.append(pl.BlockSpec((None, nbat, H_A, DK_A, DV_A), lambda b, c: (layer, b, 0, 0, 0)))
        args.append(state)
    a_args, a_specs, aliases = _alias_args(prev_outs, len(args))
    z, s_out = pl.pallas_call(
        functools.partial(_retention_kernel, chunk=chunk, nchunk=nchunk, nbat=nbat,
                          has_state=has_state, layer=layer, has_prev=prev_outs is not None),
        out_shape=(jax.ShapeDtypeStruct((batch * t, W_A), BF16),
                   jax.ShapeDtypeStruct((DEPTH, batch, H_A, DK_A, DV_A), F32)),
        grid=(batch // nbat, nsteps),
        in_specs=in_specs + a_specs,
        out_specs=(pl.BlockSpec((rows, W_A), lambda b, c: (row(b, c), 0)),
                   _state_out_spec((nbat, H_A, DK_A, DV_A), lambda b, c: (b, 0, 0, 0), layer,
                                   prev_outs is not None)),
        scratch_shapes=[pltpu.VMEM((nbat, H_A, DK_A, DV_A), F32)],
        input_output_aliases=aliases,
        compiler_params=_params("arbitrary", "arbitrary"),
        name=name,
    )(*args, *a_args)
    return z, (s_out,)


def _pool_kernel(*refs, nb, tb, pos0, has_state, layer, has_prev):
    refs = list(refs)
    u_ref, g_ref, halo_ref, wmap_ref, scale_ref = refs[:5]
    refs = refs[5:]
    if has_prev:
        refs.pop(0)
    z_ref, pout_ref, ext_scr, p_scr = refs
    ti = pl.program_id(1)
    for bi in range(nb):
        rows = pl.ds(bi * tb, tb)
        if has_state:
            ext_scr[0:1, :] = jnp.zeros((1, W_B), F32)
            ext_scr[1:POOL_HALO, :] = halo_ref[bi]
        else:
            ext_scr[0:POOL_HALO, :] = jnp.where(ti == 0, 0.0, halo_ref[...])
        ext_scr[POOL_HALO:POOL_HALO + tb, :] = u_ref[rows, :]
        t_idx = pos0 + ti * tb + lax.broadcasted_iota(jnp.int32, (tb, GW_B), 0)
        for gi, w in enumerate(POOL_WINDOWS):
            cols = slice(gi * GW_B, (gi + 1) * GW_B)
            acc = ext_scr[:, cols]
            shift = 1
            while shift < w:
                acc = acc + pltpu.roll(acc, shift, 0)
                shift *= 2
            cnt = jnp.minimum(t_idx + 1, w).astype(F32)
            u = ext_scr[POOL_HALO:POOL_HALO + tb, cols]
            p_scr[rows, cols] = acc[POOL_HALO:POOL_HALO + tb] / cnt - u

        @pl.when(ti == pl.num_programs(1) - 1)
        def _():
            _state_store(pout_ref, (bi,), ext_scr[tb + 1:tb + POOL_HALO, :], layer, has_prev)

    for gi in range(N_POOL_GROUPS):
        cols = slice(gi * GW_B, (gi + 1) * GW_B)
        pm = jnp.dot(p_scr[:, cols].astype(BF16), wmap_ref[gi], preferred_element_type=F32)
        pm = pm * scale_ref[:, cols]
        z_ref[:, cols] = (pm * _silu(g_ref[:, cols])).astype(z_ref.dtype)


def _pool(h_ab, row0, batch, t, pos0, state, layer, prev_outs, wmap, scale, *, name):
    has_state = state is not None
    if t >= POOL_ROWS:
        nb, tb = 1, POOL_ROWS
    else:
        nb, tb = SMALL_T_POOL_BATCH, t
    nt = t // tb
    rows = nb * tb
    rblk = lambda b, i: b * nt + i
    hblk = lambda b, i: rblk(b, i) + _block_offset(row0, rows)
    in_specs = [pl.BlockSpec((rows, W_B), lambda b, i: (hblk(b, i), OFF_UB // W_B)),
                pl.BlockSpec((rows, W_B), lambda b, i: (hblk(b, i), OFF_GB // W_B))]
    args = [h_ab, h_ab]
    if has_state:
        assert nt == 1
        in_specs.append(pl.BlockSpec((None, nb, POOL_BUF, W_B), lambda b, i: (layer, b, 0, 0)))
        args.append(state)
    else:
        assert nb == 1
        per = tb // POOL_HALO
        in_specs.append(pl.BlockSpec(
            (POOL_HALO, W_B), lambda b, i: (jnp.maximum(hblk(b, i) * per - 1, 0), OFF_UB // W_B)))
        args.append(h_ab)
    in_specs += [pl.BlockSpec((N_POOL_GROUPS, GW_B, GW_B), lambda b, i: (0, 0, 0)),
                 pl.BlockSpec((1, W_B), lambda b, i: (0, 0))]
    args += [wmap, scale]
    a_args, a_specs, aliases = _alias_args(prev_outs, len(args))
    z, p_out = pl.pallas_call(
        functools.partial(_pool_kernel, nb=nb, tb=tb, pos0=pos0, has_state=has_state,
                          layer=layer, has_prev=prev_outs is not None),
        out_shape=(jax.ShapeDtypeStruct((batch * t, W_B), BF16),
                   jax.ShapeDtypeStruct((DEPTH, batch, POOL_BUF, W_B), F32)),
        grid=(batch // nb, nt),
        in_specs=in_specs + a_specs,
        out_specs=(pl.BlockSpec((rows, W_B), lambda b, i: (rblk(b, i), 0)),
                   _state_out_spec((nb, POOL_BUF, W_B), lambda b, i: (b, 0, 0), layer,
                                   prev_outs is not None)),
        scratch_shapes=[pltpu.VMEM((POOL_HALO + tb, W_B), F32),
                        pltpu.VMEM((rows, W_B), F32)],
        input_output_aliases=aliases,
        compiler_params=_params("arbitrary", "arbitrary"),
        name=name,
    )(*args, *a_args)
    return z, (p_out,)


def _t5_bucket_table(bq):
    span = WINDOW + bq
    dist = np.arange(bq)[:, None] + WINDOW - np.arange(span)[None, :]
    max_exact = NUM_BUCKETS // 2
    d = np.maximum(dist, 0).astype(np.float32)
    large = max_exact + (np.log(np.maximum(d, np.float32(1.0)) / np.float32(max_exact))
                         / np.float32(math.log(MAX_DISTANCE / max_exact))
                         * np.float32(NUM_BUCKETS - max_exact)).astype(np.int32)
    large = np.minimum(large, NUM_BUCKETS - 1)
    bucket = np.where(dist < max_exact, np.maximum(dist, 0), large)
    valid = (dist >= 0) & (dist < WINDOW)
    return np.where(valid, bucket, -1).astype(np.int32)


def _attention_kernel(*refs, bq, nbat, has_cache, mask_first, layer, has_prev):
    refs = list(refs)
    (sinks_ref, relb_ref, q_ref, kc_ref, vc_ref, kp_ref, vp_ref, g_ref, bucket_ref) = refs[:9]
    refs = refs[9:]
    if has_prev:
        del refs[:2]
    z_ref, wk_ref, wv_ref, bias_scr, qbd_scr, s_scr, p_scr, o_scr, l_scr = refs
    span = WINDOW + bq
    hrows = H_C * bq
    grows = G_C * bq
    chunk = min(hrows, SOFTMAX_ROWS)
    one_matmul = hrows <= SOFTMAX_ROWS
    fmin = jnp.finfo(F32).min
    n = pl.program_id(1)

    @pl.when((pl.program_id(0) == 0) & (n == 0))
    def _():
        bucket = bucket_ref[...]
        col = lax.broadcasted_iota(jnp.int32, (bq, span), 1)
        for h in range(H_C):
            acc = jnp.where(bucket < 0, fmin, 0.0)
            for b in range(NUM_BUCKETS):
                acc = jnp.where(bucket == b, relb_ref[b, h], acc)
            acc = jnp.where(col == 0, sinks_ref[h], acc)
            bias_scr[0, h * bq:(h + 1) * bq, :] = acc
            if mask_first:
                bias_scr[1, h * bq:(h + 1) * bq, :] = jnp.where((col < WINDOW) & (col > 0), fmin, acc)

    lane_kv = lax.broadcasted_iota(jnp.int32, (1, KVW), 1) // HD_C
    low_half = lax.broadcasted_iota(jnp.int32, (1, 2 * HD_C), 1) < HD_C
    row0 = lax.broadcasted_iota(jnp.int32, (WINDOW, KVW), 0) == 0
    col0_t = lax.broadcasted_iota(jnp.int32, (KVW, WINDOW), 1) == 0
    ones = jnp.ones((span, DENOM_W), BF16)
    kc = vc = None
    for bi in range(nbat):
        rows = slice(bi * bq, (bi + 1) * bq)
        base = bi * hrows
        if has_cache:
            kp, vp = kp_ref[bi], vp_ref[bi]
        elif bi == 0:
            kp, vp = kp_ref[...], vp_ref[...]
        else:
            kp, vp = kc, vc
        bsel = jnp.where(n == 0, 1, 0) if (mask_first and bi == 0) else 0
        kc = kc_ref[rows, :]
        vc = vc_ref[rows, :]
        q = q_ref[rows, :] * (HD_C ** -0.5)
        for kk in range(KV_C):
            qbd_scr[base + kk * grows:base + (kk + 1) * grows, :] = jnp.concatenate(
                [jnp.where(lane_kv == kk, q[:, g * KVW:(g + 1) * KVW], 0.0) for g in range(G_C)],
                axis=0).astype(BF16)
        qbd = qbd_scr[base:base + hrows, :]
        nt_dims = (((1,), (1,)), ((), ()))
        if has_cache:
            kpt = jnp.where(col0_t, 0.0, kp).astype(BF16)
            s_scr[base:base + hrows, :] = jnp.concatenate(
                [jnp.dot(qbd, kpt, preferred_element_type=F32),
                 lax.dot_general(qbd, kc.astype(BF16), nt_dims, preferred_element_type=F32)], axis=1)
        else:
            kx = jnp.concatenate([jnp.where(row0, 0.0, kp), kc], axis=0).astype(BF16)
            vx = jnp.concatenate([jnp.where(row0, 0.0, vp), vc], axis=0).astype(BF16)
            s_scr[base:base + hrows, :] = lax.dot_general(qbd, kx, nt_dims,
                                                          preferred_element_type=F32)
        for c in range(hrows // chunk):
            rs = slice(base + c * chunk, base + (c + 1) * chunk)
            s = s_scr[rs, :] + bias_scr[bsel, c * chunk:(c + 1) * chunk, :]
            m = jnp.max(s, axis=-1, keepdims=True)
            p_scr[rs, :] = jnp.exp(s - m).astype(BF16)
        if has_cache:
            assert one_matmul
            p = p_scr[base:base + hrows, :]
            vpt1 = jnp.concatenate([jnp.where(col0_t, 0.0, vp).astype(BF16),
                                    jnp.ones((DENOM_W, WINDOW), BF16)], axis=0)
            vc1 = jnp.concatenate([vc.astype(BF16), jnp.ones((bq, DENOM_W), BF16)], axis=1)
            o3 = (lax.dot_general(p[:, :WINDOW], vpt1, nt_dims, preferred_element_type=F32)
                  + jnp.dot(p[:, WINDOW:], vc1, preferred_element_type=F32))
        elif one_matmul:
            o3 = jnp.dot(p_scr[base:base + hrows, :], jnp.concatenate([vx, ones], axis=1),
                         preferred_element_type=F32)
        if one_matmul:
            for kk in range(KV_C):
                o_scr[base + kk * grows:base + (kk + 1) * grows, :] = (
                    o3[kk * grows:(kk + 1) * grows, (kk // 2) * 2 * HD_C:(kk // 2 + 1) * 2 * HD_C])
            l_scr[base:base + hrows, :] = o3[:, KVW:KVW + DENOM_W]
        else:
            for kk in range(KV_C):
                rk = slice(base + kk * grows, base + (kk + 1) * grows)
                slab = vx[:, (kk // 2) * 2 * HD_C:(kk // 2 + 1) * 2 * HD_C]
                vk = jnp.where(low_half if kk % 2 == 0 else ~low_half, slab, jnp.zeros_like(slab))
                ol = jnp.dot(p_scr[rk, :], jnp.concatenate([vk, ones], axis=1),
                             preferred_element_type=F32)
                o_scr[rk, :] = ol[:, :2 * HD_C]
                l_scr[rk, :] = ol[:, 2 * HD_C:]
        for g in range(G_C):
            halves = []
            for half in range(2):
                ra, rb = (slice(base + kk * grows + g * bq, base + kk * grows + (g + 1) * bq)
                          for kk in (2 * half, 2 * half + 1))
                num = jnp.where(low_half, o_scr[ra, :], o_scr[rb, :])
                den = jnp.where(low_half, l_scr[ra, :], l_scr[rb, :])
                halves.append(num * (1.0 / den))
            og = jnp.concatenate(halves, axis=1)
            cols = slice(g * KVW, (g + 1) * KVW)
            z_ref[rows, cols] = (og * _silu(g_ref[rows, cols])).astype(z_ref.dtype)

        if has_cache:
            keep = lax.broadcasted_iota(jnp.int32, (KVW, WINDOW), 1) < WINDOW - bq
            pad = jnp.zeros((WINDOW - bq, KVW), F32)
            for ref, old, new in ((wk_ref, kp, kc), (wv_ref, vp, vc)):
                new_t = jnp.concatenate([pad, new], axis=0).T
                _state_store(ref, (bi,), jnp.where(keep, pltpu.roll(old, WINDOW - bq, 1), new_t),
                             layer, has_prev)
        elif bi == nbat - 1:
            @pl.when(n == pl.num_programs(1) - 1)
            def _():
                _state_store(wk_ref, (0,), kc, layer, has_prev)
                _state_store(wv_ref, (0,), vc, layer, has_prev)


def _attention(h_att, row0, batch, t, pos0, cache_k, cache_v, layer, prev_outs, sinks, rel_bias, *,
               name):
    bq = min(ATT_BLOCK, t)
    nb = t // bq
    has_cache = cache_k is not None
    nbat = SMALL_T_ATT_BATCH if has_cache else ATT_BLOCKS_PER_STEP
    nsteps = 1 if has_cache else nb // nbat
    wb = nbat if has_cache else 1
    win_shape = (KVW, WINDOW) if has_cache else (WINDOW, KVW)
    span = WINDOW + bq
    assert pos0 == 0 or pos0 >= WINDOW
    mask_first = pos0 == 0
    rows = nbat * bq
    row = lambda b, n, *_: b * nsteps + n
    hrow = lambda b, n, *_: row(b, n) + _block_offset(row0, rows)
    in_specs = [
        pl.BlockSpec((rows, W_C), lambda b, n, *_: (hrow(b, n), OFF_QC // W_C)),
        pl.BlockSpec((rows, KVW), lambda b, n, *_: (hrow(b, n), OFF_KC // KVW)),
        pl.BlockSpec((rows, KVW), lambda b, n, *_: (hrow(b, n), OFF_VC // KVW)),
    ]
    args = [h_att, h_att, h_att]
    if has_cache:
        assert nb == 1
        in_specs += [pl.BlockSpec((None, nbat) + win_shape, lambda b, n, *_: (layer, b, 0, 0)),
                     pl.BlockSpec((None, nbat) + win_shape, lambda b, n, *_: (layer, b, 0, 0))]
        args += [cache_k, cache_v]
    else:
        assert bq == WINDOW and nb % nbat == 0
        prev = lambda b, n, *_: jnp.maximum(hrow(b, n) * nbat - 1, 0)
        in_specs += [pl.BlockSpec((WINDOW, KVW), lambda b, n, *_: (prev(b, n), OFF_KC // KVW)),
                     pl.BlockSpec((WINDOW, KVW), lambda b, n, *_: (prev(b, n), OFF_VC // KVW))]
        args += [h_att, h_att]
    in_specs += [
        pl.BlockSpec((rows, W_C), lambda b, n, *_: (hrow(b, n), OFF_GC // W_C)),
        pl.BlockSpec((bq, span), lambda b, n, *_: (0, 0)),
    ]
    args += [h_att, jnp.asarray(_t5_bucket_table(bq))]
    a_args, a_specs, aliases = _alias_args(prev_outs, 2 + len(args))
    srows = nbat * H_C * bq
    z, wk, wv = pl.pallas_call(
        functools.partial(_attention_kernel, bq=bq, nbat=nbat, has_cache=has_cache,
                          mask_first=mask_first, layer=layer, has_prev=prev_outs is not None),
        out_shape=(jax.ShapeDtypeStruct((batch * t, W_C), BF16),
                   jax.ShapeDtypeStruct((DEPTH, batch) + win_shape, F32),
                   jax.ShapeDtypeStruct((DEPTH, batch) + win_shape, F32)),
        grid_spec=pltpu.PrefetchScalarGridSpec(
            num_scalar_prefetch=2,
            grid=(batch // wb, nsteps),
            in_specs=in_specs + a_specs,
            out_specs=(pl.BlockSpec((rows, W_C), lambda b, n, *_: (row(b, n), 0)),
                       _state_out_spec((wb,) + win_shape, lambda b, n, *_: (b, 0, 0), layer,
                                       prev_outs is not None),
                       _state_out_spec((wb,) + win_shape, lambda b, n, *_: (b, 0, 0), layer,
                                       prev_outs is not None)),
            scratch_shapes=[pltpu.VMEM((2 if mask_first else 1, H_C * bq, span), F32),
                            pltpu.VMEM((srows, KVW), BF16),
                            pltpu.VMEM((srows, span), F32),
                            pltpu.VMEM((srows, span), BF16),
                            pltpu.VMEM((srows, 2 * HD_C), F32),
                            pltpu.VMEM((srows, DENOM_W), F32)]),
        input_output_aliases=aliases,
        compiler_params=_params("arbitrary", "arbitrary"),
        name=name,
    )(sinks, rel_bias, *args, *a_args)
    return z, (wk, wv)


def _out_kernel(x_ref, za_ref, zb_ref, zc_ref, ma_ref, mb_ref, mc_ref, wa_ref, wb_ref, wc_ref,
                wo_ref, lng_ref, lnb_ref, *rest, alpha, emit_bf16):
    if emit_bf16:
        y_ref, yb_ref, merged_scr, r_scr = rest
    else:
        y_ref, merged_scr, r_scr = rest

    @pl.when(pl.program_id(0) == 0)
    def _():
        r_scr[...] = jnp.zeros_like(r_scr)

    lng, lnb = lng_ref[...], lnb_ref[...]
    for rc in range(r_scr.shape[0] // LN_ROWS):
        rs = slice(rc * LN_ROWS, (rc + 1) * LN_ROWS)
        r = r_scr[rs, :]
        mu = jnp.mean(r, axis=-1, keepdims=True)
        d = r - mu
        var = jnp.mean(d * d, axis=-1, keepdims=True)
        out = d * lax.rsqrt(var + LN_EPS) * lng + lnb
        y_ref[rs, :] = out
        if emit_bf16:
            yb_ref[rs, :] = out.astype(BF16)

    za, zb, zc = za_ref[...], zb_ref[...], zc_ref[...]
    for c in range(D_MODEL // OUT_STAGE_TN):
        cs = slice(c * OUT_STAGE_TN, (c + 1) * OUT_STAGE_TN)
        merged = (_sigmoid(ma_ref[:, cs].astype(F32))
                  * jnp.dot(za, wa_ref[:, cs], preferred_element_type=F32))
        merged += (_sigmoid(mb_ref[:, cs].astype(F32))
                   * jnp.dot(zb, wb_ref[:, cs], preferred_element_type=F32))
        merged += (_sigmoid(mc_ref[:, cs].astype(F32))
                   * jnp.dot(zc, wc_ref[:, cs], preferred_element_type=F32))
        merged_scr[:, cs] = merged.astype(BF16)
    r_scr[...] = alpha * x_ref[...] + jnp.dot(merged_scr[...], wo_ref[...],
                                              preferred_element_type=F32)


def _out_stage(x, za, zb, zc, h_gate, row0, wa, wb, wc, wo, lng, lnb, alpha, *, emit_bf16, name):
    m = x.shape[0]
    tm = OUT_STAGE_TM
    nsteps = m // tm
    cur = lambda i: jnp.minimum(i, nsteps - 1)
    rowblk = lambda w: pl.BlockSpec((tm, w), lambda i: (cur(i), 0))
    gate = lambda j: pl.BlockSpec((tm, D_MODEL), lambda i: (cur(i) + _block_offset(row0, tm), j))
    outblk = pl.BlockSpec((tm, D_MODEL), lambda i: (jnp.maximum(i - 1, 0), 0))
    const = lambda a: pl.BlockSpec(a.shape, lambda i: (0,) * a.ndim, pipeline_mode=pl.Buffered(1))
    out_shape = [jax.ShapeDtypeStruct((m, D_MODEL), F32)]
    if emit_bf16:
        out_shape.append(jax.ShapeDtypeStruct((m, D_MODEL), BF16))
    outs = pl.pallas_call(
        functools.partial(_out_kernel, alpha=alpha, emit_bf16=emit_bf16),
        out_shape=out_shape,
        grid=(nsteps + 1,),
        in_specs=[rowblk(D_MODEL), rowblk(W_A), rowblk(W_B), rowblk(W_C),
                  gate(0), gate(1), gate(2),
                  const(wa), const(wb), const(wc), const(wo), const(lng), const(lnb)],
        out_specs=[outblk] * len(out_shape),
        scratch_shapes=[pltpu.VMEM((tm, D_MODEL), BF16), pltpu.VMEM((tm, D_MODEL), F32)],
        compiler_params=_params("arbitrary"),
        name=name,
    )(x, za, zb, zc, h_gate, h_gate, h_gate, wa, wb, wc, wo, lng, lnb)
    return (outs[0], outs[1]) if emit_bf16 else (outs[0], None)


def _layer(groups, xs, xbs, layer, prevs, w, w_in, alpha):
    if xbs is None:
        h_att, xb_all = _in_proj(xs, w["w_att"], N_ATT, emit_xb=True, name=f"in_proj_att_{layer}")
        xbs = (xb_all,)
    else:
        h_att = _in_proj(xbs, w["w_att"], N_ATT, name=f"in_proj_att_{layer}")
    h_ab = _in_proj(xbs, w_in, N_AB, layer=layer, col0=0, name=f"in_proj_ab_{layer}")
    h_gate = _in_proj(xbs, w_in, N_GATE, layer=layer, col0=REF_GATE, out_dtype=BF16,
                      name=f"in_proj_gate_{layer}")
    ys, ybs, outs, row0 = [], [], [], 0
    for (batch, t, pos0, states, tag), x, prev in zip(groups, xs, prevs):
        st_ret, st_k, st_v, st_pool = states if states is not None else (None,) * 4
        pv_ret, pv_win, pv_pool = prev if prev is not None else (None,) * 3
        za, o_ret = _retention(h_ab, row0, batch, t, pos0, st_ret, layer, pv_ret,
                               name=f"retention_{tag}{layer}")
        zb, o_pool = _pool(h_ab, row0, batch, t, pos0, st_pool, layer, pv_pool, w["w_pool_map"],
                           w["pool_scale"], name=f"pool_{tag}{layer}")
        zc, o_win = _attention(h_att, row0, batch, t, pos0, st_k, st_v, layer, pv_win, w["sinks"],
                               w["rel_bias"], name=f"attention_{tag}{layer}")
        y, yb = _out_stage(x, za, zb, zc, h_gate, row0, w["w_ret_o"], w["w_pool_o"], w["w_att_o"],
                           w["w_out"], w["ln_g"], w["ln_b"], alpha, emit_bf16=layer + 1 < DEPTH,
                           name=f"out_stage_{tag}{layer}")
        ys.append(y)
        ybs.append(yb)
        outs.append((o_ret, o_win, o_pool))
        row0 += batch * t
    return ys, (tuple(ybs) if layer + 1 < DEPTH else None), outs


def _regroup_heads(w, axis):
    shp = w.shape
    w = w.reshape(shp[:axis] + (KV_C, G_C, HD_C) + shp[axis + 1:])
    w = jnp.swapaxes(w, axis, axis + 1)
    return w.reshape(shp)


def _regroup_matrix():
    new = np.arange(W_C)
    g, kv, d = new // KVW, (new // HD_C) % KV_C, new % HD_C
    p = np.zeros((W_C, W_C), np.float32)
    p[(kv * G_C + g) * HD_C + d, new] = 1.0
    return jnp.asarray(p, BF16)


def _w_att_kernel(q_ref, kv_ref, glo_ref, ghi_ref, perm_ref, o_ref):
    perm = perm_ref[...]
    regroup = lambda w: jnp.dot(w.astype(BF16), perm, preferred_element_type=F32).astype(BF16)
    o_ref[:, OFF_QC:OFF_QC + W_C] = regroup(q_ref[...])
    o_ref[:, OFF_GC:OFF_GC + W_C] = regroup(jnp.concatenate([glo_ref[...], ghi_ref[...]], axis=1))
    o_ref[:, OFF_KC:OFF_KC + 2 * KVW] = kv_ref[...].astype(BF16)


def _prep_w_att(w_in, l):
    half = W_C // 2
    assert REF_QC % W_C == 0 and REF_KC % half == 0 and REF_GC % half == 0
    assert REF_VC == REF_KC + KVW and OFF_VC == OFF_KC + KVW
    blk = lambda width, c0: pl.BlockSpec((None, W_PREP_ROWS, width), lambda i: (l, i, c0 // width))
    return pl.pallas_call(
        _w_att_kernel,
        out_shape=jax.ShapeDtypeStruct((D_MODEL, N_ATT), BF16),
        grid=(D_MODEL // W_PREP_ROWS,),
        in_specs=[blk(W_C, REF_QC), blk(half, REF_KC), blk(half, REF_GC), blk(half, REF_GC + half),
                  pl.BlockSpec((W_C, W_C), lambda i: (0, 0))],
        out_specs=pl.BlockSpec((W_PREP_ROWS, N_ATT), lambda i: (i, 0)),
        compiler_params=_params("parallel"),
        name=f"w_att_prep_{l}",
    )(w_in, w_in, w_in, w_in, _regroup_matrix())


def kernel(x_prompt, x_sample, state_ret, cache_win_k, cache_win_v, state_pool, w_in, w_ret_o,
           w_pool_map, pool_scale, w_pool_o, attn_sinks, w_att_o, w_out, ln_g, ln_b, rel_bias):
    alpha = (2.0 * DEPTH) ** 0.25
    bp, tp, _ = x_prompt.shape
    bs, ts, _ = x_sample.shape
    xp = x_prompt.reshape(bp * tp, D_MODEL)
    xs = x_sample.reshape(bs * ts, D_MODEL)
    to_t = lambda c: c.transpose(0, 1, 3, 4, 2).reshape(DEPTH, bs, KVW, WINDOW)
    from_t = lambda c: c.reshape(DEPTH, bs, KV_C, HD_C, WINDOW).transpose(0, 1, 4, 2, 3)
    ck, cv = to_t(cache_win_k), to_t(cache_win_v)
    groups = ((bp, tp, 0, None, "p"), (bs, ts, PAST_LEN, (state_ret, ck, cv, state_pool), "s"))
    xs, xbs, outs = (xp, xs), None, (None, None)
    for l in range(DEPTH):
        w = {
            "w_att": _prep_w_att(w_in, l),
            "w_ret_o": w_ret_o[l].astype(BF16),
            "w_pool_map": w_pool_map[l].astype(BF16),
            "pool_scale": pool_scale[l].reshape(1, W_B),
            "w_pool_o": w_pool_o[l].astype(BF16),
            "sinks": attn_sinks[l],
            "w_att_o": _regroup_heads(w_att_o[l], 0).astype(BF16),
            "w_out": w_out[l].astype(BF16),
            "ln_g": ln_g[l].reshape(1, D_MODEL),
            "ln_b": ln_b[l].reshape(1, D_MODEL),
            "rel_bias": rel_bias,
        }
        xs, xbs, outs = _layer(groups, xs, xbs, l, outs, w, w_in, alpha)
    ((ret_p,), (kp, vp), (pp,)), ((ret_s,), (ks, vs), (ps,)) = outs
    win = lambda a, b: a.reshape(DEPTH, b, WINDOW, KV_C, HD_C)
    return (xs[0].reshape(bp, tp, D_MODEL), xs[1].reshape(bs, ts, D_MODEL), ret_p, ret_s,
            win(kp, bp), from_t(ks), win(vp, bp), from_t(vs), pp, ps)
```

```python
import functools
import math

import jax
import jax.numpy as jnp
import numpy as np
from jax import lax
from jax.experimental import pallas as pl
from jax.experimental.pallas import tpu as pltpu

D_MODEL = 2048
DEPTH = 2
PAST_LEN = 8192
H_A, DK_A, DV_A = 4, 128, 256
W_A = H_A * DV_A
RET_CHUNK = 128
ROPE_BASE = 10000.0
N_POOL_GROUPS = 4
W_B = 1024
GW_B = W_B // N_POOL_GROUPS
POOL_WINDOWS = (2, 4, 8, 16)
POOL_BUF = 15
POOL_HALO = 16
H_C, KV_C, HD_C = 16, 4, 64
G_C = H_C // KV_C
W_C = H_C * HD_C
KVW = KV_C * HD_C
WINDOW = 128
ATT_BLOCK = 128
SOFTMAX_ROWS = 128
DENOM_W = 2 * HD_C
NUM_BUCKETS = 32
MAX_DISTANCE = 128
LN_EPS = 1e-5
RMS_EPS = 1e-6

REF_QC, REF_KC, REF_VC, REF_GC = 5120, 6144, 6400, 6656
N_AB = 5120
N_ATT = 2560
REF_GATE = 7680
N_GATE = 3 * D_MODEL
OFF_QA, OFF_KA, OFF_VA, OFF_GA, OFF_UB, OFF_GB = 0, 512, 1024, 2048, 3072, 4096
OFF_QC, OFF_GC, OFF_KC, OFF_VC = 0, 1024, 2048, 2304

F32 = jnp.float32
BF16 = jnp.bfloat16
VMEM_LIMIT = 56 * 1024 * 1024

IN_PROJ_TM = 1024
IN_PROJ_TM_F32_X = 512
IN_PROJ_TN = {N_AB: 1280, N_ATT: 2560, N_GATE: 1536}
OUT_STAGE_TM = 256
OUT_STAGE_TN = 512
LN_ROWS = 16
POOL_ROWS = 1024
SMALL_T_POOL_BATCH = 16
W_PREP_ROWS = 512
ATT_BLOCKS_PER_STEP = 4
SMALL_T_ATT_BATCH = 8
RET_ROWS = 512
SMALL_T_RET_BATCH = 4


LOG2E = 1.0 / math.log(2.0)


def _sigmoid(x):
    return 1.0 / (1.0 + jnp.exp2(x * (-LOG2E)))


def _silu(x):
    return x * _sigmoid(x)


def _params(*sem):
    return pltpu.CompilerParams(dimension_semantics=sem, vmem_limit_bytes=VMEM_LIMIT)


def _alias_args(prev_outs, n_in):
    if prev_outs is None:
        return [], [], {}
    specs = [pl.BlockSpec(memory_space=pl.ANY) for _ in prev_outs]
    return list(prev_outs), specs, {n_in + i: 1 + i for i in range(len(prev_outs))}


def _block_offset(row0, rows):
    assert row0 % rows == 0
    return row0 // rows


def _state_out_spec(block, index_map, layer, has_prev):
    if has_prev:
        return pl.BlockSpec((None,) + block, lambda *a: (layer,) + index_map(*a))
    return pl.BlockSpec((DEPTH,) + block, lambda *a: (0,) + index_map(*a))


def _state_store(ref, idx, value, layer, has_prev):
    if has_prev:
        ref[idx] = value
    else:
        for l in range(DEPTH):
            ref[(l,) + idx] = value if l == layer else jnp.zeros_like(value)


def _matmul_kernel(*refs, n_x, first_blocks, cast_w, emit_xb):
    refs = list(refs)
    x_refs, (w_ref, o_ref), rest = refs[:n_x], refs[n_x:n_x + 2], refs[n_x + 2:]
    xb_ref = rest.pop(0) if emit_xb else None
    if cast_w:
        wb_scr, = rest

        @pl.when(pl.program_id(1) == 0)
        def _():
            wb_scr[...] = w_ref[...].astype(BF16)
        w_ref = wb_scr

    x = x_refs[0][...]
    if n_x == 2:
        x = jnp.where(pl.program_id(1) < first_blocks, x, x_refs[1][...])
    x = x.astype(BF16)
    if emit_xb:
        xb_ref[...] = x
    o_ref[...] = jnp.dot(x, w_ref[...], preferred_element_type=F32).astype(o_ref.dtype)


def _in_proj(xs, w, n, *, layer=None, col0=0, emit_xb=False, out_dtype=F32, name):
    assert len(xs) in (1, 2)
    k = xs[0].shape[1]
    m = sum(x.shape[0] for x in xs)
    tm, tn = (IN_PROJ_TM_F32_X if xs[0].dtype == F32 else IN_PROJ_TM), IN_PROJ_TN[n]
    first = xs[0].shape[0] // tm
    assert all(x.shape[0] % tm == 0 for x in xs)
    cast_w = w.dtype == F32
    if cast_w:
        w_spec = pl.BlockSpec((None, k, tn), lambda j, i: (layer, 0, col0 // tn + j))
    else:
        w_spec = pl.BlockSpec((k, tn), lambda j, i: (0, j),
                              pipeline_mode=pl.Buffered(1) if n == tn else None)
    if len(xs) == 1:
        x_specs = [pl.BlockSpec((tm, k), lambda j, i: (i, 0))]
    else:
        last = xs[1].shape[0] // tm - 1
        x_specs = [pl.BlockSpec((tm, k), lambda j, i: (jnp.minimum(i, first - 1), 0)),
                   pl.BlockSpec((tm, k), lambda j, i: (jnp.clip(i - first, 0, last), 0),
                                pipeline_mode=None if last else pl.Buffered(1))]
    assert not emit_xb or n == tn
    out_shape = [jax.ShapeDtypeStruct((m, n), out_dtype)]
    out_specs = [pl.BlockSpec((tm, tn), lambda j, i: (i, j))]
    if emit_xb:
        out_shape.append(jax.ShapeDtypeStruct((m, k), BF16))
        out_specs.append(pl.BlockSpec((tm, k), lambda j, i: (i, 0)))
    outs = pl.pallas_call(
        functools.partial(_matmul_kernel, n_x=len(xs), first_blocks=first, cast_w=cast_w,
                          emit_xb=emit_xb),
        out_shape=out_shape,
        grid=(n // tn, m // tm),
        in_specs=x_specs + [w_spec],
        out_specs=out_specs,
        scratch_shapes=[pltpu.VMEM((k, tn), BF16)] if cast_w else [],
        compiler_params=_params("arbitrary", "arbitrary"),
        name=name,
    )(*xs, w)
    return outs if emit_xb else outs[0]


def _retention_kernel(*refs, chunk, nchunk, nbat, has_state, layer, has_prev):
    refs = list(refs)
    (q_ref, k_ref, v_ref, g_ref, cos_ref, sin_ref, dmask_ref, qdec_ref, kdec_ref, cdec_ref) = refs[:10]
    refs = refs[10:]
    s0_ref = refs.pop(0) if has_state else None
    if has_prev:
        refs.pop(0)
    z_ref, sout_ref, s_scr = refs
    c = pl.program_id(1)

    @pl.when(c == 0)
    def _():
        if has_state:
            s_scr[...] = s0_ref[...]
        else:
            s_scr[...] = jnp.zeros_like(s_scr)

    for bi in range(nbat):
        for hd in range(H_A):
            s = s_scr[bi, hd]
            for ci in range(nchunk):
                rows = slice((bi * nchunk + ci) * chunk, (bi * nchunk + ci + 1) * chunk)
                trow = slice(ci * chunk, (ci + 1) * chunk)
                cos = cos_ref[trow, :]
                sin = sin_ref[trow, :]
                q = q_ref[rows, hd * DK_A:(hd + 1) * DK_A]
                k = k_ref[rows, hd * DK_A:(hd + 1) * DK_A]
                v = v_ref[rows, hd * DV_A:(hd + 1) * DV_A].astype(BF16)
                qr = q * cos + pltpu.roll(q, DK_A // 2, 1) * sin
                kr = (k * cos + pltpu.roll(k, DK_A // 2, 1) * sin) * (DK_A ** -0.5)
                qb = qr.astype(BF16)
                sc = lax.dot_general(qb, kr.astype(BF16), (((1,), (1,)), ((), ())),
                                     preferred_element_type=F32) * dmask_ref[hd]
                o = (jnp.dot(sc.astype(BF16), v, preferred_element_type=F32)
                     + jnp.dot(qb, s.astype(BF16), preferred_element_type=F32) * qdec_ref[hd])
                kd = (kr * kdec_ref[hd]).astype(BF16)
                s = s * cdec_ref[hd] + lax.dot_general(
                    kd, v, (((0,), (0,)), ((), ())), preferred_element_type=F32)
                o = o * lax.rsqrt(jnp.mean(o * o, axis=-1, keepdims=True) + RMS_EPS)
                g = g_ref[rows, hd * DV_A:(hd + 1) * DV_A]
                z_ref[rows, hd * DV_A:(hd + 1) * DV_A] = (o * _silu(g)).astype(z_ref.dtype)
            s_scr[bi, hd] = s

    @pl.when(c == pl.num_programs(1) - 1)
    def _():
        _state_store(sout_ref, (Ellipsis,), s_scr[...], layer, has_prev)


def _retention_tables(pos0, t, chunk):
    half = DK_A // 2
    inv = ROPE_BASE ** (-jnp.arange(half, dtype=F32) / half)
    pos = pos0 + jnp.arange(t, dtype=F32)
    ang = pos[:, None] * inv[None, :]
    cos, sin = jnp.cos(ang), jnp.sin(ang)
    cos2 = jnp.concatenate([cos, cos], -1)
    sin2 = jnp.concatenate([-sin, sin], -1)
    lg = jnp.log1p(-jnp.exp2(-5.0 - jnp.arange(H_A, dtype=F32)))
    idx = jnp.arange(chunk, dtype=F32)
    diff = idx[:, None] - idx[None, :]
    dmask = jnp.where(diff >= 0, jnp.exp(lg[:, None, None] * jnp.maximum(diff, 0.0)), 0.0)
    qdec = jnp.exp(lg[:, None] * (idx[None, :] + 1.0))
    kdec = jnp.exp(lg[:, None] * (chunk - 1.0 - idx[None, :]))
    cdec = jnp.exp(lg * chunk)
    qdec = jnp.broadcast_to(qdec[:, :, None], (H_A, chunk, DV_A))
    kdec = jnp.broadcast_to(kdec[:, :, None], (H_A, chunk, DK_A))
    cdec = jnp.broadcast_to(cdec[:, None, None], (H_A, 1, DV_A))
    return cos2, sin2, dmask, qdec, kdec, cdec


def _retention(h_ab, row0, batch, t, pos0, state, layer, prev_outs, *, name):
    chunk = min(RET_CHUNK, t)
    if t > chunk:
        nbat, nchunk = 1, RET_ROWS // chunk
    else:
        nbat, nchunk = SMALL_T_RET_BATCH, 1
    rows = nbat * nchunk * chunk
    nsteps = t // (nchunk * chunk)
    has_state = state is not None
    tables = _retention_tables(pos0, t, chunk)
    row = lambda b, c: b * nsteps + c
    hrow = lambda b, c: row(b, c) + _block_offset(row0, rows)
    in_specs = [
        pl.BlockSpec((rows, H_A * DK_A), lambda b, c: (hrow(b, c), OFF_QA // (H_A * DK_A))),
        pl.BlockSpec((rows, H_A * DK_A), lambda b, c: (hrow(b, c), OFF_KA // (H_A * DK_A))),
        pl.BlockSpec((rows, W_A), lambda b, c: (hrow(b, c), OFF_VA // W_A)),
        pl.BlockSpec((rows, W_A), lambda b, c: (hrow(b, c), OFF_GA // W_A)),
        pl.BlockSpec((nchunk * chunk, DK_A), lambda b, c: (c, 0)),
        pl.BlockSpec((nchunk * chunk, DK_A), lambda b, c: (c, 0)),
        pl.BlockSpec((H_A, chunk, chunk), lambda b, c: (0, 0, 0)),
        pl.BlockSpec((H_A, chunk, DV_A), lambda b, c: (0, 0, 0)),
        pl.BlockSpec((H_A, chunk, DK_A), lambda b, c: (0, 0, 0)),
        pl.BlockSpec((H_A, 1, DV_A), lambda b, c: (0, 0, 0)),
    ]
    args = [h_ab, h_ab, h_ab, h_ab, *tables]
    if has_state:
        in_specs.append(pl.BlockSpec((None, nbat, H_A, DK_A, DV_A), lambda b, c: (layer, b, 0, 0, 0)))
        args.append(state)
    a_args, a_specs, aliases = _alias_args(prev_outs, len(args))
    z, s_out = pl.pallas_call(
        functools.partial(_retention_kernel, chunk=chunk, nchunk=nchunk, nbat=nbat,
                          has_state=has_state, layer=layer, has_prev=prev_outs is not None),
        out_shape=(jax.ShapeDtypeStruct((batch * t, W_A), BF16),
                   jax.ShapeDtypeStruct((DEPTH, batch, H_A, DK_A, DV_A), F32)),
        grid=(batch // nbat, nsteps),
        in_specs=in_specs + a_specs,
        out_specs=(pl.BlockSpec((rows, W_A), lambda b, c: (row(b, c), 0)),
                   _state_out_spec((nbat, H_A, DK_A, DV_A), lambda b, c: (b, 0, 0, 0), layer,
                                   prev_outs is not None)),
        scratch_shapes=[pltpu.VMEM((nbat, H_A, DK_A, DV_A), F32)],
        input_output_aliases=aliases,
        compiler_params=_params("arbitrary", "arbitrary"),
        name=name,
    )(*args, *a_args)
    return z, (s_out,)


def _pool_kernel(*refs, nb, tb, pos0, has_state, layer, has_prev):
    refs = list(refs)
    u_ref, g_ref, halo_ref, wmap_ref, scale_ref = refs[:5]
    refs = refs[5:]
    if has_prev:
        refs.pop(0)
    z_ref, pout_ref, ext_scr, p_scr = refs
    ti = pl.program_id(1)
    for bi in range(nb):
        rows = pl.ds(bi * tb, tb)
        if has_state:
            ext_scr[0:1, :] = jnp.zeros((1, W_B), F32)
            ext_scr[1:POOL_HALO, :] = halo_ref[bi]
        else:
            ext_scr[0:POOL_HALO, :] = jnp.where(ti == 0, 0.0, halo_ref[...])
        ext_scr[POOL_HALO:POOL_HALO + tb, :] = u_ref[rows, :]
        for gi, w in enumerate(POOL_WINDOWS):
            cols = slice(gi * GW_B, (gi + 1) * GW_B)
            acc = ext_scr[:, cols]
            shift = 1
            while shift < w:
                acc = acc + pltpu.roll(acc, shift, 0)
                shift *= 2
            u = ext_scr[POOL_HALO:POOL_HALO + tb, cols]
            p_scr[rows, cols] = acc[POOL_HALO:POOL_HALO + tb] * (1.0 / w) - u
            if pos0 < POOL_BUF:
                head = min(tb, POOL_HALO)

                @pl.when(ti == 0)
                def _():
                    t_idx = pos0 + lax.broadcasted_iota(jnp.int32, (head, GW_B), 0)
                    cnt = jnp.minimum(t_idx + 1, w).astype(F32)
                    p_scr[pl.ds(bi * tb, head), cols] = (
                        acc[POOL_HALO:POOL_HALO + head] / cnt - u[:head])

        @pl.when(ti == pl.num_programs(1) - 1)
        def _():
            _state_store(pout_ref, (bi,), ext_scr[tb + 1:tb + POOL_HALO, :], layer, has_prev)

    for gi in range(N_POOL_GROUPS):
        cols = slice(gi * GW_B, (gi + 1) * GW_B)
        pm = jnp.dot(p_scr[:, cols].astype(BF16), wmap_ref[gi], preferred_element_type=F32)
        pm = pm * scale_ref[:, cols]
        z_ref[:, cols] = (pm * _silu(g_ref[:, cols])).astype(z_ref.dtype)


def _pool(h_ab, row0, batch, t, pos0, state, layer, prev_outs, wmap, scale, *, name):
    has_state = state is not None
    if t >= POOL_ROWS:
        nb, tb = 1, POOL_ROWS
    else:
        nb, tb = SMALL_T_POOL_BATCH, t
    nt = t // tb
    rows = nb * tb
    rblk = lambda b, i: b * nt + i
    hblk = lambda b, i: rblk(b, i) + _block_offset(row0, rows)
    in_specs = [pl.BlockSpec((rows, W_B), lambda b, i: (hblk(b, i), OFF_UB // W_B)),
                pl.BlockSpec((rows, W_B), lambda b, i: (hblk(b, i), OFF_GB // W_B))]
    args = [h_ab, h_ab]
    if has_state:
        assert nt == 1
        in_specs.append(pl.BlockSpec((None, nb, POOL_BUF, W_B), lambda b, i: (layer, b, 0, 0)))
        args.append(state)
    else:
        assert nb == 1
        per = tb // POOL_HALO
        in_specs.append(pl.BlockSpec(
            (POOL_HALO, W_B), lambda b, i: (jnp.maximum(hblk(b, i) * per - 1, 0), OFF_UB // W_B)))
        args.append(h_ab)
    in_specs += [pl.BlockSpec((N_POOL_GROUPS, GW_B, GW_B), lambda b, i: (0, 0, 0)),
                 pl.BlockSpec((1, W_B), lambda b, i: (0, 0))]
    args += [wmap, scale]
    a_args, a_specs, aliases = _alias_args(prev_outs, len(args))
    z, p_out = pl.pallas_call(
        functools.partial(_pool_kernel, nb=nb, tb=tb, pos0=pos0, has_state=has_state,
                          layer=layer, has_prev=prev_outs is not None),
        out_shape=(jax.ShapeDtypeStruct((batch * t, W_B), BF16),
                   jax.ShapeDtypeStruct((DEPTH, batch, POOL_BUF, W_B), F32)),
        grid=(batch // nb, nt),
        in_specs=in_specs + a_specs,
        out_specs=(pl.BlockSpec((rows, W_B), lambda b, i: (rblk(b, i), 0)),
                   _state_out_spec((nb, POOL_BUF, W_B), lambda b, i: (b, 0, 0), layer,
                                   prev_outs is not None)),
        scratch_shapes=[pltpu.VMEM((POOL_HALO + tb, W_B), F32),
                        pltpu.VMEM((rows, W_B), F32)],
        input_output_aliases=aliases,
        compiler_params=_params("arbitrary", "arbitrary"),
        name=name,
    )(*args, *a_args)
    return z, (p_out,)


def _t5_bucket_table(bq):
    span = WINDOW + bq
    dist = np.arange(bq)[:, None] + WINDOW - np.arange(span)[None, :]
    max_exact = NUM_BUCKETS // 2
    d = np.maximum(dist, 0).astype(np.float32)
    large = max_exact + (np.log(np.maximum(d, np.float32(1.0)) / np.float32(max_exact))
                         / np.float32(math.log(MAX_DISTANCE / max_exact))
                         * np.float32(NUM_BUCKETS - max_exact)).astype(np.int32)
    large = np.minimum(large, NUM_BUCKETS - 1)
    bucket = np.where(dist < max_exact, np.maximum(dist, 0), large)
    valid = (dist >= 0) & (dist < WINDOW)
    return np.where(valid, bucket, -1).astype(np.int32)


def _attention_kernel(*refs, bq, nbat, has_cache, mask_first, layer, has_prev):
    refs = list(refs)
    (sinks_ref, relb_ref, q_ref, kc_ref, vc_ref, kp_ref, vp_ref, g_ref, bucket_ref) = refs[:9]
    refs = refs[9:]
    if has_prev:
        del refs[:2]
    z_ref, wk_ref, wv_ref, bias_scr, qbd_scr, s_scr, p_scr, o_scr, l_scr = refs
    span = WINDOW + bq
    hrows = H_C * bq
    grows = G_C * bq
    chunk = min(hrows, SOFTMAX_ROWS)
    one_matmul = hrows <= SOFTMAX_ROWS
    fmin = jnp.finfo(F32).min
    n = pl.program_id(1)

    @pl.when((pl.program_id(0) == 0) & (n == 0))
    def _():
        bucket = bucket_ref[...]
        col = lax.broadcasted_iota(jnp.int32, (bq, span), 1)
        for h in range(H_C):
            acc = jnp.where(bucket < 0, fmin, 0.0)
            for b in range(NUM_BUCKETS):
                acc = jnp.where(bucket == b, relb_ref[b, h] * LOG2E, acc)
            acc = jnp.where(col == 0, sinks_ref[h] * LOG2E, acc)
            bias_scr[0, h * bq:(h + 1) * bq, :] = acc
            if mask_first:
                bias_scr[1, h * bq:(h + 1) * bq, :] = jnp.where((col < WINDOW) & (col > 0), fmin, acc)

    lane_kv = lax.broadcasted_iota(jnp.int32, (1, KVW), 1) // HD_C
    low_half = lax.broadcasted_iota(jnp.int32, (1, 2 * HD_C), 1) < HD_C
    row0 = lax.broadcasted_iota(jnp.int32, (WINDOW, KVW), 0) == 0
    col0_t = lax.broadcasted_iota(jnp.int32, (KVW, WINDOW), 1) == 0
    ones = jnp.ones((span, DENOM_W), BF16)
    kc = vc = None
    for bi in range(nbat):
        rows = slice(bi * bq, (bi + 1) * bq)
        base = bi * hrows
        if has_cache:
            kp, vp = kp_ref[bi], vp_ref[bi]
        elif bi == 0:
            kp, vp = kp_ref[...], vp_ref[...]
        else:
            kp, vp = kc, vc
        bsel = jnp.where(n == 0, 1, 0) if (mask_first and bi == 0) else 0
        kc = kc_ref[rows, :]
        vc = vc_ref[rows, :]
        q = q_ref[rows, :] * (HD_C ** -0.5 * LOG2E)
        for kk in range(KV_C):
            qbd_scr[base + kk * grows:base + (kk + 1) * grows, :] = jnp.concatenate(
                [jnp.where(lane_kv == kk, q[:, g * KVW:(g + 1) * KVW], 0.0) for g in range(G_C)],
                axis=0).astype(BF16)
        qbd = qbd_scr[base:base + hrows, :]
        nt_dims = (((1,), (1,)), ((), ()))
        if has_cache:
            kpt = jnp.where(col0_t, 0.0, kp).astype(BF16)
            s_scr[base:base + hrows, :] = jnp.concatenate(
                [jnp.dot(qbd, kpt, preferred_element_type=F32),
                 lax.dot_general(qbd, kc.astype(BF16), nt_dims, preferred_element_type=F32)], axis=1)
        else:
            kx = jnp.concatenate([jnp.where(row0, 0.0, kp), kc], axis=0).astype(BF16)
            vx = jnp.concatenate([jnp.where(row0, 0.0, vp), vc], axis=0).astype(BF16)
            s_scr[base:base + hrows, :] = lax.dot_general(qbd, kx, nt_dims,
                                                          preferred_element_type=F32)
        for c in range(hrows // chunk):
            rs = slice(base + c * chunk, base + (c + 1) * chunk)
            s = s_scr[rs, :] + bias_scr[bsel, c * chunk:(c + 1) * chunk, :]
            m = jnp.max(s, axis=-1, keepdims=True)
            p_scr[rs, :] = jnp.exp2(s - m).astype(BF16)
        if has_cache:
            assert one_matmul
            p = p_scr[base:base + hrows, :]
            vpt1 = jnp.concatenate([jnp.where(col0_t, 0.0, vp).astype(BF16),
                                    jnp.ones((DENOM_W, WINDOW), BF16)], axis=0)
            vc1 = jnp.concatenate([vc.astype(BF16), jnp.ones((bq, DENOM_W), BF16)], axis=1)
            o3 = (lax.dot_general(p[:, :WINDOW], vpt1, nt_dims, preferred_element_type=F32)
                  + jnp.dot(p[:, WINDOW:], vc1, preferred_element_type=F32))
        elif one_matmul:
            o3 = jnp.dot(p_scr[base:base + hrows, :], jnp.concatenate([vx, ones], axis=1),
                         preferred_element_type=F32)
        if one_matmul:
            for kk in range(KV_C):
                o_scr[base + kk * grows:base + (kk + 1) * grows, :] = (
                    o3[kk * grows:(kk + 1) * grows, (kk // 2) * 2 * HD_C:(kk // 2 + 1) * 2 * HD_C])
            l_scr[base:base + hrows, :] = o3[:, KVW:KVW + DENOM_W]
        else:
            for kk in range(KV_C):
                rk = slice(base + kk * grows, base + (kk + 1) * grows)
                slab = vx[:, (kk // 2) * 2 * HD_C:(kk // 2 + 1) * 2 * HD_C]
                vk = jnp.where(low_half if kk % 2 == 0 else ~low_half, slab, jnp.zeros_like(slab))
                ol = jnp.dot(p_scr[rk, :], jnp.concatenate([vk, ones], axis=1),
                             preferred_element_type=F32)
                o_scr[rk, :] = ol[:, :2 * HD_C]
                l_scr[rk, :] = ol[:, 2 * HD_C:]
        for g in range(G_C):
            halves = []
            for half in range(2):
                ra, rb = (slice(base + kk * grows + g * bq, base + kk * grows + (g + 1) * bq)
                          for kk in (2 * half, 2 * half + 1))
                num = jnp.where(low_half, o_scr[ra, :], o_scr[rb, :])
                den = jnp.where(low_half, l_scr[ra, :], l_scr[rb, :])
                halves.append(num * (1.0 / den))
            og = jnp.concatenate(halves, axis=1)
            cols = slice(g * KVW, (g + 1) * KVW)
            z_ref[rows, cols] = (og * _silu(g_ref[rows, cols])).astype(z_ref.dtype)

        if has_cache:
            keep = lax.broadcasted_iota(jnp.int32, (KVW, WINDOW), 1) < WINDOW - bq
            pad = jnp.zeros((WINDOW - bq, KVW), F32)
            for ref, old, new in ((wk_ref, kp, kc), (wv_ref, vp, vc)):
                new_t = jnp.concatenate([pad, new], axis=0).T
                _state_store(ref, (bi,), jnp.where(keep, pltpu.roll(old, WINDOW - bq, 1), new_t),
                             layer, has_prev)
        elif bi == nbat - 1:
            @pl.when(n == pl.num_programs(1) - 1)
            def _():
                _state_store(wk_ref, (0,), kc, layer, has_prev)
                _state_store(wv_ref, (0,), vc, layer, has_prev)


def _attention(h_att, row0, batch, t, pos0, cache_k, cache_v, layer, prev_outs, sinks, rel_bias, *,
               name):
    bq = min(ATT_BLOCK, t)
    nb = t // bq
    has_cache = cache_k is not None
    nbat = SMALL_T_ATT_BATCH if has_cache else ATT_BLOCKS_PER_STEP
    nsteps = 1 if has_cache else nb // nbat
    wb = nbat if has_cache else 1
    win_shape = (KVW, WINDOW) if has_cache else (WINDOW, KVW)
    span = WINDOW + bq
    assert pos0 == 0 or pos0 >= WINDOW
    mask_first = pos0 == 0
    rows = nbat * bq
    row = lambda b, n, *_: b * nsteps + n
    hrow = lambda b, n, *_: row(b, n) + _block_offset(row0, rows)
    in_specs = [
        pl.BlockSpec((rows, W_C), lambda b, n, *_: (hrow(b, n), OFF_QC // W_C)),
        pl.BlockSpec((rows, KVW), lambda b, n, *_: (hrow(b, n), OFF_KC // KVW)),
        pl.BlockSpec((rows, KVW), lambda b, n, *_: (hrow(b, n), OFF_VC // KVW)),
    ]
    args = [h_att, h_att, h_att]
    if has_cache:
        assert nb == 1
        in_specs += [pl.BlockSpec((None, nbat) + win_shape, lambda b, n, *_: (layer, b, 0, 0)),
                     pl.BlockSpec((None, nbat) + win_shape, lambda b, n, *_: (layer, b, 0, 0))]
        args += [cache_k, cache_v]
    else:
        assert bq == WINDOW and nb % nbat == 0
        prev = lambda b, n, *_: jnp.maximum(hrow(b, n) * nbat - 1, 0)
        in_specs += [pl.BlockSpec((WINDOW, KVW), lambda b, n, *_: (prev(b, n), OFF_KC // KVW)),
                     pl.BlockSpec((WINDOW, KVW), lambda b, n, *_: (prev(b, n), OFF_VC // KVW))]
        args += [h_att, h_att]
    in_specs += [
        pl.BlockSpec((rows, W_C), lambda b, n, *_: (hrow(b, n), OFF_GC // W_C)),
        pl.BlockSpec((bq, span), lambda b, n, *_: (0, 0)),
    ]
    args += [h_att, jnp.asarray(_t5_bucket_table(bq))]
    a_args, a_specs, aliases = _alias_args(prev_outs, 2 + len(args))
    srows = nbat * H_C * bq
    z, wk, wv = pl.pallas_call(
        functools.partial(_attention_kernel, bq=bq, nbat=nbat, has_cache=has_cache,
                          mask_first=mask_first, layer=layer, has_prev=prev_outs is not None),
        out_shape=(jax.ShapeDtypeStruct((batch * t, W_C), BF16),
                   jax.ShapeDtypeStruct((DEPTH, batch) + win_shape, F32),
                   jax.ShapeDtypeStruct((DEPTH, batch) + win_shape, F32)),
        grid_spec=pltpu.PrefetchScalarGridSpec(
            num_scalar_prefetch=2,
            grid=(batch // wb, nsteps),
            in_specs=in_specs + a_specs,
            out_specs=(pl.BlockSpec((rows, W_C), lambda b, n, *_: (row(b, n), 0)),
                       _state_out_spec((wb,) + win_shape, lambda b, n, *_: (b, 0, 0), layer,
                                       prev_outs is not None),
                       _state_out_spec((wb,) + win_shape, lambda b, n, *_: (b, 0, 0), layer,
                                       prev_outs is not None)),
            scratch_shapes=[pltpu.VMEM((2 if mask_first else 1, H_C * bq, span), F32),
                            pltpu.VMEM((srows, KVW), BF16),
                            pltpu.VMEM((srows, span), F32),
                            pltpu.VMEM((srows, span), BF16),
                            pltpu.VMEM((srows, 2 * HD_C), F32),
                            pltpu.VMEM((srows, DENOM_W), F32)]),
        input_output_aliases=aliases,
        compiler_params=_params("arbitrary", "arbitrary"),
        name=name,
    )(sinks, rel_bias, *args, *a_args)
    return z, (wk, wv)


def _out_kernel(x_ref, za_ref, zb_ref, zc_ref, ma_ref, mb_ref, mc_ref, wa_ref, wb_ref, wc_ref,
                wo_ref, lng_ref, lnb_ref, *rest, alpha, emit_bf16):
    if emit_bf16:
        y_ref, yb_ref, merged_scr, r_scr = rest
    else:
        y_ref, merged_scr, r_scr = rest

    @pl.when(pl.program_id(0) == 0)
    def _():
        r_scr[...] = jnp.zeros_like(r_scr)

    lng, lnb = lng_ref[...], lnb_ref[...]
    for rc in range(r_scr.shape[0] // LN_ROWS):
        rs = slice(rc * LN_ROWS, (rc + 1) * LN_ROWS)
        r = r_scr[rs, :]
        mu = jnp.mean(r, axis=-1, keepdims=True)
        d = r - mu
        var = jnp.mean(d * d, axis=-1, keepdims=True)
        out = d * lax.rsqrt(var + LN_EPS) * lng + lnb
        y_ref[rs, :] = out
        if emit_bf16:
            yb_ref[rs, :] = out.astype(BF16)

    za, zb, zc = za_ref[...], zb_ref[...], zc_ref[...]
    for c in range(D_MODEL // OUT_STAGE_TN):
        cs = slice(c * OUT_STAGE_TN, (c + 1) * OUT_STAGE_TN)
        merged = (_sigmoid(ma_ref[:, cs].astype(F32))
                  * jnp.dot(za, wa_ref[:, cs], preferred_element_type=F32))
        merged += (_sigmoid(mb_ref[:, cs].astype(F32))
                   * jnp.dot(zb, wb_ref[:, cs], preferred_element_type=F32))
        merged += (_sigmoid(mc_ref[:, cs].astype(F32))
                   * jnp.dot(zc, wc_ref[:, cs], preferred_element_type=F32))
        merged_scr[:, cs] = merged.astype(BF16)
    r_scr[...] = alpha * x_ref[...] + jnp.dot(merged_scr[...], wo_ref[...],
                                              preferred_element_type=F32)


def _out_stage(x, za, zb, zc, h_gate, row0, wa, wb, wc, wo, lng, lnb, alpha, *, emit_bf16, name):
    m = x.shape[0]
    tm = OUT_STAGE_TM
    nsteps = m // tm
    cur = lambda i: jnp.minimum(i, nsteps - 1)
    rowblk = lambda w: pl.BlockSpec((tm, w), lambda i: (cur(i), 0))
    gate = lambda j: pl.BlockSpec((tm, D_MODEL), lambda i: (cur(i) + _block_offset(row0, tm), j))
    outblk = pl.BlockSpec((tm, D_MODEL), lambda i: (jnp.maximum(i - 1, 0), 0))
    const = lambda a: pl.BlockSpec(a.shape, lambda i: (0,) * a.ndim, pipeline_mode=pl.Buffered(1))
    out_shape = [jax.ShapeDtypeStruct((m, D_MODEL), F32)]
    if emit_bf16:
        out_shape.append(jax.ShapeDtypeStruct((m, D_MODEL), BF16))
    outs = pl.pallas_call(
        functools.partial(_out_kernel, alpha=alpha, emit_bf16=emit_bf16),
        out_shape=out_shape,
        grid=(nsteps + 1,),
        in_specs=[rowblk(D_MODEL), rowblk(W_A), rowblk(W_B), rowblk(W_C),
                  gate(0), gate(1), gate(2),
                  const(wa), const(wb), const(wc), const(wo), const(lng), const(lnb)],
        out_specs=[outblk] * len(out_shape),
        scratch_shapes=[pltpu.VMEM((tm, D_MODEL), BF16), pltpu.VMEM((tm, D_MODEL), F32)],
        compiler_params=_params("arbitrary"),
        name=name,
    )(x, za, zb, zc, h_gate, h_gate, h_gate, wa, wb, wc, wo, lng, lnb)
    return (outs[0], outs[1]) if emit_bf16 else (outs[0], None)


def _layer(groups, xs, xbs, layer, prevs, w, w_in, alpha):
    if xbs is None:
        h_att, xb_all = _in_proj(xs, w["w_att"], N_ATT, emit_xb=True, name=f"in_proj_att_{layer}")
        xbs = (xb_all,)
    else:
        h_att = _in_proj(xbs, w["w_att"], N_ATT, name=f"in_proj_att_{layer}")
    h_ab = _in_proj(xbs, w_in, N_AB, layer=layer, col0=0, name=f"in_proj_ab_{layer}")
    h_gate = _in_proj(xbs, w_in, N_GATE, layer=layer, col0=REF_GATE, out_dtype=BF16,
                      name=f"in_proj_gate_{layer}")
    ys, ybs, outs, row0 = [], [], [], 0
    for (batch, t, pos0, states, tag), x, prev in zip(groups, xs, prevs):
        st_ret, st_k, st_v, st_pool = states if states is not None else (None,) * 4
        pv_ret, pv_win, pv_pool = prev if prev is not None else (None,) * 3
        za, o_ret = _retention(h_ab, row0, batch, t, pos0, st_ret, layer, pv_ret,
                               name=f"retention_{tag}{layer}")
        zb, o_pool = _pool(h_ab, row0, batch, t, pos0, st_pool, layer, pv_pool, w["w_pool_map"],
                           w["pool_scale"], name=f"pool_{tag}{layer}")
        zc, o_win = _attention(h_att, row0, batch, t, pos0, st_k, st_v, layer, pv_win, w["sinks"],
                               w["rel_bias"], name=f"attention_{tag}{layer}")
        y, yb = _out_stage(x, za, zb, zc, h_gate, row0, w["w_ret_o"], w["w_pool_o"], w["w_att_o"],
                           w["w_out"], w["ln_g"], w["ln_b"], alpha, emit_bf16=layer + 1 < DEPTH,
                           name=f"out_stage_{tag}{layer}")
        ys.append(y)
        ybs.append(yb)
        outs.append((o_ret, o_win, o_pool))
        row0 += batch * t
    return ys, (tuple(ybs) if layer + 1 < DEPTH else None), outs


def _regroup_heads(w, axis):
    shp = w.shape
    w = w.reshape(shp[:axis] + (KV_C, G_C, HD_C) + shp[axis + 1:])
    w = jnp.swapaxes(w, axis, axis + 1)
    return w.reshape(shp)


def _regroup_matrix():
    new = np.arange(W_C)
    g, kv, d = new // KVW, (new // HD_C) % KV_C, new % HD_C
    p = np.zeros((W_C, W_C), np.float32)
    p[(kv * G_C + g) * HD_C + d, new] = 1.0
    return jnp.asarray(p, BF16)


def _w_att_kernel(q_ref, kv_ref, glo_ref, ghi_ref, perm_ref, o_ref):
    perm = perm_ref[...]
    regroup = lambda w: jnp.dot(w.astype(BF16), perm, preferred_element_type=F32).astype(BF16)
    o_ref[:, OFF_QC:OFF_QC + W_C] = regroup(q_ref[...])
    o_ref[:, OFF_GC:OFF_GC + W_C] = regroup(jnp.concatenate([glo_ref[...], ghi_ref[...]], axis=1))
    o_ref[:, OFF_KC:OFF_KC + 2 * KVW] = kv_ref[...].astype(BF16)


def _prep_w_att(w_in, l):
    half = W_C // 2
    assert REF_QC % W_C == 0 and REF_KC % half == 0 and REF_GC % half == 0
    assert REF_VC == REF_KC + KVW and OFF_VC == OFF_KC + KVW
    blk = lambda width, c0: pl.BlockSpec((None, W_PREP_ROWS, width), lambda i: (l, i, c0 // width))
    return pl.pallas_call(
        _w_att_kernel,
        out_shape=jax.ShapeDtypeStruct((D_MODEL, N_ATT), BF16),
        grid=(D_MODEL // W_PREP_ROWS,),
        in_specs=[blk(W_C, REF_QC), blk(half, REF_KC), blk(half, REF_GC), blk(half, REF_GC + half),
                  pl.BlockSpec((W_C, W_C), lambda i: (0, 0))],
        out_specs=pl.BlockSpec((W_PREP_ROWS, N_ATT), lambda i: (i, 0)),
        compiler_params=_params("parallel"),
        name=f"w_att_prep_{l}",
    )(w_in, w_in, w_in, w_in, _regroup_matrix())


def kernel(x_prompt, x_sample, state_ret, cache_win_k, cache_win_v, state_pool, w_in, w_ret_o,
           w_pool_map, pool_scale, w_pool_o, attn_sinks, w_att_o, w_out, ln_g, ln_b, rel_bias):
    alpha = (2.0 * DEPTH) ** 0.25
    bp, tp, _ = x_prompt.shape
    bs, ts, _ = x_sample.shape
    xp = x_prompt.reshape(bp * tp, D_MODEL)
    xs = x_sample.reshape(bs * ts, D_MODEL)
    to_t = lambda c: c.transpose(0, 1, 3, 4, 2).reshape(DEPTH, bs, KVW, WINDOW)
    from_t = lambda c: c.reshape(DEPTH, bs, KV_C, HD_C, WINDOW).transpose(0, 1, 4, 2, 3)
    ck, cv = to_t(cache_win_k), to_t(cache_win_v)
    groups = ((bp, tp, 0, None, "p"), (bs, ts, PAST_LEN, (state_ret, ck, cv, state_pool), "s"))
    xs, xbs, outs = (xp, xs), None, (None, None)
    for l in range(DEPTH):
        w = {
            "w_att": _prep_w_att(w_in, l),
            "w_ret_o": w_ret_o[l].astype(BF16),
            "w_pool_map": w_pool_map[l].astype(BF16),
            "pool_scale": pool_scale[l].reshape(1, W_B),
            "w_pool_o": w_pool_o[l].astype(BF16),
            "sinks": attn_sinks[l],
            "w_att_o": _regroup_heads(w_att_o[l], 0).astype(BF16),
            "w_out": w_out[l].astype(BF16),
            "ln_g": ln_g[l].reshape(1, D_MODEL),
            "ln_b": ln_b[l].reshape(1, D_MODEL),
            "rel_bias": rel_bias,
        }
        xs, xbs, outs = _layer(groups, xs, xbs, l, outs, w, w_in, alpha)
    ((ret_p,), (kp, vp), (pp,)), ((ret_s,), (ks, vs), (ps,)) = outs
    win = lambda a, b: a.reshape(DEPTH, b, WINDOW, KV_C, HD_C)
    return (xs[0].reshape(bp, tp, D_MODEL), xs[1].reshape(bs, ts, D_MODEL), ret_p, ret_s,
            win(kp, bp), from_t(ks), win(vp, bp), from_t(vs), pp, ps)
```

```python
import functools
import math

import jax
import jax.numpy as jnp
import numpy as np
from jax import lax
from jax.experimental import pallas as pl
from jax.experimental.pallas import tpu as pltpu

D_MODEL = 2048
DEPTH = 2
PAST_LEN = 8192
H_A, DK_A, DV_A = 4, 128, 256
W_A = H_A * DV_A
RET_CHUNK = 128
ROPE_BASE = 10000.0
N_POOL_GROUPS = 4
W_B = 1024
GW_B = W_B // N_POOL_GROUPS
POOL_WINDOWS = (2, 4, 8, 16)
POOL_BUF = 15
POOL_HALO = 16
H_C, KV_C, HD_C = 16, 4, 64
G_C = H_C // KV_C
W_C = H_C * HD_C
KVW = KV_C * HD_C
WINDOW = 128
ATT_BLOCK = 128
SOFTMAX_ROWS = 128
DENOM_W = 2 * HD_C
NUM_BUCKETS = 32
MAX_DISTANCE = 128
LN_EPS = 1e-5
RMS_EPS = 1e-6

REF_QC, REF_KC, REF_VC, REF_GC = 5120, 6144, 6400, 6656
N_AB = 5120
N_ATT = 2560
REF_GATE = 7680
N_GATE = 3 * D_MODEL
OFF_QA, OFF_KA, OFF_VA, OFF_GA, OFF_UB, OFF_GB = 0, 512, 1024, 2048, 3072, 4096
OFF_QC, OFF_GC, OFF_KC, OFF_VC = 0, 1024, 2048, 2304

F32 = jnp.float32
BF16 = jnp.bfloat16
VMEM_LIMIT = 56 * 1024 * 1024

IN_PROJ_TM = 1024
IN_PROJ_TM_F32_X = 512
IN_PROJ_TN = {N_AB: 1280, N_ATT: 2560, N_GATE: 1536}
OUT_STAGE_TM = 256
OUT_STAGE_TN = 512
LN_ROWS = 16
POOL_ROWS = 1024
SMALL_T_POOL_BATCH = 16
W_PREP_ROWS = 512
ATT_BLOCKS_PER_STEP = 4
SMALL_T_ATT_BATCH = 8
RET_ROWS = 512
SMALL_T_RET_BATCH = 8


LOG2E = 1.0 / math.log(2.0)


def _sigmoid(x):
    return 1.0 / (1.0 + jnp.exp2(x * (-LOG2E)))


def _silu(x):
    return x * _sigmoid(x)


def _params(*sem):
    return pltpu.CompilerParams(dimension_semantics=sem, vmem_limit_bytes=VMEM_LIMIT)


def _alias_args(prev_outs, n_in):
    if prev_outs is None:
        return [], [], {}
    specs = [pl.BlockSpec(memory_space=pl.ANY) for _ in prev_outs]
    return list(prev_outs), specs, {n_in + i: 1 + i for i in range(len(prev_outs))}


def _block_offset(row0, rows):
    assert row0 % rows == 0
    return row0 // rows


def _state_out_spec(block, index_map, layer, has_prev):
    if has_prev:
        return pl.BlockSpec((None,) + block, lambda *a: (layer,) + index_map(*a))
    return pl.BlockSpec((DEPTH,) + block, lambda *a: (0,) + index_map(*a))


def _state_store(ref, idx, value, layer, has_prev):
    if has_prev:
        ref[idx] = value
    else:
        for l in range(DEPTH):
            ref[(l,) + idx] = value if l == layer else jnp.zeros_like(value)


def _matmul_kernel(*refs, n_x, first_blocks, cast_w, emit_xb):
    refs = list(refs)
    x_refs, (w_ref, o_ref), rest = refs[:n_x], refs[n_x:n_x + 2], refs[n_x + 2:]
    xb_ref = rest.pop(0) if emit_xb else None
    if cast_w:
        wb_scr, = rest

        @pl.when(pl.program_id(1) == 0)
        def _():
            wb_scr[...] = w_ref[...].astype(BF16)
        w_ref = wb_scr

    x = x_refs[0][...]
    if n_x == 2:
        x = jnp.where(pl.program_id(1) < first_blocks, x, x_refs[1][...])
    x = x.astype(BF16)
    if emit_xb:
        xb_ref[...] = x
    o_ref[...] = jnp.dot(x, w_ref[...], preferred_element_type=F32).astype(o_ref.dtype)


def _in_proj(xs, w, n, *, layer=None, col0=0, emit_xb=False, out_dtype=F32, name):
    assert len(xs) in (1, 2)
    k = xs[0].shape[1]
    m = sum(x.shape[0] for x in xs)
    tm, tn = (IN_PROJ_TM_F32_X if xs[0].dtype == F32 else IN_PROJ_TM), IN_PROJ_TN[n]
    first = xs[0].shape[0] // tm
    assert all(x.shape[0] % tm == 0 for x in xs)
    cast_w = w.dtype == F32
    if cast_w:
        w_spec = pl.BlockSpec((None, k, tn), lambda j, i: (layer, 0, col0 // tn + j))
    else:
        w_spec = pl.BlockSpec((k, tn), lambda j, i: (0, j),
                              pipeline_mode=pl.Buffered(1) if n == tn else None)
    if len(xs) == 1:
        x_specs = [pl.BlockSpec((tm, k), lambda j, i: (i, 0))]
    else:
        last = xs[1].shape[0] // tm - 1
        x_specs = [pl.BlockSpec((tm, k), lambda j, i: (jnp.minimum(i, first - 1), 0)),
                   pl.BlockSpec((tm, k), lambda j, i: (jnp.clip(i - first, 0, last), 0),
                                pipeline_mode=None if last else pl.Buffered(1))]
    assert not emit_xb or n == tn
    out_shape = [jax.ShapeDtypeStruct((m, n), out_dtype)]
    out_specs = [pl.BlockSpec((tm, tn), lambda j, i: (i, j))]
    if emit_xb:
        out_shape.append(jax.ShapeDtypeStruct((m, k), BF16))
        out_specs.append(pl.BlockSpec((tm, k), lambda j, i: (i, 0)))
    outs = pl.pallas_call(
        functools.partial(_matmul_kernel, n_x=len(xs), first_blocks=first, cast_w=cast_w,
                          emit_xb=emit_xb),
        out_shape=out_shape,
        grid=(n // tn, m // tm),
        in_specs=x_specs + [w_spec],
        out_specs=out_specs,
        scratch_shapes=[pltpu.VMEM((k, tn), BF16)] if cast_w else [],
        compiler_params=_params("arbitrary", "arbitrary"),
        name=name,
    )(*xs, w)
    return outs if emit_xb else outs[0]


def _retention_kernel(*refs, chunk, nchunk, nbat, has_state, layer, has_prev):
    refs = list(refs)
    (q_ref, k_ref, v_ref, g_ref, cos_ref, sin_ref, dmask_ref, qdec_ref, kdec_ref, cdec_ref) = refs[:10]
    refs = refs[10:]
    s0_ref = refs.pop(0) if has_state else None
    if has_prev:
        refs.pop(0)
    z_ref, sout_ref, s_scr = refs
    c = pl.program_id(1)

    @pl.when(c == 0)
    def _():
        if has_state:
            s_scr[...] = s0_ref[...]
        else:
            s_scr[...] = jnp.zeros_like(s_scr)

    for bi in range(nbat):
        for hd in range(H_A):
            s = s_scr[bi, hd]
            for ci in range(nchunk):
                rows = slice((bi * nchunk + ci) * chunk, (bi * nchunk + ci + 1) * chunk)
                trow = slice(ci * chunk, (ci + 1) * chunk)
                cos = cos_ref[trow, :]
                sin = sin_ref[trow, :]
                q = q_ref[rows, hd * DK_A:(hd + 1) * DK_A]
                k = k_ref[rows, hd * DK_A:(hd + 1) * DK_A]
                v = v_ref[rows, hd * DV_A:(hd + 1) * DV_A].astype(BF16)
                qr = q * cos + pltpu.roll(q, DK_A // 2, 1) * sin
                kr = (k * cos + pltpu.roll(k, DK_A // 2, 1) * sin) * (DK_A ** -0.5)
                qb = qr.astype(BF16)
                sc = lax.dot_general(qb, kr.astype(BF16), (((1,), (1,)), ((), ())),
                                     preferred_element_type=F32) * dmask_ref[hd]
                o = (jnp.dot(sc.astype(BF16), v, preferred_element_type=F32)
                     + jnp.dot(qb, s.astype(BF16), preferred_element_type=F32) * qdec_ref[hd])
                kd = (kr * kdec_ref[hd]).astype(BF16)
                s = s * cdec_ref[hd] + lax.dot_general(
                    kd, v, (((0,), (0,)), ((), ())), preferred_element_type=F32)
                o = o * lax.rsqrt(jnp.mean(o * o, axis=-1, keepdims=True) + RMS_EPS)
                g = g_ref[rows, hd * DV_A:(hd + 1) * DV_A]
                z_ref[rows, hd * DV_A:(hd + 1) * DV_A] = (o * _silu(g)).astype(z_ref.dtype)
            s_scr[bi, hd] = s

    @pl.when(c == pl.num_programs(1) - 1)
    def _():
        _state_store(sout_ref, (Ellipsis,), s_scr[...], layer, has_prev)


def _retention_tables(pos0, t, chunk):
    half = DK_A // 2
    inv = ROPE_BASE ** (-jnp.arange(half, dtype=F32) / half)
    pos = pos0 + jnp.arange(t, dtype=F32)
    ang = pos[:, None] * inv[None, :]
    cos, sin = jnp.cos(ang), jnp.sin(ang)
    cos2 = jnp.concatenate([cos, cos], -1)
    sin2 = jnp.concatenate([-sin, sin], -1)
    lg = jnp.log1p(-jnp.exp2(-5.0 - jnp.arange(H_A, dtype=F32)))
    idx = jnp.arange(chunk, dtype=F32)
    diff = idx[:, None] - idx[None, :]
    dmask = jnp.where(diff >= 0, jnp.exp(lg[:, None, None] * jnp.maximum(diff, 0.0)), 0.0)
    qdec = jnp.exp(lg[:, None] * (idx[None, :] + 1.0))
    kdec = jnp.exp(lg[:, None] * (chunk - 1.0 - idx[None, :]))
    cdec = jnp.exp(lg * chunk)
    qdec = jnp.broadcast_to(qdec[:, :, None], (H_A, chunk, DV_A))
    kdec = jnp.broadcast_to(kdec[:, :, None], (H_A, chunk, DK_A))
    cdec = jnp.broadcast_to(cdec[:, None, None], (H_A, 1, DV_A))
    return cos2, sin2, dmask, qdec, kdec, cdec


def _retention(h_ab, row0, batch, t, pos0, state, layer, prev_outs, *, name):
    chunk = min(RET_CHUNK, t)
    if t > chunk:
        nbat, nchunk = 1, RET_ROWS // chunk
    else:
        nbat, nchunk = SMALL_T_RET_BATCH, 1
    rows = nbat * nchunk * chunk
    nsteps = t // (nchunk * chunk)
    has_state = state is not None
    tables = _retention_tables(pos0, t, chunk)
    row = lambda b, c: b * nsteps + c
    hrow = lambda b, c: row(b, c) + _block_offset(row0, rows)
    in_specs = [
        pl.BlockSpec((rows, H_A * DK_A), lambda b, c: (hrow(b, c), OFF_QA // (H_A * DK_A))),
        pl.BlockSpec((rows, H_A * DK_A), lambda b, c: (hrow(b, c), OFF_KA // (H_A * DK_A))),
        pl.BlockSpec((rows, W_A), lambda b, c: (hrow(b, c), OFF_VA // W_A)),
        pl.BlockSpec((rows, W_A), lambda b, c: (hrow(b, c), OFF_GA // W_A)),
        pl.BlockSpec((nchunk * chunk, DK_A), lambda b, c: (c, 0)),
        pl.BlockSpec((nchunk * chunk, DK_A), lambda b, c: (c, 0)),
        pl.BlockSpec((H_A, chunk, chunk), lambda b, c: (0, 0, 0)),
        pl.BlockSpec((H_A, chunk, DV_A), lambda b, c: (0, 0, 0)),
        pl.BlockSpec((H_A, chunk, DK_A), lambda b, c: (0, 0, 0)),
        pl.BlockSpec((H_A, 1, DV_A), lambda b, c: (0, 0, 0)),
    ]
    args = [h_ab, h_ab, h_ab, h_ab, *tables]
    if has_state:
        in_specs.append(pl.BlockSpec((None, nbat, H_A, DK_A, DV_A), lambda b, c: (layer, b, 0, 0, 0)))
        args.append(state)
    a_args, a_specs, aliases = _alias_args(prev_outs, len(args))
    z, s_out = pl.pallas_call(
        functools.partial(_retention_kernel, chunk=chunk, nchunk=nchunk, nbat=nbat,
                          has_state=has_state, layer=layer, has_prev=prev_outs is not None),
        out_shape=(jax.ShapeDtypeStruct((batch * t, W_A), BF16),
                   jax.ShapeDtypeStruct((DEPTH, batch, H_A, DK_A, DV_A), F32)),
        grid=(batch // nbat, nsteps),
        in_specs=in_specs + a_specs,
        out_specs=(pl.BlockSpec((rows, W_A), lambda b, c: (row(b, c), 0)),
                   _state_out_spec((nbat, H_A, DK_A, DV_A), lambda b, c: (b, 0, 0, 0), layer,
                                   prev_outs is not None)),
        scratch_shapes=[pltpu.VMEM((nbat, H_A, DK_A, DV_A), F32)],
        input_output_aliases=aliases,
        compiler_params=_params("arbitrary", "arbitrary"),
        name=name,
    )(*args, *a_args)
    return z, (s_out,)


def _pool_kernel(*refs, nb, tb, pos0, has_state, layer, has_prev):
    refs = list(refs)
    u_ref, g_ref, halo_ref, wmap_ref, scale_ref = refs[:5]
    refs = refs[5:]
    if has_prev:
        refs.pop(0)
    z_ref, pout_ref, ext_scr, p_scr = refs
    ti = pl.program_id(1)
    for bi in range(nb):
        rows = pl.ds(bi * tb, tb)
        if has_state:
            ext_scr[0:1, :] = jnp.zeros((1, W_B), F32)
            ext_scr[1:POOL_HALO, :] = halo_ref[bi]
        else:
            ext_scr[0:POOL_HALO, :] = jnp.where(ti == 0, 0.0, halo_ref[...])
        ext_scr[POOL_HALO:POOL_HALO + tb, :] = u_ref[rows, :]
        for gi, w in enumerate(POOL_WINDOWS):
            cols = slice(gi * GW_B, (gi + 1) * GW_B)
            acc = ext_scr[:, cols]
            shift = 1
            while shift < w:
                acc = acc + pltpu.roll(acc, shift, 0)
                shift *= 2
            u = ext_scr[POOL_HALO:POOL_HALO + tb, cols]
            p_scr[rows, cols] = acc[POOL_HALO:POOL_HALO + tb] * (1.0 / w) - u
            if pos0 < POOL_BUF:
                head = min(tb, POOL_HALO)

                @pl.when(ti == 0)
                def _():
                    t_idx = pos0 + lax.broadcasted_iota(jnp.int32, (head, GW_B), 0)
                    cnt = jnp.minimum(t_idx + 1, w).astype(F32)
                    p_scr[pl.ds(bi * tb, head), cols] = (
                        acc[POOL_HALO:POOL_HALO + head] / cnt - u[:head])

        @pl.when(ti == pl.num_programs(1) - 1)
        def _():
            _state_store(pout_ref, (bi,), ext_scr[tb + 1:tb + POOL_HALO, :], layer, has_prev)

    for gi in range(N_POOL_GROUPS):
        cols = slice(gi * GW_B, (gi + 1) * GW_B)
        pm = jnp.dot(p_scr[:, cols].astype(BF16), wmap_ref[gi], preferred_element_type=F32)
        pm = pm * scale_ref[:, cols]
        z_ref[:, cols] = (pm * _silu(g_ref[:, cols])).astype(z_ref.dtype)


def _pool(h_ab, row0, batch, t, pos0, state, layer, prev_outs, wmap, scale, *, name):
    has_state = state is not None
    if t >= POOL_ROWS:
        nb, tb = 1, POOL_ROWS
    else:
        nb, tb = SMALL_T_POOL_BATCH, t
    nt = t // tb
    rows = nb * tb
    rblk = lambda b, i: b * nt + i
    hblk = lambda b, i: rblk(b, i) + _block_offset(row0, rows)
    in_specs = [pl.BlockSpec((rows, W_B), lambda b, i: (hblk(b, i), OFF_UB // W_B)),
                pl.BlockSpec((rows, W_B), lambda b, i: (hblk(b, i), OFF_GB // W_B))]
    args = [h_ab, h_ab]
    if has_state:
        assert nt == 1
        in_specs.append(pl.BlockSpec((None, nb, POOL_BUF, W_B), lambda b, i: (layer, b, 0, 0)))
        args.append(state)
    else:
        assert nb == 1
        per = tb // POOL_HALO
        in_specs.append(pl.BlockSpec(
            (POOL_HALO, W_B), lambda b, i: (jnp.maximum(hblk(b, i) * per - 1, 0), OFF_UB // W_B)))
        args.append(h_ab)
    in_specs += [pl.BlockSpec((N_POOL_GROUPS, GW_B, GW_B), lambda b, i: (0, 0, 0)),
                 pl.BlockSpec((1, W_B), lambda b, i: (0, 0))]
    args += [wmap, scale]
    a_args, a_specs, aliases = _alias_args(prev_outs, len(args))
    z, p_out = pl.pallas_call(
        functools.partial(_pool_kernel, nb=nb, tb=tb, pos0=pos0, has_state=has_state,
                          layer=layer, has_prev=prev_outs is not None),
        out_shape=(jax.ShapeDtypeStruct((batch * t, W_B), BF16),
                   jax.ShapeDtypeStruct((DEPTH, batch, POOL_BUF, W_B), F32)),
        grid=(batch // nb, nt),
        in_specs=in_specs + a_specs,
        out_specs=(pl.BlockSpec((rows, W_B), lambda b, i: (rblk(b, i), 0)),
                   _state_out_spec((nb, POOL_BUF, W_B), lambda b, i: (b, 0, 0), layer,
                                   prev_outs is not None)),
        scratch_shapes=[pltpu.VMEM((POOL_HALO + tb, W_B), F32),
                        pltpu.VMEM((rows, W_B), F32)],
        input_output_aliases=aliases,
        compiler_params=_params("arbitrary", "arbitrary"),
        name=name,
    )(*args, *a_args)
    return z, (p_out,)


def _t5_bucket_table(bq):
    span = WINDOW + bq
    dist = np.arange(bq)[:, None] + WINDOW - np.arange(span)[None, :]
    max_exact = NUM_BUCKETS // 2
    d = np.maximum(dist, 0).astype(np.float32)
    large = max_exact + (np.log(np.maximum(d, np.float32(1.0)) / np.float32(max_exact))
                         / np.float32(math.log(MAX_DISTANCE / max_exact))
                         * np.float32(NUM_BUCKETS - max_exact)).astype(np.int32)
    large = np.minimum(large, NUM_BUCKETS - 1)
    bucket = np.where(dist < max_exact, np.maximum(dist, 0), large)
    valid = (dist >= 0) & (dist < WINDOW)
    return np.where(valid, bucket, -1).astype(np.int32)


def _attention_kernel(*refs, bq, nbat, has_cache, mask_first, layer, has_prev):
    refs = list(refs)
    (sinks_ref, relb_ref, q_ref, kc_ref, vc_ref, kp_ref, vp_ref, g_ref, bucket_ref) = refs[:9]
    refs = refs[9:]
    if has_prev:
        del refs[:2]
    z_ref, wk_ref, wv_ref, bias_scr, qbd_scr, s_scr, p_scr, o_scr, l_scr = refs
    span = WINDOW + bq
    hrows = H_C * bq
    grows = G_C * bq
    chunk = min(hrows, SOFTMAX_ROWS)
    one_matmul = hrows <= SOFTMAX_ROWS
    fmin = jnp.finfo(F32).min
    n = pl.program_id(1)

    @pl.when((pl.program_id(0) == 0) & (n == 0))
    def _():
        bucket = bucket_ref[...]
        col = lax.broadcasted_iota(jnp.int32, (bq, span), 1)
        for h in range(H_C):
            acc = jnp.where(bucket < 0, fmin, 0.0)
            for b in range(NUM_BUCKETS):
                acc = jnp.where(bucket == b, relb_ref[b, h] * LOG2E, acc)
            acc = jnp.where(col == 0, sinks_ref[h] * LOG2E, acc)
            bias_scr[0, h * bq:(h + 1) * bq, :] = acc
            if mask_first:
                bias_scr[1, h * bq:(h + 1) * bq, :] = jnp.where((col < WINDOW) & (col > 0), fmin, acc)

    lane_kv = lax.broadcasted_iota(jnp.int32, (1, KVW), 1) // HD_C
    low_half = lax.broadcasted_iota(jnp.int32, (1, 2 * HD_C), 1) < HD_C
    row0 = lax.broadcasted_iota(jnp.int32, (WINDOW, KVW), 0) == 0
    col0_t = lax.broadcasted_iota(jnp.int32, (KVW, WINDOW), 1) == 0
    ones = jnp.ones((span, DENOM_W), BF16)
    kc = vc = None
    for bi in range(nbat):
        rows = slice(bi * bq, (bi + 1) * bq)
        base = bi * hrows
        if has_cache:
            kp, vp = kp_ref[bi], vp_ref[bi]
        elif bi == 0:
            kp, vp = kp_ref[...], vp_ref[...]
        else:
            kp, vp = kc, vc
        bsel = jnp.where(n == 0, 1, 0) if (mask_first and bi == 0) else 0
        kc = kc_ref[rows, :]
        vc = vc_ref[rows, :]
        q = q_ref[rows, :] * (HD_C ** -0.5 * LOG2E)
        for kk in range(KV_C):
            qbd_scr[base + kk * grows:base + (kk + 1) * grows, :] = jnp.concatenate(
                [jnp.where(lane_kv == kk, q[:, g * KVW:(g + 1) * KVW], 0.0) for g in range(G_C)],
                axis=0).astype(BF16)
        qbd = qbd_scr[base:base + hrows, :]
        nt_dims = (((1,), (1,)), ((), ()))
        if has_cache:
            kpt = jnp.where(col0_t, 0.0, kp).astype(BF16)
            s_scr[base:base + hrows, :] = jnp.concatenate(
                [jnp.dot(qbd, kpt, preferred_element_type=F32),
                 lax.dot_general(qbd, kc.astype(BF16), nt_dims, preferred_element_type=F32)], axis=1)
        else:
            kx = jnp.concatenate([jnp.where(row0, 0.0, kp), kc], axis=0).astype(BF16)
            vx = jnp.concatenate([jnp.where(row0, 0.0, vp), vc], axis=0).astype(BF16)
            s_scr[base:base + hrows, :] = lax.dot_general(qbd, kx, nt_dims,
                                                          preferred_element_type=F32)
        for c in range(hrows // chunk):
            rs = slice(base + c * chunk, base + (c + 1) * chunk)
            s = s_scr[rs, :] + bias_scr[bsel, c * chunk:(c + 1) * chunk, :]
            m = jnp.max(s, axis=-1, keepdims=True)
            p_scr[rs, :] = jnp.exp2(s - m).astype(BF16)
        if has_cache:
            assert one_matmul
            p = p_scr[base:base + hrows, :]
            vpt1 = jnp.concatenate([jnp.where(col0_t, 0.0, vp).astype(BF16),
                                    jnp.ones((DENOM_W, WINDOW), BF16)], axis=0)
            vc1 = jnp.concatenate([vc.astype(BF16), jnp.ones((bq, DENOM_W), BF16)], axis=1)
            o3 = (lax.dot_general(p[:, :WINDOW], vpt1, nt_dims, preferred_element_type=F32)
                  + jnp.dot(p[:, WINDOW:], vc1, preferred_element_type=F32))
        elif one_matmul:
            o3 = jnp.dot(p_scr[base:base + hrows, :], jnp.concatenate([vx, ones], axis=1),
                         preferred_element_type=F32)
        if one_matmul:
            for kk in range(KV_C):
                o_scr[base + kk * grows:base + (kk + 1) * grows, :] = (
                    o3[kk * grows:(kk + 1) * grows, (kk // 2) * 2 * HD_C:(kk // 2 + 1) * 2 * HD_C])
            l_scr[base:base + hrows, :] = o3[:, KVW:KVW + DENOM_W]
        else:
            for kk in range(KV_C):
                rk = slice(base + kk * grows, base + (kk + 1) * grows)
                slab = vx[:, (kk // 2) * 2 * HD_C:(kk // 2 + 1) * 2 * HD_C]
                vk = jnp.where(low_half if kk % 2 == 0 else ~low_half, slab, jnp.zeros_like(slab))
                ol = jnp.dot(p_scr[rk, :], jnp.concatenate([vk, ones], axis=1),
                             preferred_element_type=F32)
                o_scr[rk, :] = ol[:, :2 * HD_C]
                l_scr[rk, :] = ol[:, 2 * HD_C:]
        for g in range(G_C):
            halves = []
            for half in range(2):
                ra, rb = (slice(base + kk * grows + g * bq, base + kk * grows + (g + 1) * bq)
                          for kk in (2 * half, 2 * half + 1))
                num = jnp.where(low_half, o_scr[ra, :], o_scr[rb, :])
                den = jnp.where(low_half, l_scr[ra, :], l_scr[rb, :])
                halves.append(num * (1.0 / den))
            og = jnp.concatenate(halves, axis=1)
            cols = slice(g * KVW, (g + 1) * KVW)
            z_ref[rows, cols] = (og * _silu(g_ref[rows, cols])).astype(z_ref.dtype)

        if has_cache:
            keep = lax.broadcasted_iota(jnp.int32, (KVW, WINDOW), 1) < WINDOW - bq
            pad = jnp.zeros((WINDOW - bq, KVW), F32)
            for ref, old, new in ((wk_ref, kp, kc), (wv_ref, vp, vc)):
                new_t = jnp.concatenate([pad, new], axis=0).T
                _state_store(ref, (bi,), jnp.where(keep, pltpu.roll(old, WINDOW - bq, 1), new_t),
                             layer, has_prev)
        elif bi == nbat - 1:
            @pl.when(n == pl.num_programs(1) - 1)
            def _():
                _state_store(wk_ref, (0,), kc, layer, has_prev)
                _state_store(wv_ref, (0,), vc, layer, has_prev)


def _attention(h_att, row0, batch, t, pos0, cache_k, cache_v, layer, prev_outs, sinks, rel_bias, *,
               name):
    bq = min(ATT_BLOCK, t)
    nb = t // bq
    has_cache = cache_k is not None
    nbat = SMALL_T_ATT_BATCH if has_cache else ATT_BLOCKS_PER_STEP
    nsteps = 1 if has_cache else nb // nbat
    wb = nbat if has_cache else 1
    win_shape = (KVW, WINDOW) if has_cache else (WINDOW, KVW)
    span = WINDOW + bq
    assert pos0 == 0 or pos0 >= WINDOW
    mask_first = pos0 == 0
    rows = nbat * bq
    row = lambda b, n, *_: b * nsteps + n
    hrow = lambda b, n, *_: row(b, n) + _block_offset(row0, rows)
    in_specs = [
        pl.BlockSpec((rows, W_C), lambda b, n, *_: (hrow(b, n), OFF_QC // W_C)),
        pl.BlockSpec((rows, KVW), lambda b, n, *_: (hrow(b, n), OFF_KC // KVW)),
        pl.BlockSpec((rows, KVW), lambda b, n, *_: (hrow(b, n), OFF_VC // KVW)),
    ]
    args = [h_att, h_att, h_att]
    if has_cache:
        assert nb == 1
        in_specs += [pl.BlockSpec((None, nbat) + win_shape, lambda b, n, *_: (layer, b, 0, 0)),
                     pl.BlockSpec((None, nbat) + win_shape, lambda b, n, *_: (layer, b, 0, 0))]
        args += [cache_k, cache_v]
    else:
        assert bq == WINDOW and nb % nbat == 0
        prev = lambda b, n, *_: jnp.maximum(hrow(b, n) * nbat - 1, 0)
        in_specs += [pl.BlockSpec((WINDOW, KVW), lambda b, n, *_: (prev(b, n), OFF_KC // KVW)),
                     pl.BlockSpec((WINDOW, KVW), lambda b, n, *_: (prev(b, n), OFF_VC // KVW))]
        args += [h_att, h_att]
    in_specs += [
        pl.BlockSpec((rows, W_C), lambda b, n, *_: (hrow(b, n), OFF_GC // W_C)),
        pl.BlockSpec((bq, span), lambda b, n, *_: (0, 0)),
    ]
    args += [h_att, jnp.asarray(_t5_bucket_table(bq))]
    a_args, a_specs, aliases = _alias_args(prev_outs, 2 + len(args))
    srows = nbat * H_C * bq
    z, wk, wv = pl.pallas_call(
        functools.partial(_attention_kernel, bq=bq, nbat=nbat, has_cache=has_cache,
                          mask_first=mask_first, layer=layer, has_prev=prev_outs is not None),
        out_shape=(jax.ShapeDtypeStruct((batch * t, W_C), BF16),
                   jax.ShapeDtypeStruct((DEPTH, batch) + win_shape, F32),
                   jax.ShapeDtypeStruct((DEPTH, batch) + win_shape, F32)),
        grid_spec=pltpu.PrefetchScalarGridSpec(
            num_scalar_prefetch=2,
            grid=(batch // wb, nsteps),
            in_specs=in_specs + a_specs,
            out_specs=(pl.BlockSpec((rows, W_C), lambda b, n, *_: (row(b, n), 0)),
                       _state_out_spec((wb,) + win_shape, lambda b, n, *_: (b, 0, 0), layer,
                                       prev_outs is not None),
                       _state_out_spec((wb,) + win_shape, lambda b, n, *_: (b, 0, 0), layer,
                                       prev_outs is not None)),
            scratch_shapes=[pltpu.VMEM((2 if mask_first else 1, H_C * bq, span), F32),
                            pltpu.VMEM((srows, KVW), BF16),
                            pltpu.VMEM((srows, span), F32),
                            pltpu.VMEM((srows, span), BF16),
                            pltpu.VMEM((srows, 2 * HD_C), F32),
                            pltpu.VMEM((srows, DENOM_W), F32)]),
        input_output_aliases=aliases,
        compiler_params=_params("arbitrary", "arbitrary"),
        name=name,
    )(sinks, rel_bias, *args, *a_args)
    return z, (wk, wv)


def _out_kernel(*refs, n_x, lead_blocks, split_out, alpha):
    refs = list(refs)
    x_refs, refs = refs[:n_x], refs[n_x:]
    (za0, za1, zb0, zb1, zc0, zc1, ma_ref, mb_ref, mc_ref, wa_ref, wb_ref, wc_ref, wo_ref,
     lng_ref, lnb_ref) = refs[:15]
    if split_out:
        y0_ref, y1_ref, merged_scr, r_scr, y_scr = refs[15:]
    else:
        y_ref, yb_ref, merged_scr, r_scr = refs[15:]
    i = pl.program_id(0)

    @pl.when(i == 0)
    def _():
        r_scr[...] = jnp.zeros_like(r_scr)

    lng, lnb = lng_ref[...], lnb_ref[...]
    for rc in range(r_scr.shape[0] // LN_ROWS):
        rs = slice(rc * LN_ROWS, (rc + 1) * LN_ROWS)
        r = r_scr[rs, :]
        mu = jnp.mean(r, axis=-1, keepdims=True)
        d = r - mu
        var = jnp.mean(d * d, axis=-1, keepdims=True)
        out = d * lax.rsqrt(var + LN_EPS) * lng + lnb
        if split_out:
            y0_ref[rs, :] = out
            y_scr[rs, :] = out
        else:
            y_ref[rs, :] = out
            yb_ref[rs, :] = out.astype(BF16)
    if split_out:
        @pl.when(i <= lead_blocks)
        def _():
            y1_ref[...] = y_scr[...]

    in_first = i >= lead_blocks
    pick = lambda a, b: jnp.where(in_first, a[...], b[...])
    za, zb, zc = pick(za0, za1), pick(zb0, zb1), pick(zc0, zc1)
    x = x_refs[0][...] if n_x == 1 else pick(*x_refs)
    for c in range(D_MODEL // OUT_STAGE_TN):
        cs = slice(c * OUT_STAGE_TN, (c + 1) * OUT_STAGE_TN)
        merged = (_sigmoid(ma_ref[:, cs].astype(F32))
                  * jnp.dot(za, wa_ref[:, cs], preferred_element_type=F32))
        merged += (_sigmoid(mb_ref[:, cs].astype(F32))
                   * jnp.dot(zb, wb_ref[:, cs], preferred_element_type=F32))
        merged += (_sigmoid(mc_ref[:, cs].astype(F32))
                   * jnp.dot(zc, wc_ref[:, cs], preferred_element_type=F32))
        merged_scr[:, cs] = merged.astype(BF16)
    r_scr[...] = alpha * x + jnp.dot(merged_scr[...], wo_ref[...], preferred_element_type=F32)


def _out_stage(xs, zas, zbs, zcs, h_gate, wa, wb, wc, wo, lng, lnb, alpha, *, last, name):
    tm = OUT_STAGE_TM
    m0, m1 = zas[0].shape[0], zas[1].shape[0]
    nb0, nb1 = m0 // tm, m1 // tm
    nb = nb0 + nb1
    first = lambda i: jnp.clip(i - nb1, 0, nb0 - 1)
    second = lambda i: jnp.minimum(i, nb1 - 1)
    cur = lambda i: jnp.where(i < nb1, nb0 + i, jnp.minimum(i, nb - 1) - nb1)
    pair = lambda w: [pl.BlockSpec((tm, w), lambda i: (first(i), 0)),
                      pl.BlockSpec((tm, w), lambda i: (second(i), 0))]
    x_specs = pair(D_MODEL) if len(xs) == 2 else [pl.BlockSpec((tm, D_MODEL), lambda i: (cur(i), 0))]
    gate = lambda j: pl.BlockSpec((tm, D_MODEL), lambda i: (cur(i), j))
    const = lambda a: pl.BlockSpec(a.shape, lambda i: (0,) * a.ndim, pipeline_mode=pl.Buffered(1))
    scratch = [pltpu.VMEM((tm, D_MODEL), BF16), pltpu.VMEM((tm, D_MODEL), F32)]
    if last:
        out_shape = [jax.ShapeDtypeStruct((m0, D_MODEL), F32), jax.ShapeDtypeStruct((m1, D_MODEL), F32)]
        out_specs = [pl.BlockSpec((tm, D_MODEL), lambda i: (first(i - 1), 0)),
                     pl.BlockSpec((tm, D_MODEL), lambda i: (jnp.clip(i - 1, 0, nb1 - 1), 0))]
        scratch.append(pltpu.VMEM((tm, D_MODEL), F32))
    else:
        out_shape = [jax.ShapeDtypeStruct((m0 + m1, D_MODEL), F32),
                     jax.ShapeDtypeStruct((m0 + m1, D_MODEL), BF16)]
        out_specs = [pl.BlockSpec((tm, D_MODEL), lambda i: (cur(jnp.maximum(i - 1, 0)), 0))] * 2
    return pl.pallas_call(
        functools.partial(_out_kernel, n_x=len(xs), lead_blocks=nb1, split_out=last, alpha=alpha),
        out_shape=out_shape,
        grid=(nb + 1,),
        in_specs=x_specs + pair(W_A) + pair(W_B) + pair(W_C) + [gate(0), gate(1), gate(2),
                 const(wa), const(wb), const(wc), const(wo), const(lng), const(lnb)],
        out_specs=out_specs,
        scratch_shapes=scratch,
        compiler_params=_params("arbitrary"),
        name=name,
    )(*xs, *zas, *zbs, *zcs, h_gate, h_gate, h_gate, wa, wb, wc, wo, lng, lnb)


def _layer(groups, xs, xbs, layer, prevs, w, w_in, alpha):
    if xbs is None:
        h_att, xb_all = _in_proj(xs, w["w_att"], N_ATT, emit_xb=True, name=f"in_proj_att_{layer}")
        xbs = (xb_all,)
    else:
        h_att = _in_proj(xbs, w["w_att"], N_ATT, name=f"in_proj_att_{layer}")
    h_ab = _in_proj(xbs, w_in, N_AB, layer=layer, col0=0, name=f"in_proj_ab_{layer}")
    h_gate = _in_proj(xbs, w_in, N_GATE, layer=layer, col0=REF_GATE, out_dtype=BF16,
                      name=f"in_proj_gate_{layer}")
    zas, zbs, zcs, outs, row0 = [], [], [], [], 0
    for (batch, t, pos0, states, tag), prev in zip(groups, prevs):
        st_ret, st_k, st_v, st_pool = states if states is not None else (None,) * 4
        pv_ret, pv_win, pv_pool = prev if prev is not None else (None,) * 3
        za, o_ret = _retention(h_ab, row0, batch, t, pos0, st_ret, layer, pv_ret,
                               name=f"retention_{tag}{layer}")
        zb, o_pool = _pool(h_ab, row0, batch, t, pos0, st_pool, layer, pv_pool, w["w_pool_map"],
                           w["pool_scale"], name=f"pool_{tag}{layer}")
        zc, o_win = _attention(h_att, row0, batch, t, pos0, st_k, st_v, layer, pv_win, w["sinks"],
                               w["rel_bias"], name=f"attention_{tag}{layer}")
        zas.append(za)
        zbs.append(zb)
        zcs.append(zc)
        outs.append((o_ret, o_win, o_pool))
        row0 += batch * t
    last = layer + 1 == DEPTH
    res = _out_stage(xs, zas, zbs, zcs, h_gate, w["w_ret_o"], w["w_pool_o"], w["w_att_o"],
                     w["w_out"], w["ln_g"], w["ln_b"], alpha, last=last, name=f"out_stage_{layer}")
    if last:
        return tuple(res), None, outs
    y, yb = res
    return (y,), (yb,), outs


def _regroup_heads(w, axis):
    shp = w.shape
    w = w.reshape(shp[:axis] + (KV_C, G_C, HD_C) + shp[axis + 1:])
    w = jnp.swapaxes(w, axis, axis + 1)
    return w.reshape(shp)


def _regroup_matrix():
    new = np.arange(W_C)
    g, kv, d = new // KVW, (new // HD_C) % KV_C, new % HD_C
    p = np.zeros((W_C, W_C), np.float32)
    p[(kv * G_C + g) * HD_C + d, new] = 1.0
    return jnp.asarray(p, BF16)


def _w_att_kernel(q_ref, kv_ref, glo_ref, ghi_ref, perm_ref, o_ref):
    perm = perm_ref[...]
    regroup = lambda w: jnp.dot(w.astype(BF16), perm, preferred_element_type=F32).astype(BF16)
    o_ref[:, OFF_QC:OFF_QC + W_C] = regroup(q_ref[...])
    o_ref[:, OFF_GC:OFF_GC + W_C] = regroup(jnp.concatenate([glo_ref[...], ghi_ref[...]], axis=1))
    o_ref[:, OFF_KC:OFF_KC + 2 * KVW] = kv_ref[...].astype(BF16)


def _prep_w_att(w_in, l):
    half = W_C // 2
    assert REF_QC % W_C == 0 and REF_KC % half == 0 and REF_GC % half == 0
    assert REF_VC == REF_KC + KVW and OFF_VC == OFF_KC + KVW
    blk = lambda width, c0: pl.BlockSpec((None, W_PREP_ROWS, width), lambda i: (l, i, c0 // width))
    return pl.pallas_call(
        _w_att_kernel,
        out_shape=jax.ShapeDtypeStruct((D_MODEL, N_ATT), BF16),
        grid=(D_MODEL // W_PREP_ROWS,),
        in_specs=[blk(W_C, REF_QC), blk(half, REF_KC), blk(half, REF_GC), blk(half, REF_GC + half),
                  pl.BlockSpec((W_C, W_C), lambda i: (0, 0))],
        out_specs=pl.BlockSpec((W_PREP_ROWS, N_ATT), lambda i: (i, 0)),
        compiler_params=_params("parallel"),
        name=f"w_att_prep_{l}",
    )(w_in, w_in, w_in, w_in, _regroup_matrix())


def kernel(x_prompt, x_sample, state_ret, cache_win_k, cache_win_v, state_pool, w_in, w_ret_o,
           w_pool_map, pool_scale, w_pool_o, attn_sinks, w_att_o, w_out, ln_g, ln_b, rel_bias):
    alpha = (2.0 * DEPTH) ** 0.25
    bp, tp, _ = x_prompt.shape
    bs, ts, _ = x_sample.shape
    xp = x_prompt.reshape(bp * tp, D_MODEL)
    xs = x_sample.reshape(bs * ts, D_MODEL)
    to_t = lambda c: c.transpose(0, 1, 3, 4, 2).reshape(DEPTH, bs, KVW, WINDOW)
    from_t = lambda c: c.reshape(DEPTH, bs, KV_C, HD_C, WINDOW).transpose(0, 1, 4, 2, 3)
    ck, cv = to_t(cache_win_k), to_t(cache_win_v)
    groups = ((bp, tp, 0, None, "p"), (bs, ts, PAST_LEN, (state_ret, ck, cv, state_pool), "s"))
    xs, xbs, outs = (xp, xs), None, (None, None)
    for l in range(DEPTH):
        w = {
            "w_att": _prep_w_att(w_in, l),
            "w_ret_o": w_ret_o[l].astype(BF16),
            "w_pool_map": w_pool_map[l].astype(BF16),
            "pool_scale": pool_scale[l].reshape(1, W_B),
            "w_pool_o": w_pool_o[l].astype(BF16),
            "sinks": attn_sinks[l],
            "w_att_o": _regroup_heads(w_att_o[l], 0).astype(BF16),
            "w_out": w_out[l].astype(BF16),
            "ln_g": ln_g[l].reshape(1, D_MODEL),
            "ln_b": ln_b[l].reshape(1, D_MODEL),
            "rel_bias": rel_bias,
        }
        xs, xbs, outs = _layer(groups, xs, xbs, l, outs, w, w_in, alpha)
    ((ret_p,), (kp, vp), (pp,)), ((ret_s,), (ks, vs), (ps,)) = outs
    win = lambda a, b: a.reshape(DEPTH, b, WINDOW, KV_C, HD_C)
    return (xs[0].reshape(bp, tp, D_MODEL), xs[1].reshape(bs, ts, D_MODEL), ret_p, ret_s,
            win(kp, bp), from_t(ks), win(vp, bp), from_t(vs), pp, ps)
```

```python
import functools
import math

import jax
import jax.numpy as jnp
import numpy as np
from jax import lax
from jax.experimental import pallas as pl
from jax.experimental.pallas import tpu as pltpu

D_MODEL = 2048
DEPTH = 2
PAST_LEN = 8192
H_A, DK_A, DV_A = 4, 128, 256
W_A = H_A * DV_A
RET_CHUNK = 128
ROPE_BASE = 10000.0
N_POOL_GROUPS = 4
W_B = 1024
GW_B = W_B // N_POOL_GROUPS
POOL_WINDOWS = (2, 4, 8, 16)
POOL_BUF = 15
POOL_HALO = 16
H_C, KV_C, HD_C = 16, 4, 64
G_C = H_C // KV_C
W_C = H_C * HD_C
KVW = KV_C * HD_C
WINDOW = 128
ATT_BLOCK = 128
SOFTMAX_ROWS = 128
DENOM_W = 2 * HD_C
NUM_BUCKETS = 32
MAX_DISTANCE = 128
LN_EPS = 1e-5
RMS_EPS = 1e-6

REF_QC, REF_KC, REF_VC, REF_GC = 5120, 6144, 6400, 6656
N_AB = 5120
N_ATT = 2560
REF_GATE = 7680
N_GATE = 3 * D_MODEL
OFF_QA, OFF_KA, OFF_VA, OFF_GA, OFF_UB, OFF_GB = 0, 512, 1024, 2048, 3072, 4096
OFF_QC, OFF_GC, OFF_KC, OFF_VC = 0, 1024, 2048, 2304

F32 = jnp.float32
BF16 = jnp.bfloat16
VMEM_LIMIT = 56 * 1024 * 1024

IN_PROJ_TM = 1024
IN_PROJ_TM_F32_X = 512
IN_PROJ_TN = {N_AB: 1280, N_ATT: 2560, N_GATE: 1536}
OUT_STAGE_TM = 256
OUT_STAGE_TN = 512
LN_ROWS = 16
POOL_ROWS = 1024
SMALL_T_POOL_BATCH = 16
W_PREP_ROWS = 512
ATT_BLOCKS_PER_STEP = 4
SMALL_T_ATT_BATCH = 8
RET_ROWS = 512
SMALL_T_RET_BATCH = 8


LOG2E = 1.0 / math.log(2.0)


def _sigmoid(x):
    return 1.0 / (1.0 + jnp.exp2(x * (-LOG2E)))


def _silu(x):
    return x * _sigmoid(x)


def _params(*sem):
    return pltpu.CompilerParams(dimension_semantics=sem, vmem_limit_bytes=VMEM_LIMIT)


def _alias_args(prev_outs, n_in):
    if prev_outs is None:
        return [], [], {}
    specs = [pl.BlockSpec(memory_space=pl.ANY) for _ in prev_outs]
    return list(prev_outs), specs, {n_in + i: 1 + i for i in range(len(prev_outs))}


def _block_offset(row0, rows):
    assert row0 % rows == 0
    return row0 // rows


def _state_out_spec(block, index_map, layer, has_prev):
    if has_prev:
        return pl.BlockSpec((None,) + block, lambda *a: (layer,) + index_map(*a))
    return pl.BlockSpec((DEPTH,) + block, lambda *a: (0,) + index_map(*a))


def _state_store(ref, idx, value, layer, has_prev):
    if has_prev:
        ref[idx] = value
    else:
        for l in range(DEPTH):
            ref[(l,) + idx] = value if l == layer else jnp.zeros_like(value)


def _matmul_kernel(*refs, n_x, first_blocks, cast_w, emit_xb):
    refs = list(refs)
    x_refs, (w_ref, o_ref), rest = refs[:n_x], refs[n_x:n_x + 2], refs[n_x + 2:]
    xb_ref = rest.pop(0) if emit_xb else None
    if cast_w:
        wb_scr, = rest

        @pl.when(pl.program_id(1) == 0)
        def _():
            wb_scr[...] = w_ref[...].astype(BF16)
        w_ref = wb_scr

    x = x_refs[0][...]
    if n_x == 2:
        x = jnp.where(pl.program_id(1) < first_blocks, x, x_refs[1][...])
    x = x.astype(BF16)
    if emit_xb:
        xb_ref[...] = x
    o_ref[...] = jnp.dot(x, w_ref[...], preferred_element_type=F32).astype(o_ref.dtype)


def _in_proj(xs, w, n, *, layer=None, col0=0, emit_xb=False, out_dtype=F32, name):
    assert len(xs) in (1, 2)
    k = xs[0].shape[1]
    m = sum(x.shape[0] for x in xs)
    tm, tn = (IN_PROJ_TM_F32_X if xs[0].dtype == F32 else IN_PROJ_TM), IN_PROJ_TN[n]
    first = xs[0].shape[0] // tm
    assert all(x.shape[0] % tm == 0 for x in xs)
    cast_w = w.dtype == F32
    if cast_w:
        w_spec = pl.BlockSpec((None, k, tn), lambda j, i: (layer, 0, col0 // tn + j))
    else:
        w_spec = pl.BlockSpec((k, tn), lambda j, i: (0, j),
                              pipeline_mode=pl.Buffered(1) if n == tn else None)
    if len(xs) == 1:
        x_specs = [pl.BlockSpec((tm, k), lambda j, i: (i, 0))]
    else:
        last = xs[1].shape[0] // tm - 1
        x_specs = [pl.BlockSpec((tm, k), lambda j, i: (jnp.minimum(i, first - 1), 0)),
                   pl.BlockSpec((tm, k), lambda j, i: (jnp.clip(i - first, 0, last), 0),
                                pipeline_mode=None if last else pl.Buffered(1))]
    assert not emit_xb or n == tn
    out_shape = [jax.ShapeDtypeStruct((m, n), out_dtype)]
    out_specs = [pl.BlockSpec((tm, tn), lambda j, i: (i, j))]
    if emit_xb:
        out_shape.append(jax.ShapeDtypeStruct((m, k), BF16))
        out_specs.append(pl.BlockSpec((tm, k), lambda j, i: (i, 0)))
    outs = pl.pallas_call(
        functools.partial(_matmul_kernel, n_x=len(xs), first_blocks=first, cast_w=cast_w,
                          emit_xb=emit_xb),
        out_shape=out_shape,
        grid=(n // tn, m // tm),
        in_specs=x_specs + [w_spec],
        out_specs=out_specs,
        scratch_shapes=[pltpu.VMEM((k, tn), BF16)] if cast_w else [],
        compiler_params=_params("arbitrary", "arbitrary"),
        name=name,
    )(*xs, w)
    return outs if emit_xb else outs[0]


def _retention_kernel(*refs, chunk, nchunk, nbat, has_state, layer, has_prev, finalize=True):
    refs = list(refs)
    (q_ref, k_ref, v_ref, g_ref, cos_ref, sin_ref, dmask_ref, qdec_ref, kdec_ref, cdec_ref) = refs[:10]
    refs = refs[10:]
    s0_ref = refs.pop(0) if has_state else None
    if has_prev:
        refs.pop(0)
    z_ref, sout_ref, s_scr = refs
    c = pl.program_id(1)

    @pl.when(c == 0)
    def _():
        if has_state:
            s_scr[...] = s0_ref[...]
        else:
            s_scr[...] = jnp.zeros_like(s_scr)

    for bi in range(nbat):
        for hd in range(H_A):
            s = s_scr[bi, hd]
            for ci in range(nchunk):
                rows = slice((bi * nchunk + ci) * chunk, (bi * nchunk + ci + 1) * chunk)
                trow = slice(ci * chunk, (ci + 1) * chunk)
                cos = cos_ref[trow, :]
                sin = sin_ref[trow, :]
                q = q_ref[rows, hd * DK_A:(hd + 1) * DK_A]
                k = k_ref[rows, hd * DK_A:(hd + 1) * DK_A]
                v = v_ref[rows, hd * DV_A:(hd + 1) * DV_A].astype(BF16)
                qr = q * cos + pltpu.roll(q, DK_A // 2, 1) * sin
                kr = (k * cos + pltpu.roll(k, DK_A // 2, 1) * sin) * (DK_A ** -0.5)
                qb = qr.astype(BF16)
                sc = lax.dot_general(qb, kr.astype(BF16), (((1,), (1,)), ((), ())),
                                     preferred_element_type=F32) * dmask_ref[hd]
                o = (jnp.dot(sc.astype(BF16), v, preferred_element_type=F32)
                     + jnp.dot(qb, s.astype(BF16), preferred_element_type=F32) * qdec_ref[hd])
                kd = (kr * kdec_ref[hd]).astype(BF16)
                s = s * cdec_ref[hd] + lax.dot_general(
                    kd, v, (((0,), (0,)), ((), ())), preferred_element_type=F32)
                o = o * lax.rsqrt(jnp.mean(o * o, axis=-1, keepdims=True) + RMS_EPS)
                g = g_ref[rows, hd * DV_A:(hd + 1) * DV_A]
                z_ref[rows, hd * DV_A:(hd + 1) * DV_A] = (o * _silu(g)).astype(z_ref.dtype)
            s_scr[bi, hd] = s

    if finalize:
        @pl.when(c == pl.num_programs(1) - 1)
        def _():
            _state_store(sout_ref, (Ellipsis,), s_scr[...], layer, has_prev)


def _retention_tables(pos0, t, chunk):
    half = DK_A // 2
    inv = ROPE_BASE ** (-jnp.arange(half, dtype=F32) / half)
    pos = pos0 + jnp.arange(t, dtype=F32)
    ang = pos[:, None] * inv[None, :]
    cos, sin = jnp.cos(ang), jnp.sin(ang)
    cos2 = jnp.concatenate([cos, cos], -1)
    sin2 = jnp.concatenate([-sin, sin], -1)
    lg = jnp.log1p(-jnp.exp2(-5.0 - jnp.arange(H_A, dtype=F32)))
    idx = jnp.arange(chunk, dtype=F32)
    diff = idx[:, None] - idx[None, :]
    dmask = jnp.where(diff >= 0, jnp.exp(lg[:, None, None] * jnp.maximum(diff, 0.0)), 0.0)
    qdec = jnp.exp(lg[:, None] * (idx[None, :] + 1.0))
    kdec = jnp.exp(lg[:, None] * (chunk - 1.0 - idx[None, :]))
    cdec = jnp.exp(lg * chunk)
    qdec = jnp.broadcast_to(qdec[:, :, None], (H_A, chunk, DV_A))
    kdec = jnp.broadcast_to(kdec[:, :, None], (H_A, chunk, DK_A))
    cdec = jnp.broadcast_to(cdec[:, None, None], (H_A, 1, DV_A))
    return cos2, sin2, dmask, qdec, kdec, cdec


def _retention(h_ab, row0, batch, t, pos0, state, layer, prev_outs, *, name):
    chunk = min(RET_CHUNK, t)
    if t > chunk:
        nbat, nchunk = 1, RET_ROWS // chunk
    else:
        nbat, nchunk = SMALL_T_RET_BATCH, 1
    rows = nbat * nchunk * chunk
    nsteps = t // (nchunk * chunk)
    has_state = state is not None
    tables = _retention_tables(pos0, t, chunk)
    row = lambda b, c: b * nsteps + c
    hrow = lambda b, c: row(b, c) + _block_offset(row0, rows)
    in_specs = [
        pl.BlockSpec((rows, H_A * DK_A), lambda b, c: (hrow(b, c), OFF_QA // (H_A * DK_A))),
        pl.BlockSpec((rows, H_A * DK_A), lambda b, c: (hrow(b, c), OFF_KA // (H_A * DK_A))),
        pl.BlockSpec((rows, W_A), lambda b, c: (hrow(b, c), OFF_VA // W_A)),
        pl.BlockSpec((rows, W_A), lambda b, c: (hrow(b, c), OFF_GA // W_A)),
        pl.BlockSpec((nchunk * chunk, DK_A), lambda b, c: (c, 0)),
        pl.BlockSpec((nchunk * chunk, DK_A), lambda b, c: (c, 0)),
        pl.BlockSpec((H_A, chunk, chunk), lambda b, c: (0, 0, 0)),
        pl.BlockSpec((H_A, chunk, DV_A), lambda b, c: (0, 0, 0)),
        pl.BlockSpec((H_A, chunk, DK_A), lambda b, c: (0, 0, 0)),
        pl.BlockSpec((H_A, 1, DV_A), lambda b, c: (0, 0, 0)),
    ]
    args = [h_ab, h_ab, h_ab, h_ab, *tables]
    if has_state:
        in_specs.append(pl.BlockSpec((None, nbat, H_A, DK_A, DV_A), lambda b, c: (layer, b, 0, 0, 0)))
        args.append(state)
    a_args, a_specs, aliases = _alias_args(prev_outs, len(args))
    z, s_out = pl.pallas_call(
        functools.partial(_retention_kernel, chunk=chunk, nchunk=nchunk, nbat=nbat,
                          has_state=has_state, layer=layer, has_prev=prev_outs is not None),
        out_shape=(jax.ShapeDtypeStruct((batch * t, W_A), BF16),
                   jax.ShapeDtypeStruct((DEPTH, batch, H_A, DK_A, DV_A), F32)),
        grid=(batch // nbat, nsteps),
        in_specs=in_specs + a_specs,
        out_specs=(pl.BlockSpec((rows, W_A), lambda b, c: (row(b, c), 0)),
                   _state_out_spec((nbat, H_A, DK_A, DV_A), lambda b, c: (b, 0, 0, 0), layer,
                                   prev_outs is not None)),
        scratch_shapes=[pltpu.VMEM((nbat, H_A, DK_A, DV_A), F32)],
        input_output_aliases=aliases,
        compiler_params=_params("arbitrary", "arbitrary"),
        name=name,
    )(*args, *a_args)
    return z, (s_out,)


def _pool_kernel(*refs, nb, tb, pos0, has_state, layer, has_prev, finalize=True):
    refs = list(refs)
    u_ref, g_ref, halo_ref, wmap_ref, scale_ref = refs[:5]
    refs = refs[5:]
    if has_prev:
        refs.pop(0)
    z_ref, pout_ref, ext_scr, p_scr = refs
    ti = pl.program_id(1)
    for bi in range(nb):
        rows = pl.ds(bi * tb, tb)
        if has_state:
            ext_scr[0:1, :] = jnp.zeros((1, W_B), F32)
            ext_scr[1:POOL_HALO, :] = halo_ref[bi]
        else:
            ext_scr[0:POOL_HALO, :] = jnp.where(ti == 0, 0.0, halo_ref[...])
        ext_scr[POOL_HALO:POOL_HALO + tb, :] = u_ref[rows, :]
        for gi, w in enumerate(POOL_WINDOWS):
            cols = slice(gi * GW_B, (gi + 1) * GW_B)
            acc = ext_scr[:, cols]
            shift = 1
            while shift < w:
                acc = acc + pltpu.roll(acc, shift, 0)
                shift *= 2
            u = ext_scr[POOL_HALO:POOL_HALO + tb, cols]
            p_scr[rows, cols] = acc[POOL_HALO:POOL_HALO + tb] * (1.0 / w) - u
            if pos0 < POOL_BUF:
                head = min(tb, POOL_HALO)
                t_idx = pos0 + ti * tb + lax.broadcasted_iota(jnp.int32, (head, GW_B), 0)
                cnt = jnp.minimum(t_idx + 1, w).astype(F32)
                p_scr[pl.ds(bi * tb, head), cols] = acc[POOL_HALO:POOL_HALO + head] / cnt - u[:head]

        if finalize:
            @pl.when(ti == pl.num_programs(1) - 1)
            def _():
                _state_store(pout_ref, (bi,), ext_scr[tb + 1:tb + POOL_HALO, :], layer, has_prev)

    for gi in range(N_POOL_GROUPS):
        cols = slice(gi * GW_B, (gi + 1) * GW_B)
        pm = jnp.dot(p_scr[:, cols].astype(BF16), wmap_ref[gi], preferred_element_type=F32)
        pm = pm * scale_ref[:, cols]
        z_ref[:, cols] = (pm * _silu(g_ref[:, cols])).astype(z_ref.dtype)


def _pool(h_ab, row0, batch, t, pos0, state, layer, prev_outs, wmap, scale, *, name):
    has_state = state is not None
    if t >= POOL_ROWS:
        nb, tb = 1, POOL_ROWS
    else:
        nb, tb = SMALL_T_POOL_BATCH, t
    nt = t // tb
    rows = nb * tb
    rblk = lambda b, i: b * nt + i
    hblk = lambda b, i: rblk(b, i) + _block_offset(row0, rows)
    in_specs = [pl.BlockSpec((rows, W_B), lambda b, i: (hblk(b, i), OFF_UB // W_B)),
                pl.BlockSpec((rows, W_B), lambda b, i: (hblk(b, i), OFF_GB // W_B))]
    args = [h_ab, h_ab]
    if has_state:
        assert nt == 1
        in_specs.append(pl.BlockSpec((None, nb, POOL_BUF, W_B), lambda b, i: (layer, b, 0, 0)))
        args.append(state)
    else:
        assert nb == 1
        per = tb // POOL_HALO
        in_specs.append(pl.BlockSpec(
            (POOL_HALO, W_B), lambda b, i: (jnp.maximum(hblk(b, i) * per - 1, 0), OFF_UB // W_B)))
        args.append(h_ab)
    in_specs += [pl.BlockSpec((N_POOL_GROUPS, GW_B, GW_B), lambda b, i: (0, 0, 0)),
                 pl.BlockSpec((1, W_B), lambda b, i: (0, 0))]
    args += [wmap, scale]
    a_args, a_specs, aliases = _alias_args(prev_outs, len(args))
    z, p_out = pl.pallas_call(
        functools.partial(_pool_kernel, nb=nb, tb=tb, pos0=pos0, has_state=has_state,
                          layer=layer, has_prev=prev_outs is not None),
        out_shape=(jax.ShapeDtypeStruct((batch * t, W_B), BF16),
                   jax.ShapeDtypeStruct((DEPTH, batch, POOL_BUF, W_B), F32)),
        grid=(batch // nb, nt),
        in_specs=in_specs + a_specs,
        out_specs=(pl.BlockSpec((rows, W_B), lambda b, i: (rblk(b, i), 0)),
                   _state_out_spec((nb, POOL_BUF, W_B), lambda b, i: (b, 0, 0), layer,
                                   prev_outs is not None)),
        scratch_shapes=[pltpu.VMEM((POOL_HALO + tb, W_B), F32),
                        pltpu.VMEM((rows, W_B), F32)],
        input_output_aliases=aliases,
        compiler_params=_params("arbitrary", "arbitrary"),
        name=name,
    )(*args, *a_args)
    return z, (p_out,)


N_RET_IN, N_POOL_IN = 10, 5


def _ret_pool_kernel(*refs, chunk, nchunk, tb, pos0, layer, has_prev):
    refs = list(refs)
    ret_in, pool_in = refs[:N_RET_IN], refs[N_RET_IN:N_RET_IN + N_POOL_IN]
    refs = refs[N_RET_IN + N_POOL_IN:]
    prev = [[refs.pop(0)], [refs.pop(0)]] if has_prev else [[], []]
    za_ref, sout_ref, zb_ref, pout_ref, s_scr, ext_scr, p_scr = refs
    _retention_kernel(*ret_in, *prev[0], za_ref, sout_ref, s_scr, chunk=chunk, nchunk=nchunk,
                      nbat=1, has_state=False, layer=layer, has_prev=has_prev, finalize=False)
    _pool_kernel(*pool_in, *prev[1], zb_ref, pout_ref, ext_scr, p_scr, nb=1, tb=tb, pos0=pos0,
                 has_state=False, layer=layer, has_prev=has_prev, finalize=False)

    @pl.when(pl.program_id(1) == pl.num_programs(1) - 1)
    def _():
        _state_store(sout_ref, (Ellipsis,), s_scr[...], layer, has_prev)
        _state_store(pout_ref, (0,), ext_scr[tb + 1:tb + POOL_HALO, :], layer, has_prev)


def _ret_pool(h_ab, row0, batch, t, pos0, layer, prev_ret, prev_pool, wmap, scale, *, name):
    chunk, rows = RET_CHUNK, RET_ROWS
    nchunk, nsteps = rows // chunk, t // rows
    has_prev = prev_ret is not None
    tables = _retention_tables(pos0, t, chunk)
    row = lambda b, c: b * nsteps + c
    hrow = lambda b, c: row(b, c) + _block_offset(row0, rows)
    col = lambda width, off: pl.BlockSpec((rows, width), lambda b, c: (hrow(b, c), off // width))
    full = lambda shape: pl.BlockSpec(shape, lambda b, c: (0,) * len(shape))
    in_specs = [
        col(H_A * DK_A, OFF_QA), col(H_A * DK_A, OFF_KA), col(W_A, OFF_VA), col(W_A, OFF_GA),
        pl.BlockSpec((rows, DK_A), lambda b, c: (c, 0)), pl.BlockSpec((rows, DK_A), lambda b, c: (c, 0)),
        full((H_A, chunk, chunk)), full((H_A, chunk, DV_A)), full((H_A, chunk, DK_A)),
        full((H_A, 1, DV_A)),
        col(W_B, OFF_UB), col(W_B, OFF_GB),
        pl.BlockSpec((POOL_HALO, W_B), lambda b, c: (
            jnp.maximum(hrow(b, c) * (rows // POOL_HALO) - 1, 0), OFF_UB // W_B)),
        full((N_POOL_GROUPS, GW_B, GW_B)), full((1, W_B)),
    ]
    args = [h_ab] * 4 + list(tables) + [h_ab, h_ab, h_ab, wmap, scale]
    assert len(args) == N_RET_IN + N_POOL_IN
    aliases, a_args, a_specs = {}, [], []
    if has_prev:
        a_args = [prev_ret[0], prev_pool[0]]
        a_specs = [pl.BlockSpec(memory_space=pl.ANY)] * 2
        aliases = {len(args): 1, len(args) + 1: 3}
    za, s_out, zb, p_out = pl.pallas_call(
        functools.partial(_ret_pool_kernel, chunk=chunk, nchunk=nchunk, tb=rows, pos0=pos0,
                          layer=layer, has_prev=has_prev),
        out_shape=(jax.ShapeDtypeStruct((batch * t, W_A), BF16),
                   jax.ShapeDtypeStruct((DEPTH, batch, H_A, DK_A, DV_A), F32),
                   jax.ShapeDtypeStruct((batch * t, W_B), BF16),
                   jax.ShapeDtypeStruct((DEPTH, batch, POOL_BUF, W_B), F32)),
        grid=(batch, nsteps),
        in_specs=in_specs + a_specs,
        out_specs=(pl.BlockSpec((rows, W_A), lambda b, c: (row(b, c), 0)),
                   _state_out_spec((1, H_A, DK_A, DV_A), lambda b, c: (b, 0, 0, 0), layer, has_prev),
                   pl.BlockSpec((rows, W_B), lambda b, c: (row(b, c), 0)),
                   _state_out_spec((1, POOL_BUF, W_B), lambda b, c: (b, 0, 0), layer, has_prev)),
        scratch_shapes=[pltpu.VMEM((1, H_A, DK_A, DV_A), F32),
                        pltpu.VMEM((POOL_HALO + rows, W_B), F32),
                        pltpu.VMEM((rows, W_B), F32)],
        input_output_aliases=aliases,
        compiler_params=_params("arbitrary", "arbitrary"),
        name=name,
    )(*args, *a_args)
    return za, (s_out,), zb, (p_out,)


def _t5_bucket_table(bq):
    span = WINDOW + bq
    dist = np.arange(bq)[:, None] + WINDOW - np.arange(span)[None, :]
    max_exact = NUM_BUCKETS // 2
    d = np.maximum(dist, 0).astype(np.float32)
    large = max_exact + (np.log(np.maximum(d, np.float32(1.0)) / np.float32(max_exact))
                         / np.float32(math.log(MAX_DISTANCE / max_exact))
                         * np.float32(NUM_BUCKETS - max_exact)).astype(np.int32)
    large = np.minimum(large, NUM_BUCKETS - 1)
    bucket = np.where(dist < max_exact, np.maximum(dist, 0), large)
    valid = (dist >= 0) & (dist < WINDOW)
    return np.where(valid, bucket, -1).astype(np.int32)


def _attention_kernel(*refs, bq, nbat, has_cache, mask_first, layer, has_prev):
    refs = list(refs)
    (sinks_ref, relb_ref, q_ref, kc_ref, vc_ref, kp_ref, vp_ref, g_ref, bucket_ref) = refs[:9]
    refs = refs[9:]
    if has_prev:
        del refs[:2]
    z_ref, wk_ref, wv_ref, bias_scr, qbd_scr, s_scr, p_scr, o_scr, l_scr = refs
    span = WINDOW + bq
    hrows = H_C * bq
    grows = G_C * bq
    chunk = min(hrows, SOFTMAX_ROWS)
    one_matmul = hrows <= SOFTMAX_ROWS
    fmin = jnp.finfo(F32).min
    n = pl.program_id(1)

    @pl.when((pl.program_id(0) == 0) & (n == 0))
    def _():
        bucket = bucket_ref[...]
        col = lax.broadcasted_iota(jnp.int32, (bq, span), 1)
        for h in range(H_C):
            acc = jnp.where(bucket < 0, fmin, 0.0)
            for b in range(NUM_BUCKETS):
                acc = jnp.where(bucket == b, relb_ref[b, h] * LOG2E, acc)
            acc = jnp.where(col == 0, sinks_ref[h] * LOG2E, acc)
            bias_scr[0, h * bq:(h + 1) * bq, :] = acc
            if mask_first:
                bias_scr[1, h * bq:(h + 1) * bq, :] = jnp.where((col < WINDOW) & (col > 0), fmin, acc)

    lane_kv = lax.broadcasted_iota(jnp.int32, (1, KVW), 1) // HD_C
    low_half = lax.broadcasted_iota(jnp.int32, (1, 2 * HD_C), 1) < HD_C
    row0 = lax.broadcasted_iota(jnp.int32, (WINDOW, KVW), 0) == 0
    col0_t = lax.broadcasted_iota(jnp.int32, (KVW, WINDOW), 1) == 0
    ones = jnp.ones((span, DENOM_W), BF16)
    kc = vc = None
    for bi in range(nbat):
        rows = slice(bi * bq, (bi + 1) * bq)
        base = bi * hrows
        if has_cache:
            kp, vp = kp_ref[bi], vp_ref[bi]
        elif bi == 0:
            kp, vp = kp_ref[...], vp_ref[...]
        else:
            kp, vp = kc, vc
        bsel = jnp.where(n == 0, 1, 0) if (mask_first and bi == 0) else 0
        kc = kc_ref[rows, :]
        vc = vc_ref[rows, :]
        q = q_ref[rows, :] * (HD_C ** -0.5 * LOG2E)
        for kk in range(KV_C):
            qbd_scr[base + kk * grows:base + (kk + 1) * grows, :] = jnp.concatenate(
                [jnp.where(lane_kv == kk, q[:, g * KVW:(g + 1) * KVW], 0.0) for g in range(G_C)],
                axis=0).astype(BF16)
        qbd = qbd_scr[base:base + hrows, :]
        nt_dims = (((1,), (1,)), ((), ()))
        if has_cache:
            kpt = jnp.where(col0_t, 0.0, kp).astype(BF16)
            s_scr[base:base + hrows, :] = jnp.concatenate(
                [jnp.dot(qbd, kpt, preferred_element_type=F32),
                 lax.dot_general(qbd, kc.astype(BF16), nt_dims, preferred_element_type=F32)], axis=1)
        else:
            kx = jnp.concatenate([jnp.where(row0, 0.0, kp), kc], axis=0).astype(BF16)
            vx = jnp.concatenate([jnp.where(row0, 0.0, vp), vc], axis=0).astype(BF16)
            s_scr[base:base + hrows, :] = lax.dot_general(qbd, kx, nt_dims,
                                                          preferred_element_type=F32)
        for c in range(hrows // chunk):
            rs = slice(base + c * chunk, base + (c + 1) * chunk)
            s = s_scr[rs, :] + bias_scr[bsel, c * chunk:(c + 1) * chunk, :]
            m = jnp.max(s, axis=-1, keepdims=True)
            p_scr[rs, :] = jnp.exp2(s - m).astype(BF16)
        if has_cache:
            assert one_matmul
            p = p_scr[base:base + hrows, :]
            vpt1 = jnp.concatenate([jnp.where(col0_t, 0.0, vp).astype(BF16),
                                    jnp.ones((DENOM_W, WINDOW), BF16)], axis=0)
            vc1 = jnp.concatenate([vc.astype(BF16), jnp.ones((bq, DENOM_W), BF16)], axis=1)
            o3 = (lax.dot_general(p[:, :WINDOW], vpt1, nt_dims, preferred_element_type=F32)
                  + jnp.dot(p[:, WINDOW:], vc1, preferred_element_type=F32))
        elif one_matmul:
            o3 = jnp.dot(p_scr[base:base + hrows, :], jnp.concatenate([vx, ones], axis=1),
                         preferred_element_type=F32)
        if one_matmul:
            for kk in range(KV_C):
                o_scr[base + kk * grows:base + (kk + 1) * grows, :] = (
                    o3[kk * grows:(kk + 1) * grows, (kk // 2) * 2 * HD_C:(kk // 2 + 1) * 2 * HD_C])
            l_scr[base:base + hrows, :] = o3[:, KVW:KVW + DENOM_W]
        else:
            for kk in range(KV_C):
                rk = slice(base + kk * grows, base + (kk + 1) * grows)
                slab = vx[:, (kk // 2) * 2 * HD_C:(kk // 2 + 1) * 2 * HD_C]
                vk = jnp.where(low_half if kk % 2 == 0 else ~low_half, slab, jnp.zeros_like(slab))
                ol = jnp.dot(p_scr[rk, :], jnp.concatenate([vk, ones], axis=1),
                             preferred_element_type=F32)
                o_scr[rk, :] = ol[:, :2 * HD_C]
                l_scr[rk, :] = ol[:, 2 * HD_C:]
        for g in range(G_C):
            halves = []
            for half in range(2):
                ra, rb = (slice(base + kk * grows + g * bq, base + kk * grows + (g + 1) * bq)
                          for kk in (2 * half, 2 * half + 1))
                num = jnp.where(low_half, o_scr[ra, :], o_scr[rb, :])
                den = jnp.where(low_half, l_scr[ra, :], l_scr[rb, :])
                halves.append(num * (1.0 / den))
            og = jnp.concatenate(halves, axis=1)
            cols = slice(g * KVW, (g + 1) * KVW)
            z_ref[rows, cols] = (og * _silu(g_ref[rows, cols])).astype(z_ref.dtype)

        if has_cache:
            keep = lax.broadcasted_iota(jnp.int32, (KVW, WINDOW), 1) < WINDOW - bq
            pad = jnp.zeros((WINDOW - bq, KVW), F32)
            for ref, old, new in ((wk_ref, kp, kc), (wv_ref, vp, vc)):
                new_t = jnp.concatenate([pad, new], axis=0).T
                _state_store(ref, (bi,), jnp.where(keep, pltpu.roll(old, WINDOW - bq, 1), new_t),
                             layer, has_prev)
        elif bi == nbat - 1:
            @pl.when(n == pl.num_programs(1) - 1)
            def _():
                _state_store(wk_ref, (0,), kc, layer, has_prev)
                _state_store(wv_ref, (0,), vc, layer, has_prev)


def _attention(h_att, row0, batch, t, pos0, cache_k, cache_v, layer, prev_outs, sinks, rel_bias, *,
               name):
    bq = min(ATT_BLOCK, t)
    nb = t // bq
    has_cache = cache_k is not None
    nbat = SMALL_T_ATT_BATCH if has_cache else ATT_BLOCKS_PER_STEP
    nsteps = 1 if has_cache else nb // nbat
    wb = nbat if has_cache else 1
    win_shape = (KVW, WINDOW) if has_cache else (WINDOW, KVW)
    span = WINDOW + bq
    assert pos0 == 0 or pos0 >= WINDOW
    mask_first = pos0 == 0
    rows = nbat * bq
    row = lambda b, n, *_: b * nsteps + n
    hrow = lambda b, n, *_: row(b, n) + _block_offset(row0, rows)
    in_specs = [
        pl.BlockSpec((rows, W_C), lambda b, n, *_: (hrow(b, n), OFF_QC // W_C)),
        pl.BlockSpec((rows, KVW), lambda b, n, *_: (hrow(b, n), OFF_KC // KVW)),
        pl.BlockSpec((rows, KVW), lambda b, n, *_: (hrow(b, n), OFF_VC // KVW)),
    ]
    args = [h_att, h_att, h_att]
    if has_cache:
        assert nb == 1
        in_specs += [pl.BlockSpec((None, nbat) + win_shape, lambda b, n, *_: (layer, b, 0, 0)),
                     pl.BlockSpec((None, nbat) + win_shape, lambda b, n, *_: (layer, b, 0, 0))]
        args += [cache_k, cache_v]
    else:
        assert bq == WINDOW and nb % nbat == 0
        prev = lambda b, n, *_: jnp.maximum(hrow(b, n) * nbat - 1, 0)
        in_specs += [pl.BlockSpec((WINDOW, KVW), lambda b, n, *_: (prev(b, n), OFF_KC // KVW)),
                     pl.BlockSpec((WINDOW, KVW), lambda b, n, *_: (prev(b, n), OFF_VC // KVW))]
        args += [h_att, h_att]
    in_specs += [
        pl.BlockSpec((rows, W_C), lambda b, n, *_: (hrow(b, n), OFF_GC // W_C)),
        pl.BlockSpec((bq, span), lambda b, n, *_: (0, 0)),
    ]
    args += [h_att, jnp.asarray(_t5_bucket_table(bq))]
    a_args, a_specs, aliases = _alias_args(prev_outs, 2 + len(args))
    srows = nbat * H_C * bq
    z, wk, wv = pl.pallas_call(
        functools.partial(_attention_kernel, bq=bq, nbat=nbat, has_cache=has_cache,
                          mask_first=mask_first, layer=layer, has_prev=prev_outs is not None),
        out_shape=(jax.ShapeDtypeStruct((batch * t, W_C), BF16),
                   jax.ShapeDtypeStruct((DEPTH, batch) + win_shape, F32),
                   jax.ShapeDtypeStruct((DEPTH, batch) + win_shape, F32)),
        grid_spec=pltpu.PrefetchScalarGridSpec(
            num_scalar_prefetch=2,
            grid=(batch // wb, nsteps),
            in_specs=in_specs + a_specs,
            out_specs=(pl.BlockSpec((rows, W_C), lambda b, n, *_: (row(b, n), 0)),
                       _state_out_spec((wb,) + win_shape, lambda b, n, *_: (b, 0, 0), layer,
                                       prev_outs is not None),
                       _state_out_spec((wb,) + win_shape, lambda b, n, *_: (b, 0, 0), layer,
                                       prev_outs is not None)),
            scratch_shapes=[pltpu.VMEM((2 if mask_first else 1, H_C * bq, span), F32),
                            pltpu.VMEM((srows, KVW), BF16),
                            pltpu.VMEM((srows, span), F32),
                            pltpu.VMEM((srows, span), BF16),
                            pltpu.VMEM((srows, 2 * HD_C), F32),
                            pltpu.VMEM((srows, DENOM_W), F32)]),
        input_output_aliases=aliases,
        compiler_params=_params("arbitrary", "arbitrary"),
        name=name,
    )(sinks, rel_bias, *args, *a_args)
    return z, (wk, wv)


def _out_kernel(*refs, n_x, lead_blocks, split_out, alpha):
    refs = list(refs)
    x_refs, refs = refs[:n_x], refs[n_x:]
    (za0, za1, zb0, zb1, zc0, zc1, ma_ref, mb_ref, mc_ref, wa_ref, wb_ref, wc_ref, wo_ref,
     lng_ref, lnb_ref) = refs[:15]
    if split_out:
        y0_ref, y1_ref, merged_scr, r_scr, y_scr = refs[15:]
    else:
        y_ref, yb_ref, merged_scr, r_scr = refs[15:]
    i = pl.program_id(0)

    @pl.when(i == 0)
    def _():
        r_scr[...] = jnp.zeros_like(r_scr)

    lng, lnb = lng_ref[...], lnb_ref[...]
    for rc in range(r_scr.shape[0] // LN_ROWS):
        rs = slice(rc * LN_ROWS, (rc + 1) * LN_ROWS)
        r = r_scr[rs, :]
        mu = jnp.mean(r, axis=-1, keepdims=True)
        d = r - mu
        var = jnp.mean(d * d, axis=-1, keepdims=True)
        out = d * lax.rsqrt(var + LN_EPS) * lng + lnb
        if split_out:
            y0_ref[rs, :] = out
            y_scr[rs, :] = out
        else:
            y_ref[rs, :] = out
            yb_ref[rs, :] = out.astype(BF16)
    if split_out:
        @pl.when(i <= lead_blocks)
        def _():
            y1_ref[...] = y_scr[...]

    in_first = i >= lead_blocks
    pick = lambda a, b: jnp.where(in_first, a[...], b[...])
    za, zb, zc = pick(za0, za1), pick(zb0, zb1), pick(zc0, zc1)
    x = x_refs[0][...] if n_x == 1 else pick(*x_refs)
    for c in range(D_MODEL // OUT_STAGE_TN):
        cs = slice(c * OUT_STAGE_TN, (c + 1) * OUT_STAGE_TN)
        merged = (_sigmoid(ma_ref[:, cs].astype(F32))
                  * jnp.dot(za, wa_ref[:, cs], preferred_element_type=F32))
        merged += (_sigmoid(mb_ref[:, cs].astype(F32))
                   * jnp.dot(zb, wb_ref[:, cs], preferred_element_type=F32))
        merged += (_sigmoid(mc_ref[:, cs].astype(F32))
                   * jnp.dot(zc, wc_ref[:, cs], preferred_element_type=F32))
        merged_scr[:, cs] = merged.astype(BF16)
    r_scr[...] = alpha * x + jnp.dot(merged_scr[...], wo_ref[...], preferred_element_type=F32)


def _out_stage(xs, zas, zbs, zcs, h_gate, wa, wb, wc, wo, lng, lnb, alpha, *, last, name):
    tm = OUT_STAGE_TM
    m0, m1 = zas[0].shape[0], zas[1].shape[0]
    nb0, nb1 = m0 // tm, m1 // tm
    nb = nb0 + nb1
    first = lambda i: jnp.clip(i - nb1, 0, nb0 - 1)
    second = lambda i: jnp.minimum(i, nb1 - 1)
    cur = lambda i: jnp.where(i < nb1, nb0 + i, jnp.minimum(i, nb - 1) - nb1)
    pair = lambda w: [pl.BlockSpec((tm, w), lambda i: (first(i), 0)),
                      pl.BlockSpec((tm, w), lambda i: (second(i), 0))]
    x_specs = pair(D_MODEL) if len(xs) == 2 else [pl.BlockSpec((tm, D_MODEL), lambda i: (cur(i), 0))]
    gate = lambda j: pl.BlockSpec((tm, D_MODEL), lambda i: (cur(i), j))
    const = lambda a: pl.BlockSpec(a.shape, lambda i: (0,) * a.ndim, pipeline_mode=pl.Buffered(1))
    scratch = [pltpu.VMEM((tm, D_MODEL), BF16), pltpu.VMEM((tm, D_MODEL), F32)]
    if last:
        out_shape = [jax.ShapeDtypeStruct((m0, D_MODEL), F32), jax.ShapeDtypeStruct((m1, D_MODEL), F32)]
        out_specs = [pl.BlockSpec((tm, D_MODEL), lambda i: (first(i - 1), 0)),
                     pl.BlockSpec((tm, D_MODEL), lambda i: (jnp.clip(i - 1, 0, nb1 - 1), 0))]
        scratch.append(pltpu.VMEM((tm, D_MODEL), F32))
    else:
        out_shape = [jax.ShapeDtypeStruct((m0 + m1, D_MODEL), F32),
                     jax.ShapeDtypeStruct((m0 + m1, D_MODEL), BF16)]
        out_specs = [pl.BlockSpec((tm, D_MODEL), lambda i: (cur(jnp.maximum(i - 1, 0)), 0))] * 2
    return pl.pallas_call(
        functools.partial(_out_kernel, n_x=len(xs), lead_blocks=nb1, split_out=last, alpha=alpha),
        out_shape=out_shape,
        grid=(nb + 1,),
        in_specs=x_specs + pair(W_A) + pair(W_B) + pair(W_C) + [gate(0), gate(1), gate(2),
                 const(wa), const(wb), const(wc), const(wo), const(lng), const(lnb)],
        out_specs=out_specs,
        scratch_shapes=scratch,
        compiler_params=_params("arbitrary"),
        name=name,
    )(*xs, *zas, *zbs, *zcs, h_gate, h_gate, h_gate, wa, wb, wc, wo, lng, lnb)


def _layer(groups, xs, xbs, layer, prevs, w, w_in, alpha):
    if xbs is None:
        h_att, xb_all = _in_proj(xs, w["w_att"], N_ATT, emit_xb=True, name=f"in_proj_att_{layer}")
        xbs = (xb_all,)
    else:
        h_att = _in_proj(xbs, w["w_att"], N_ATT, name=f"in_proj_att_{layer}")
    h_ab = _in_proj(xbs, w_in, N_AB, layer=layer, col0=0, name=f"in_proj_ab_{layer}")
    h_gate = _in_proj(xbs, w_in, N_GATE, layer=layer, col0=REF_GATE, out_dtype=BF16,
                      name=f"in_proj_gate_{layer}")
    zas, zbs, zcs, outs, row0 = [], [], [], [], 0
    for (batch, t, pos0, states, tag), prev in zip(groups, prevs):
        st_ret, st_k, st_v, st_pool = states if states is not None else (None,) * 4
        pv_ret, pv_win, pv_pool = prev if prev is not None else (None,) * 3
        if states is None and t % RET_ROWS == 0:
            za, o_ret, zb, o_pool = _ret_pool(h_ab, row0, batch, t, pos0, layer, pv_ret, pv_pool,
                                              w["w_pool_map"], w["pool_scale"],
                                              name=f"ret_pool_{tag}{layer}")
        else:
            za, o_ret = _retention(h_ab, row0, batch, t, pos0, st_ret, layer, pv_ret,
                                   name=f"retention_{tag}{layer}")
            zb, o_pool = _pool(h_ab, row0, batch, t, pos0, st_pool, layer, pv_pool,
                               w["w_pool_map"], w["pool_scale"], name=f"pool_{tag}{layer}")
        zc, o_win = _attention(h_att, row0, batch, t, pos0, st_k, st_v, layer, pv_win, w["sinks"],
                               w["rel_bias"], name=f"attention_{tag}{layer}")
        zas.append(za)
        zbs.append(zb)
        zcs.append(zc)
        outs.append((o_ret, o_win, o_pool))
        row0 += batch * t
    last = layer + 1 == DEPTH
    res = _out_stage(xs, zas, zbs, zcs, h_gate, w["w_ret_o"], w["w_pool_o"], w["w_att_o"],
                     w["w_out"], w["ln_g"], w["ln_b"], alpha, last=last, name=f"out_stage_{layer}")
    if last:
        return tuple(res), None, outs
    y, yb = res
    return (y,), (yb,), outs


def _regroup_heads(w, axis):
    shp = w.shape
    w = w.reshape(shp[:axis] + (KV_C, G_C, HD_C) + shp[axis + 1:])
    w = jnp.swapaxes(w, axis, axis + 1)
    return w.reshape(shp)


def _regroup_matrix():
    new = np.arange(W_C)
    g, kv, d = new // KVW, (new // HD_C) % KV_C, new % HD_C
    p = np.zeros((W_C, W_C), np.float32)
    p[(kv * G_C + g) * HD_C + d, new] = 1.0
    return jnp.asarray(p, BF16)


def _w_att_kernel(q_ref, kv_ref, glo_ref, ghi_ref, perm_ref, o_ref):
    perm = perm_ref[...]
    regroup = lambda w: jnp.dot(w.astype(BF16), perm, preferred_element_type=F32).astype(BF16)
    o_ref[:, OFF_QC:OFF_QC + W_C] = regroup(q_ref[...])
    o_ref[:, OFF_GC:OFF_GC + W_C] = regroup(jnp.concatenate([glo_ref[...], ghi_ref[...]], axis=1))
    o_ref[:, OFF_KC:OFF_KC + 2 * KVW] = kv_ref[...].astype(BF16)


def _prep_w_att(w_in, l):
    half = W_C // 2
    assert REF_QC % W_C == 0 and REF_KC % half == 0 and REF_GC % half == 0
    assert REF_VC == REF_KC + KVW and OFF_VC == OFF_KC + KVW
    blk = lambda width, c0: pl.BlockSpec((None, W_PREP_ROWS, width), lambda i: (l, i, c0 // width))
    return pl.pallas_call(
        _w_att_kernel,
        out_shape=jax.ShapeDtypeStruct((D_MODEL, N_ATT), BF16),
        grid=(D_MODEL // W_PREP_ROWS,),
        in_specs=[blk(W_C, REF_QC), blk(half, REF_KC), blk(half, REF_GC), blk(half, REF_GC + half),
                  pl.BlockSpec((W_C, W_C), lambda i: (0, 0))],
        out_specs=pl.BlockSpec((W_PREP_ROWS, N_ATT), lambda i: (i, 0)),
        compiler_params=_params("parallel"),
        name=f"w_att_prep_{l}",
    )(w_in, w_in, w_in, w_in, _regroup_matrix())


def kernel(x_prompt, x_sample, state_ret, cache_win_k, cache_win_v, state_pool, w_in, w_ret_o,
           w_pool_map, pool_scale, w_pool_o, attn_sinks, w_att_o, w_out, ln_g, ln_b, rel_bias):
    alpha = (2.0 * DEPTH) ** 0.25
    bp, tp, _ = x_prompt.shape
    bs, ts, _ = x_sample.shape
    xp = x_prompt.reshape(bp * tp, D_MODEL)
    xs = x_sample.reshape(bs * ts, D_MODEL)
    to_t = lambda c: c.transpose(0, 1, 3, 4, 2).reshape(DEPTH, bs, KVW, WINDOW)
    from_t = lambda c: c.reshape(DEPTH, bs, KV_C, HD_C, WINDOW).transpose(0, 1, 4, 2, 3)
    ck, cv = to_t(cache_win_k), to_t(cache_win_v)
    groups = ((bp, tp, 0, None, "p"), (bs, ts, PAST_LEN, (state_ret, ck, cv, state_pool), "s"))
    xs, xbs, outs = (xp, xs), None, (None, None)
    for l in range(DEPTH):
        w = {
            "w_att": _prep_w_att(w_in, l),
            "w_ret_o": w_ret_o[l].astype(BF16),
            "w_pool_map": w_pool_map[l].astype(BF16),
            "pool_scale": pool_scale[l].reshape(1, W_B),
            "w_pool_o": w_pool_o[l].astype(BF16),
            "sinks": attn_sinks[l],
            "w_att_o": _regroup_heads(w_att_o[l], 0).astype(BF16),
            "w_out": w_out[l].astype(BF16),
            "ln_g": ln_g[l].reshape(1, D_MODEL),
            "ln_b": ln_b[l].reshape(1, D_MODEL),
            "rel_bias": rel_bias,
        }
        xs, xbs, outs = _layer(groups, xs, xbs, l, outs, w, w_in, alpha)
    ((ret_p,), (kp, vp), (pp,)), ((ret_s,), (ks, vs), (ps,)) = outs
    win = lambda a, b: a.reshape(DEPTH, b, WINDOW, KV_C, HD_C)
    return (xs[0].reshape(bp, tp, D_MODEL), xs[1].reshape(bs, ts, D_MODEL), ret_p, ret_s,
            win(kp, bp), from_t(ks), win(vp, bp), from_t(vs), pp, ps)
```

```python
import functools
import math

import jax
import jax.numpy as jnp
import numpy as np
from jax import lax
from jax.experimental import pallas as pl
from jax.experimental.pallas import tpu as pltpu

D_MODEL = 2048
DEPTH = 2
PAST_LEN = 8192
H_A, DK_A, DV_A = 4, 128, 256
W_A = H_A * DV_A
RET_CHUNK = 128
ROPE_BASE = 10000.0
N_POOL_GROUPS = 4
W_B = 1024
GW_B = W_B // N_POOL_GROUPS
POOL_WINDOWS = (2, 4, 8, 16)
POOL_BUF = 15
POOL_HALO = 16
H_C, KV_C, HD_C = 16, 4, 64
G_C = H_C // KV_C
W_C = H_C * HD_C
KVW = KV_C * HD_C
WINDOW = 128
ATT_BLOCK = 128
SOFTMAX_ROWS = 128
DENOM_W = 2 * HD_C
NUM_BUCKETS = 32
MAX_DISTANCE = 128
LN_EPS = 1e-5
RMS_EPS = 1e-6

REF_QC, REF_KC, REF_VC, REF_GC = 5120, 6144, 6400, 6656
N_AB = 5120
N_ATT = 2560
REF_GATE = 7680
N_GATE = 3 * D_MODEL
OFF_QA, OFF_KA, OFF_VA, OFF_GA, OFF_UB, OFF_GB = 0, 512, 1024, 2048, 3072, 4096
OFF_QC, OFF_GC, OFF_KC, OFF_VC = 0, 1024, 2048, 2304

F32 = jnp.float32
BF16 = jnp.bfloat16
VMEM_LIMIT = 56 * 1024 * 1024

IN_PROJ_TM = 1024
IN_PROJ_TM_F32_X = 512
IN_PROJ_TN = {N_AB: 1280, N_ATT: 2560, N_GATE: 1536}
OUT_STAGE_TM = 256
OUT_STAGE_TN = 512
LN_ROWS = 16
POOL_ROWS = 1024
SMALL_T_POOL_BATCH = 16
W_PREP_ROWS = 512
ATT_BLOCKS_PER_STEP = 4
SMALL_T_ATT_BATCH = 8
RET_ROWS = 512
SMALL_T_RET_BATCH = 8


LOG2E = 1.0 / math.log(2.0)


def _sigmoid(x):
    return 1.0 / (1.0 + jnp.exp2(x * (-LOG2E)))


def _silu(x):
    return x * _sigmoid(x)


def _params(*sem):
    return pltpu.CompilerParams(dimension_semantics=sem, vmem_limit_bytes=VMEM_LIMIT)


def _alias_args(prev_outs, n_in):
    if prev_outs is None:
        return [], [], {}
    specs = [pl.BlockSpec(memory_space=pl.ANY) for _ in prev_outs]
    return list(prev_outs), specs, {n_in + i: 1 + i for i in range(len(prev_outs))}


def _block_offset(row0, rows):
    assert row0 % rows == 0
    return row0 // rows


def _state_out_spec(block, index_map, layer, has_prev):
    if has_prev:
        return pl.BlockSpec((None,) + block, lambda *a: (layer,) + index_map(*a))
    return pl.BlockSpec((DEPTH,) + block, lambda *a: (0,) + index_map(*a))


def _state_store(ref, idx, value, layer, has_prev):
    if has_prev:
        ref[idx] = value
    else:
        for l in range(DEPTH):
            ref[(l,) + idx] = value if l == layer else jnp.zeros_like(value)


def _matmul_kernel(*refs, n_x, first_blocks, cast_w, emit_xb):
    refs = list(refs)
    x_refs, (w_ref, o_ref), rest = refs[:n_x], refs[n_x:n_x + 2], refs[n_x + 2:]
    xb_ref = rest.pop(0) if emit_xb else None
    if cast_w:
        wb_scr, = rest

        @pl.when(pl.program_id(1) == 0)
        def _():
            wb_scr[...] = w_ref[...].astype(BF16)
        w_ref = wb_scr

    x = x_refs[0][...]
    if n_x == 2:
        x = jnp.where(pl.program_id(1) < first_blocks, x, x_refs[1][...])
    x = x.astype(BF16)
    if emit_xb:
        xb_ref[...] = x
    o_ref[...] = jnp.dot(x, w_ref[...], preferred_element_type=F32).astype(o_ref.dtype)


def _in_proj(xs, w, n, *, layer=None, col0=0, emit_xb=False, out_dtype=F32, name):
    assert len(xs) in (1, 2)
    k = xs[0].shape[1]
    m = sum(x.shape[0] for x in xs)
    tm, tn = (IN_PROJ_TM_F32_X if xs[0].dtype == F32 else IN_PROJ_TM), IN_PROJ_TN[n]
    first = xs[0].shape[0] // tm
    assert all(x.shape[0] % tm == 0 for x in xs)
    cast_w = w.dtype == F32
    if cast_w:
        w_spec = pl.BlockSpec((None, k, tn), lambda j, i: (layer, 0, col0 // tn + j))
    else:
        w_spec = pl.BlockSpec((k, tn), lambda j, i: (0, j),
                              pipeline_mode=pl.Buffered(1) if n == tn else None)
    if len(xs) == 1:
        x_specs = [pl.BlockSpec((tm, k), lambda j, i: (i, 0))]
    else:
        last = xs[1].shape[0] // tm - 1
        x_specs = [pl.BlockSpec((tm, k), lambda j, i: (jnp.minimum(i, first - 1), 0)),
                   pl.BlockSpec((tm, k), lambda j, i: (jnp.clip(i - first, 0, last), 0),
                                pipeline_mode=None if last else pl.Buffered(1))]
    assert not emit_xb or n == tn
    out_shape = [jax.ShapeDtypeStruct((m, n), out_dtype)]
    out_specs = [pl.BlockSpec((tm, tn), lambda j, i: (i, j))]
    if emit_xb:
        out_shape.append(jax.ShapeDtypeStruct((m, k), BF16))
        out_specs.append(pl.BlockSpec((tm, k), lambda j, i: (i, 0)))
    outs = pl.pallas_call(
        functools.partial(_matmul_kernel, n_x=len(xs), first_blocks=first, cast_w=cast_w,
                          emit_xb=emit_xb),
        out_shape=out_shape,
        grid=(n // tn, m // tm),
        in_specs=x_specs + [w_spec],
        out_specs=out_specs,
        scratch_shapes=[pltpu.VMEM((k, tn), BF16)] if cast_w else [],
        compiler_params=_params("arbitrary", "arbitrary"),
        name=name,
    )(*xs, w)
    return outs if emit_xb else outs[0]


def _retention_kernel(*refs, chunk, nchunk, nbat, has_state, layer, has_prev, finalize=True):
    refs = list(refs)
    (q_ref, k_ref, v_ref, g_ref, cos_ref, sin_ref, dmask_ref, qdec_ref, kdec_ref, cdec_ref) = refs[:10]
    refs = refs[10:]
    s0_ref = refs.pop(0) if has_state else None
    if has_prev:
        refs.pop(0)
    z_ref, sout_ref, s_scr = refs
    c = pl.program_id(1)

    @pl.when(c == 0)
    def _():
        if has_state:
            s_scr[...] = s0_ref[...]
        else:
            s_scr[...] = jnp.zeros_like(s_scr)

    for bi in range(nbat):
        for hd in range(H_A):
            s = s_scr[bi, hd]
            for ci in range(nchunk):
                rows = slice((bi * nchunk + ci) * chunk, (bi * nchunk + ci + 1) * chunk)
                trow = slice(ci * chunk, (ci + 1) * chunk)
                cos = cos_ref[trow, :]
                sin = sin_ref[trow, :]
                q = q_ref[rows, hd * DK_A:(hd + 1) * DK_A]
                k = k_ref[rows, hd * DK_A:(hd + 1) * DK_A]
                v = v_ref[rows, hd * DV_A:(hd + 1) * DV_A].astype(BF16)
                qr = q * cos + pltpu.roll(q, DK_A // 2, 1) * sin
                kr = (k * cos + pltpu.roll(k, DK_A // 2, 1) * sin) * (DK_A ** -0.5)
                qb = qr.astype(BF16)
                sc = lax.dot_general(qb, kr.astype(BF16), (((1,), (1,)), ((), ())),
                                     preferred_element_type=F32) * dmask_ref[hd]
                o = (jnp.dot(sc.astype(BF16), v, preferred_element_type=F32)
                     + jnp.dot(qb, s.astype(BF16), preferred_element_type=F32) * qdec_ref[hd])
                kd = (kr * kdec_ref[hd]).astype(BF16)
                s = s * cdec_ref[hd] + lax.dot_general(
                    kd, v, (((0,), (0,)), ((), ())), preferred_element_type=F32)
                o = o * lax.rsqrt(jnp.mean(o * o, axis=-1, keepdims=True) + RMS_EPS)
                g = g_ref[rows, hd * DV_A:(hd + 1) * DV_A]
                z_ref[rows, hd * DV_A:(hd + 1) * DV_A] = (o * _silu(g)).astype(z_ref.dtype)
            s_scr[bi, hd] = s

    if finalize:
        @pl.when(c == pl.num_programs(1) - 1)
        def _():
            _state_store(sout_ref, (Ellipsis,), s_scr[...], layer, has_prev)


def _retention_tables(pos0, t, chunk):
    half = DK_A // 2
    inv = ROPE_BASE ** (-jnp.arange(half, dtype=F32) / half)
    pos = pos0 + jnp.arange(t, dtype=F32)
    ang = pos[:, None] * inv[None, :]
    cos, sin = jnp.cos(ang), jnp.sin(ang)
    cos2 = jnp.concatenate([cos, cos], -1)
    sin2 = jnp.concatenate([-sin, sin], -1)
    lg = jnp.log1p(-jnp.exp2(-5.0 - jnp.arange(H_A, dtype=F32)))
    idx = jnp.arange(chunk, dtype=F32)
    diff = idx[:, None] - idx[None, :]
    dmask = jnp.where(diff >= 0, jnp.exp(lg[:, None, None] * jnp.maximum(diff, 0.0)), 0.0)
    qdec = jnp.exp(lg[:, None] * (idx[None, :] + 1.0))
    kdec = jnp.exp(lg[:, None] * (chunk - 1.0 - idx[None, :]))
    cdec = jnp.exp(lg * chunk)
    qdec = jnp.broadcast_to(qdec[:, :, None], (H_A, chunk, DV_A))
    kdec = jnp.broadcast_to(kdec[:, :, None], (H_A, chunk, DK_A))
    cdec = jnp.broadcast_to(cdec[:, None, None], (H_A, 1, DV_A))
    return cos2, sin2, dmask, qdec, kdec, cdec


def _retention(h_ab, row0, batch, t, pos0, state, layer, prev_outs, *, name):
    chunk = min(RET_CHUNK, t)
    if t > chunk:
        nbat, nchunk = 1, RET_ROWS // chunk
    else:
        nbat, nchunk = SMALL_T_RET_BATCH, 1
    rows = nbat * nchunk * chunk
    nsteps = t // (nchunk * chunk)
    has_state = state is not None
    tables = _retention_tables(pos0, t, chunk)
    row = lambda b, c: b * nsteps + c
    hrow = lambda b, c: row(b, c) + _block_offset(row0, rows)
    in_specs = [
        pl.BlockSpec((rows, H_A * DK_A), lambda b, c: (hrow(b, c), OFF_QA // (H_A * DK_A))),
        pl.BlockSpec((rows, H_A * DK_A), lambda b, c: (hrow(b, c), OFF_KA // (H_A * DK_A))),
        pl.BlockSpec((rows, W_A), lambda b, c: (hrow(b, c), OFF_VA // W_A)),
        pl.BlockSpec((rows, W_A), lambda b, c: (hrow(b, c), OFF_GA // W_A)),
        pl.BlockSpec((nchunk * chunk, DK_A), lambda b, c: (c, 0)),
        pl.BlockSpec((nchunk * chunk, DK_A), lambda b, c: (c, 0)),
        pl.BlockSpec((H_A, chunk, chunk), lambda b, c: (0, 0, 0)),
        pl.BlockSpec((H_A, chunk, DV_A), lambda b, c: (0, 0, 0)),
        pl.BlockSpec((H_A, chunk, DK_A), lambda b, c: (0, 0, 0)),
        pl.BlockSpec((H_A, 1, DV_A), lambda b, c: (0, 0, 0)),
    ]
    args = [h_ab, h_ab, h_ab, h_ab, *tables]
    if has_state:
        in_specs.append(pl.BlockSpec((None, nbat, H_A, DK_A, DV_A), lambda b, c: (layer, b, 0, 0, 0)))
        args.append(state)
    a_args, a_specs, aliases = _alias_args(prev_outs, len(args))
    z, s_out = pl.pallas_call(
        functools.partial(_retention_kernel, chunk=chunk, nchunk=nchunk, nbat=nbat,
                          has_state=has_state, layer=layer, has_prev=prev_outs is not None),
        out_shape=(jax.ShapeDtypeStruct((batch * t, W_A), BF16),
                   jax.ShapeDtypeStruct((DEPTH, batch, H_A, DK_A, DV_A), F32)),
        grid=(batch // nbat, nsteps),
        in_specs=in_specs + a_specs,
        out_specs=(pl.BlockSpec((rows, W_A), lambda b, c: (row(b, c), 0)),
                   _state_out_spec((nbat, H_A, DK_A, DV_A), lambda b, c: (b, 0, 0, 0), layer,
                                   prev_outs is not None)),
        scratch_shapes=[pltpu.VMEM((nbat, H_A, DK_A, DV_A), F32)],
        input_output_aliases=aliases,
        compiler_params=_params("arbitrary", "arbitrary"),
        name=name,
    )(*args, *a_args)
    return z, (s_out,)


def _pool_kernel(*refs, nb, tb, pos0, has_state, layer, has_prev, finalize=True):
    refs = list(refs)
    u_ref, g_ref, halo_ref, wmap_ref, scale_ref = refs[:5]
    refs = refs[5:]
    if has_prev:
        refs.pop(0)
    z_ref, pout_ref, ext_scr, p_scr = refs
    ti = pl.program_id(1)
    for bi in range(nb):
        rows = pl.ds(bi * tb, tb)
        if has_state:
            ext_scr[0:1, :] = jnp.zeros((1, W_B), F32)
            ext_scr[1:POOL_HALO, :] = halo_ref[bi]
        else:
            ext_scr[0:POOL_HALO, :] = jnp.where(ti == 0, 0.0, halo_ref[...])
        ext_scr[POOL_HALO:POOL_HALO + tb, :] = u_ref[rows, :]
        for gi, w in enumerate(POOL_WINDOWS):
            cols = slice(gi * GW_B, (gi + 1) * GW_B)
            acc = ext_scr[:, cols]
            shift = 1
            while shift < w:
                acc = acc + pltpu.roll(acc, shift, 0)
                shift *= 2
            u = ext_scr[POOL_HALO:POOL_HALO + tb, cols]
            p_scr[rows, cols] = acc[POOL_HALO:POOL_HALO + tb] * (1.0 / w) - u
            if pos0 < POOL_BUF:
                head = min(tb, POOL_HALO)
                t_idx = pos0 + ti * tb + lax.broadcasted_iota(jnp.int32, (head, GW_B), 0)
                cnt = jnp.minimum(t_idx + 1, w).astype(F32)
                p_scr[pl.ds(bi * tb, head), cols] = acc[POOL_HALO:POOL_HALO + head] / cnt - u[:head]

        if finalize == "inline":
            _state_store(pout_ref, (bi,), ext_scr[tb + 1:tb + POOL_HALO, :], layer, has_prev)
        elif finalize:
            @pl.when(ti == pl.num_programs(1) - 1)
            def _():
                _state_store(pout_ref, (bi,), ext_scr[tb + 1:tb + POOL_HALO, :], layer, has_prev)

    for gi in range(N_POOL_GROUPS):
        cols = slice(gi * GW_B, (gi + 1) * GW_B)
        pm = jnp.dot(p_scr[:, cols].astype(BF16), wmap_ref[gi], preferred_element_type=F32)
        pm = pm * scale_ref[:, cols]
        z_ref[:, cols] = (pm * _silu(g_ref[:, cols])).astype(z_ref.dtype)


def _pool(h_ab, row0, batch, t, pos0, state, layer, prev_outs, wmap, scale, *, name):
    has_state = state is not None
    if t >= POOL_ROWS:
        nb, tb = 1, POOL_ROWS
    else:
        nb, tb = SMALL_T_POOL_BATCH, t
    nt = t // tb
    rows = nb * tb
    rblk = lambda b, i: b * nt + i
    hblk = lambda b, i: rblk(b, i) + _block_offset(row0, rows)
    in_specs = [pl.BlockSpec((rows, W_B), lambda b, i: (hblk(b, i), OFF_UB // W_B)),
                pl.BlockSpec((rows, W_B), lambda b, i: (hblk(b, i), OFF_GB // W_B))]
    args = [h_ab, h_ab]
    if has_state:
        assert nt == 1
        in_specs.append(pl.BlockSpec((None, nb, POOL_BUF, W_B), lambda b, i: (layer, b, 0, 0)))
        args.append(state)
    else:
        assert nb == 1
        per = tb // POOL_HALO
        in_specs.append(pl.BlockSpec(
            (POOL_HALO, W_B), lambda b, i: (jnp.maximum(hblk(b, i) * per - 1, 0), OFF_UB // W_B)))
        args.append(h_ab)
    in_specs += [pl.BlockSpec((N_POOL_GROUPS, GW_B, GW_B), lambda b, i: (0, 0, 0)),
                 pl.BlockSpec((1, W_B), lambda b, i: (0, 0))]
    args += [wmap, scale]
    a_args, a_specs, aliases = _alias_args(prev_outs, len(args))
    z, p_out = pl.pallas_call(
        functools.partial(_pool_kernel, nb=nb, tb=tb, pos0=pos0, has_state=has_state,
                          layer=layer, has_prev=prev_outs is not None),
        out_shape=(jax.ShapeDtypeStruct((batch * t, W_B), BF16),
                   jax.ShapeDtypeStruct((DEPTH, batch, POOL_BUF, W_B), F32)),
        grid=(batch // nb, nt),
        in_specs=in_specs + a_specs,
        out_specs=(pl.BlockSpec((rows, W_B), lambda b, i: (rblk(b, i), 0)),
                   _state_out_spec((nb, POOL_BUF, W_B), lambda b, i: (b, 0, 0), layer,
                                   prev_outs is not None)),
        scratch_shapes=[pltpu.VMEM((POOL_HALO + tb, W_B), F32),
                        pltpu.VMEM((rows, W_B), F32)],
        input_output_aliases=aliases,
        compiler_params=_params("arbitrary", "arbitrary"),
        name=name,
    )(*args, *a_args)
    return z, (p_out,)


N_RET_IN, N_POOL_IN = 10, 5


def _ret_pool_kernel(*refs, chunk, nchunk, nseq, tb, pos0, has_state, layer, has_prev):
    refs = list(refs)
    n_ret = N_RET_IN + has_state
    ret_in, pool_in = refs[:n_ret], refs[n_ret:n_ret + N_POOL_IN]
    refs = refs[n_ret + N_POOL_IN:]
    prev = [[refs.pop(0)], [refs.pop(0)]] if has_prev else [[], []]
    za_ref, sout_ref, zb_ref, pout_ref, s_scr, ext_scr, p_scr = refs
    _retention_kernel(*ret_in, *prev[0], za_ref, sout_ref, s_scr, chunk=chunk, nchunk=nchunk,
                      nbat=nseq, has_state=has_state, layer=layer, has_prev=has_prev,
                      finalize=False)
    _pool_kernel(*pool_in, *prev[1], zb_ref, pout_ref, ext_scr, p_scr, nb=nseq, tb=tb, pos0=pos0,
                 has_state=has_state, layer=layer, has_prev=has_prev,
                 finalize="inline" if has_state else False)
    if has_state:
        _state_store(sout_ref, (Ellipsis,), s_scr[...], layer, has_prev)
    else:
        @pl.when(pl.program_id(1) == pl.num_programs(1) - 1)
        def _():
            _state_store(sout_ref, (Ellipsis,), s_scr[...], layer, has_prev)
            _state_store(pout_ref, (0,), ext_scr[tb + 1:tb + POOL_HALO, :], layer, has_prev)


def _ret_pool(h_ab, row0, batch, t, pos0, st_ret, st_pool, layer, prev_ret, prev_pool, wmap, scale,
              *, name):
    has_state = st_ret is not None
    if has_state:
        assert t <= RET_CHUNK
        chunk, nseq = t, SMALL_T_RET_BATCH
        rows, tb = nseq * t, t
    else:
        chunk, nseq, rows, tb = RET_CHUNK, 1, RET_ROWS, RET_ROWS
    nchunk, nsteps = tb // chunk, t // tb
    has_prev = prev_ret is not None
    tables = _retention_tables(pos0, t, chunk)
    row = lambda b, c: b * nsteps + c
    hrow = lambda b, c: row(b, c) + _block_offset(row0, rows)
    col = lambda width, off: pl.BlockSpec((rows, width), lambda b, c: (hrow(b, c), off // width))
    full = lambda shape: pl.BlockSpec(shape, lambda b, c: (0,) * len(shape))
    in_specs = [
        col(H_A * DK_A, OFF_QA), col(H_A * DK_A, OFF_KA), col(W_A, OFF_VA), col(W_A, OFF_GA),
        pl.BlockSpec((tb, DK_A), lambda b, c: (c, 0)), pl.BlockSpec((tb, DK_A), lambda b, c: (c, 0)),
        full((H_A, chunk, chunk)), full((H_A, chunk, DV_A)), full((H_A, chunk, DK_A)),
        full((H_A, 1, DV_A)),
    ]
    args = [h_ab] * 4 + list(tables)
    if has_state:
        in_specs.append(pl.BlockSpec((None, nseq, H_A, DK_A, DV_A), lambda b, c: (layer, b, 0, 0, 0)))
        args.append(st_ret)
    in_specs += [col(W_B, OFF_UB), col(W_B, OFF_GB)]
    args += [h_ab, h_ab]
    if has_state:
        in_specs.append(pl.BlockSpec((None, nseq, POOL_BUF, W_B), lambda b, c: (layer, b, 0, 0)))
        args.append(st_pool)
    else:
        in_specs.append(pl.BlockSpec((POOL_HALO, W_B), lambda b, c: (
            jnp.maximum(hrow(b, c) * (rows // POOL_HALO) - 1, 0), OFF_UB // W_B)))
        args.append(h_ab)
    in_specs += [full((N_POOL_GROUPS, GW_B, GW_B)), full((1, W_B))]
    args += [wmap, scale]
    assert len(args) == N_RET_IN + has_state + N_POOL_IN
    aliases, a_args, a_specs = {}, [], []
    if has_prev:
        a_args = [prev_ret[0], prev_pool[0]]
        a_specs = [pl.BlockSpec(memory_space=pl.ANY)] * 2
        aliases = {len(args): 1, len(args) + 1: 3}
    za, s_out, zb, p_out = pl.pallas_call(
        functools.partial(_ret_pool_kernel, chunk=chunk, nchunk=nchunk, nseq=nseq, tb=tb, pos0=pos0,
                          has_state=has_state, layer=layer, has_prev=has_prev),
        out_shape=(jax.ShapeDtypeStruct((batch * t, W_A), BF16),
                   jax.ShapeDtypeStruct((DEPTH, batch, H_A, DK_A, DV_A), F32),
                   jax.ShapeDtypeStruct((batch * t, W_B), BF16),
                   jax.ShapeDtypeStruct((DEPTH, batch, POOL_BUF, W_B), F32)),
        grid=(batch // nseq, nsteps),
        in_specs=in_specs + a_specs,
        out_specs=(pl.BlockSpec((rows, W_A), lambda b, c: (row(b, c), 0)),
                   _state_out_spec((nseq, H_A, DK_A, DV_A), lambda b, c: (b, 0, 0, 0), layer,
                                   has_prev),
                   pl.BlockSpec((rows, W_B), lambda b, c: (row(b, c), 0)),
                   _state_out_spec((nseq, POOL_BUF, W_B), lambda b, c: (b, 0, 0), layer, has_prev)),
        scratch_shapes=[pltpu.VMEM((nseq, H_A, DK_A, DV_A), F32),
                        pltpu.VMEM((POOL_HALO + tb, W_B), F32),
                        pltpu.VMEM((rows, W_B), F32)],
        input_output_aliases=aliases,
        compiler_params=_params("arbitrary", "arbitrary"),
        name=name,
    )(*args, *a_args)
    return za, (s_out,), zb, (p_out,)


def _t5_bucket_table(bq):
    span = WINDOW + bq
    dist = np.arange(bq)[:, None] + WINDOW - np.arange(span)[None, :]
    max_exact = NUM_BUCKETS // 2
    d = np.maximum(dist, 0).astype(np.float32)
    large = max_exact + (np.log(np.maximum(d, np.float32(1.0)) / np.float32(max_exact))
                         / np.float32(math.log(MAX_DISTANCE / max_exact))
                         * np.float32(NUM_BUCKETS - max_exact)).astype(np.int32)
    large = np.minimum(large, NUM_BUCKETS - 1)
    bucket = np.where(dist < max_exact, np.maximum(dist, 0), large)
    valid = (dist >= 0) & (dist < WINDOW)
    return np.where(valid, bucket, -1).astype(np.int32)


def _attention_kernel(*refs, bq, nbat, has_cache, mask_first, layer, has_prev):
    refs = list(refs)
    (sinks_ref, relb_ref, q_ref, kc_ref, vc_ref, kp_ref, vp_ref, g_ref, bucket_ref) = refs[:9]
    refs = refs[9:]
    if has_prev:
        del refs[:2]
    z_ref, wk_ref, wv_ref, bias_scr, qbd_scr, s_scr, p_scr, o_scr, l_scr = refs
    span = WINDOW + bq
    hrows = H_C * bq
    grows = G_C * bq
    chunk = min(hrows, SOFTMAX_ROWS)
    one_matmul = hrows <= SOFTMAX_ROWS
    fmin = jnp.finfo(F32).min
    n = pl.program_id(1)

    @pl.when((pl.program_id(0) == 0) & (n == 0))
    def _():
        bucket = bucket_ref[...]
        col = lax.broadcasted_iota(jnp.int32, (bq, span), 1)
        for h in range(H_C):
            acc = jnp.where(bucket < 0, fmin, 0.0)
            for b in range(NUM_BUCKETS):
                acc = jnp.where(bucket == b, relb_ref[b, h] * LOG2E, acc)
            acc = jnp.where(col == 0, sinks_ref[h] * LOG2E, acc)
            bias_scr[0, h * bq:(h + 1) * bq, :] = acc
            if mask_first:
                bias_scr[1, h * bq:(h + 1) * bq, :] = jnp.where((col < WINDOW) & (col > 0), fmin, acc)

    lane_kv = lax.broadcasted_iota(jnp.int32, (1, KVW), 1) // HD_C
    low_half = lax.broadcasted_iota(jnp.int32, (1, 2 * HD_C), 1) < HD_C
    row0 = lax.broadcasted_iota(jnp.int32, (WINDOW, KVW), 0) == 0
    col0_t = lax.broadcasted_iota(jnp.int32, (KVW, WINDOW), 1) == 0
    ones = jnp.ones((span, DENOM_W), BF16)
    kc = vc = None
    for bi in range(nbat):
        rows = slice(bi * bq, (bi + 1) * bq)
        base = bi * hrows
        if has_cache:
            kp, vp = kp_ref[bi], vp_ref[bi]
        elif bi == 0:
            kp, vp = kp_ref[...], vp_ref[...]
        else:
            kp, vp = kc, vc
        bsel = jnp.where(n == 0, 1, 0) if (mask_first and bi == 0) else 0
        kc = kc_ref[rows, :]
        vc = vc_ref[rows, :]
        q = q_ref[rows, :] * (HD_C ** -0.5 * LOG2E)
        for kk in range(KV_C):
            qbd_scr[base + kk * grows:base + (kk + 1) * grows, :] = jnp.concatenate(
                [jnp.where(lane_kv == kk, q[:, g * KVW:(g + 1) * KVW], 0.0) for g in range(G_C)],
                axis=0).astype(BF16)
        qbd = qbd_scr[base:base + hrows, :]
        nt_dims = (((1,), (1,)), ((), ()))
        if has_cache:
            kpt = jnp.where(col0_t, 0.0, kp).astype(BF16)
            s_scr[base:base + hrows, :] = jnp.concatenate(
                [jnp.dot(qbd, kpt, preferred_element_type=F32),
                 lax.dot_general(qbd, kc.astype(BF16), nt_dims, preferred_element_type=F32)], axis=1)
        else:
            kx = jnp.concatenate([jnp.where(row0, 0.0, kp), kc], axis=0).astype(BF16)
            vx = jnp.concatenate([jnp.where(row0, 0.0, vp), vc], axis=0).astype(BF16)
            s_scr[base:base + hrows, :] = lax.dot_general(qbd, kx, nt_dims,
                                                          preferred_element_type=F32)
        for c in range(hrows // chunk):
            rs = slice(base + c * chunk, base + (c + 1) * chunk)
            s = s_scr[rs, :] + bias_scr[bsel, c * chunk:(c + 1) * chunk, :]
            m = jnp.max(s, axis=-1, keepdims=True)
            p_scr[rs, :] = jnp.exp2(s - m).astype(BF16)
        if has_cache:
            assert one_matmul
            p = p_scr[base:base + hrows, :]
            vpt1 = jnp.concatenate([jnp.where(col0_t, 0.0, vp).astype(BF16),
                                    jnp.ones((DENOM_W, WINDOW), BF16)], axis=0)
            vc1 = jnp.concatenate([vc.astype(BF16), jnp.ones((bq, DENOM_W), BF16)], axis=1)
            o3 = (lax.dot_general(p[:, :WINDOW], vpt1, nt_dims, preferred_element_type=F32)
                  + jnp.dot(p[:, WINDOW:], vc1, preferred_element_type=F32))
        elif one_matmul:
            o3 = jnp.dot(p_scr[base:base + hrows, :], jnp.concatenate([vx, ones], axis=1),
                         preferred_element_type=F32)
        if one_matmul:
            for kk in range(KV_C):
                o_scr[base + kk * grows:base + (kk + 1) * grows, :] = (
                    o3[kk * grows:(kk + 1) * grows, (kk // 2) * 2 * HD_C:(kk // 2 + 1) * 2 * HD_C])
            l_scr[base:base + hrows, :] = o3[:, KVW:KVW + DENOM_W]
        else:
            for kk in range(KV_C):
                rk = slice(base + kk * grows, base + (kk + 1) * grows)
                slab = vx[:, (kk // 2) * 2 * HD_C:(kk // 2 + 1) * 2 * HD_C]
                vk = jnp.where(low_half if kk % 2 == 0 else ~low_half, slab, jnp.zeros_like(slab))
                ol = jnp.dot(p_scr[rk, :], jnp.concatenate([vk, ones], axis=1),
                             preferred_element_type=F32)
                o_scr[rk, :] = ol[:, :2 * HD_C]
                l_scr[rk, :] = ol[:, 2 * HD_C:]
        for g in range(G_C):
            halves = []
            for half in range(2):
                ra, rb = (slice(base + kk * grows + g * bq, base + kk * grows + (g + 1) * bq)
                          for kk in (2 * half, 2 * half + 1))
                num = jnp.where(low_half, o_scr[ra, :], o_scr[rb, :])
                den = jnp.where(low_half, l_scr[ra, :], l_scr[rb, :])
                halves.append(num * (1.0 / den))
            og = jnp.concatenate(halves, axis=1)
            cols = slice(g * KVW, (g + 1) * KVW)
            z_ref[rows, cols] = (og * _silu(g_ref[rows, cols])).astype(z_ref.dtype)

        if has_cache:
            keep = lax.broadcasted_iota(jnp.int32, (KVW, WINDOW), 1) < WINDOW - bq
            pad = jnp.zeros((WINDOW - bq, KVW), F32)
            for ref, old, new in ((wk_ref, kp, kc), (wv_ref, vp, vc)):
                new_t = jnp.concatenate([pad, new], axis=0).T
                _state_store(ref, (bi,), jnp.where(keep, pltpu.roll(old, WINDOW - bq, 1), new_t),
                             layer, has_prev)
        elif bi == nbat - 1:
            @pl.when(n == pl.num_programs(1) - 1)
            def _():
                _state_store(wk_ref, (0,), kc, layer, has_prev)
                _state_store(wv_ref, (0,), vc, layer, has_prev)


def _attention(h_att, row0, batch, t, pos0, cache_k, cache_v, layer, prev_outs, sinks, rel_bias, *,
               name):
    bq = min(ATT_BLOCK, t)
    nb = t // bq
    has_cache = cache_k is not None
    nbat = SMALL_T_ATT_BATCH if has_cache else ATT_BLOCKS_PER_STEP
    nsteps = 1 if has_cache else nb // nbat
    wb = nbat if has_cache else 1
    win_shape = (KVW, WINDOW) if has_cache else (WINDOW, KVW)
    span = WINDOW + bq
    assert pos0 == 0 or pos0 >= WINDOW
    mask_first = pos0 == 0
    rows = nbat * bq
    row = lambda b, n, *_: b * nsteps + n
    hrow = lambda b, n, *_: row(b, n) + _block_offset(row0, rows)
    in_specs = [
        pl.BlockSpec((rows, W_C), lambda b, n, *_: (hrow(b, n), OFF_QC // W_C)),
        pl.BlockSpec((rows, KVW), lambda b, n, *_: (hrow(b, n), OFF_KC // KVW)),
        pl.BlockSpec((rows, KVW), lambda b, n, *_: (hrow(b, n), OFF_VC // KVW)),
    ]
    args = [h_att, h_att, h_att]
    if has_cache:
        assert nb == 1
        in_specs += [pl.BlockSpec((None, nbat) + win_shape, lambda b, n, *_: (layer, b, 0, 0)),
                     pl.BlockSpec((None, nbat) + win_shape, lambda b, n, *_: (layer, b, 0, 0))]
        args += [cache_k, cache_v]
    else:
        assert bq == WINDOW and nb % nbat == 0
        prev = lambda b, n, *_: jnp.maximum(hrow(b, n) * nbat - 1, 0)
        in_specs += [pl.BlockSpec((WINDOW, KVW), lambda b, n, *_: (prev(b, n), OFF_KC // KVW)),
                     pl.BlockSpec((WINDOW, KVW), lambda b, n, *_: (prev(b, n), OFF_VC // KVW))]
        args += [h_att, h_att]
    in_specs += [
        pl.BlockSpec((rows, W_C), lambda b, n, *_: (hrow(b, n), OFF_GC // W_C)),
        pl.BlockSpec((bq, span), lambda b, n, *_: (0, 0)),
    ]
    args += [h_att, jnp.asarray(_t5_bucket_table(bq))]
    a_args, a_specs, aliases = _alias_args(prev_outs, 2 + len(args))
    srows = nbat * H_C * bq
    z, wk, wv = pl.pallas_call(
        functools.partial(_attention_kernel, bq=bq, nbat=nbat, has_cache=has_cache,
                          mask_first=mask_first, layer=layer, has_prev=prev_outs is not None),
        out_shape=(jax.ShapeDtypeStruct((batch * t, W_C), BF16),
                   jax.ShapeDtypeStruct((DEPTH, batch) + win_shape, F32),
                   jax.ShapeDtypeStruct((DEPTH, batch) + win_shape, F32)),
        grid_spec=pltpu.PrefetchScalarGridSpec(
            num_scalar_prefetch=2,
            grid=(batch // wb, nsteps),
            in_specs=in_specs + a_specs,
            out_specs=(pl.BlockSpec((rows, W_C), lambda b, n, *_: (row(b, n), 0)),
                       _state_out_spec((wb,) + win_shape, lambda b, n, *_: (b, 0, 0), layer,
                                       prev_outs is not None),
                       _state_out_spec((wb,) + win_shape, lambda b, n, *_: (b, 0, 0), layer,
                                       prev_outs is not None)),
            scratch_shapes=[pltpu.VMEM((2 if mask_first else 1, H_C * bq, span), F32),
                            pltpu.VMEM((srows, KVW), BF16),
                            pltpu.VMEM((srows, span), F32),
                            pltpu.VMEM((srows, span), BF16),
                            pltpu.VMEM((srows, 2 * HD_C), F32),
                            pltpu.VMEM((srows, DENOM_W), F32)]),
        input_output_aliases=aliases,
        compiler_params=_params("arbitrary", "arbitrary"),
        name=name,
    )(sinks, rel_bias, *args, *a_args)
    return z, (wk, wv)


def _out_kernel(*refs, n_x, lead_blocks, split_out, alpha):
    refs = list(refs)
    x_refs, refs = refs[:n_x], refs[n_x:]
    (za0, za1, zb0, zb1, zc0, zc1, ma_ref, mb_ref, mc_ref, wa_ref, wb_ref, wc_ref, wo_ref,
     lng_ref, lnb_ref) = refs[:15]
    if split_out:
        y0_ref, y1_ref, merged_scr, r_scr, y_scr = refs[15:]
    else:
        y_ref, yb_ref, merged_scr, r_scr = refs[15:]
    i = pl.program_id(0)

    @pl.when(i == 0)
    def _():
        r_scr[...] = jnp.zeros_like(r_scr)

    lng, lnb = lng_ref[...], lnb_ref[...]
    for rc in range(r_scr.shape[0] // LN_ROWS):
        rs = slice(rc * LN_ROWS, (rc + 1) * LN_ROWS)
        r = r_scr[rs, :]
        mu = jnp.mean(r, axis=-1, keepdims=True)
        d = r - mu
        var = jnp.mean(d * d, axis=-1, keepdims=True)
        out = d * lax.rsqrt(var + LN_EPS) * lng + lnb
        if split_out:
            y0_ref[rs, :] = out
            y_scr[rs, :] = out
        else:
            y_ref[rs, :] = out
            yb_ref[rs, :] = out.astype(BF16)
    if split_out:
        @pl.when(i <= lead_blocks)
        def _():
            y1_ref[...] = y_scr[...]

    in_first = i >= lead_blocks
    pick = lambda a, b: jnp.where(in_first, a[...], b[...])
    za, zb, zc = pick(za0, za1), pick(zb0, zb1), pick(zc0, zc1)
    x = x_refs[0][...] if n_x == 1 else pick(*x_refs)
    for c in range(D_MODEL // OUT_STAGE_TN):
        cs = slice(c * OUT_STAGE_TN, (c + 1) * OUT_STAGE_TN)
        merged = (_sigmoid(ma_ref[:, cs].astype(F32))
                  * jnp.dot(za, wa_ref[:, cs], preferred_element_type=F32))
        merged += (_sigmoid(mb_ref[:, cs].astype(F32))
                   * jnp.dot(zb, wb_ref[:, cs], preferred_element_type=F32))
        merged += (_sigmoid(mc_ref[:, cs].astype(F32))
                   * jnp.dot(zc, wc_ref[:, cs], preferred_element_type=F32))
        merged_scr[:, cs] = merged.astype(BF16)
    r_scr[...] = alpha * x + jnp.dot(merged_scr[...], wo_ref[...], preferred_element_type=F32)


def _out_stage(xs, zas, zbs, zcs, h_gate, wa, wb, wc, wo, lng, lnb, alpha, *, last, name):
    tm = OUT_STAGE_TM
    m0, m1 = zas[0].shape[0], zas[1].shape[0]
    nb0, nb1 = m0 // tm, m1 // tm
    nb = nb0 + nb1
    first = lambda i: jnp.clip(i - nb1, 0, nb0 - 1)
    second = lambda i: jnp.minimum(i, nb1 - 1)
    cur = lambda i: jnp.where(i < nb1, nb0 + i, jnp.minimum(i, nb - 1) - nb1)
    pair = lambda w: [pl.BlockSpec((tm, w), lambda i: (first(i), 0)),
                      pl.BlockSpec((tm, w), lambda i: (second(i), 0))]
    x_specs = pair(D_MODEL) if len(xs) == 2 else [pl.BlockSpec((tm, D_MODEL), lambda i: (cur(i), 0))]
    gate = lambda j: pl.BlockSpec((tm, D_MODEL), lambda i: (cur(i), j))
    const = lambda a: pl.BlockSpec(a.shape, lambda i: (0,) * a.ndim, pipeline_mode=pl.Buffered(1))
    scratch = [pltpu.VMEM((tm, D_MODEL), BF16), pltpu.VMEM((tm, D_MODEL), F32)]
    if last:
        out_shape = [jax.ShapeDtypeStruct((m0, D_MODEL), F32), jax.ShapeDtypeStruct((m1, D_MODEL), F32)]
        out_specs = [pl.BlockSpec((tm, D_MODEL), lambda i: (first(i - 1), 0)),
                     pl.BlockSpec((tm, D_MODEL), lambda i: (jnp.clip(i - 1, 0, nb1 - 1), 0))]
        scratch.append(pltpu.VMEM((tm, D_MODEL), F32))
    else:
        out_shape = [jax.ShapeDtypeStruct((m0 + m1, D_MODEL), F32),
                     jax.ShapeDtypeStruct((m0 + m1, D_MODEL), BF16)]
        out_specs = [pl.BlockSpec((tm, D_MODEL), lambda i: (cur(jnp.maximum(i - 1, 0)), 0))] * 2
    return pl.pallas_call(
        functools.partial(_out_kernel, n_x=len(xs), lead_blocks=nb1, split_out=last, alpha=alpha),
        out_shape=out_shape,
        grid=(nb + 1,),
        in_specs=x_specs + pair(W_A) + pair(W_B) + pair(W_C) + [gate(0), gate(1), gate(2),
                 const(wa), const(wb), const(wc), const(wo), const(lng), const(lnb)],
        out_specs=out_specs,
        scratch_shapes=scratch,
        compiler_params=_params("arbitrary"),
        name=name,
    )(*xs, *zas, *zbs, *zcs, h_gate, h_gate, h_gate, wa, wb, wc, wo, lng, lnb)


def _layer(groups, xs, xbs, layer, prevs, w, w_in, alpha):
    if xbs is None:
        h_att, xb_all = _in_proj(xs, w["w_att"], N_ATT, emit_xb=True, name=f"in_proj_att_{layer}")
        xbs = (xb_all,)
    else:
        h_att = _in_proj(xbs, w["w_att"], N_ATT, name=f"in_proj_att_{layer}")
    h_ab = _in_proj(xbs, w_in, N_AB, layer=layer, col0=0, name=f"in_proj_ab_{layer}")
    h_gate = _in_proj(xbs, w_in, N_GATE, layer=layer, col0=REF_GATE, out_dtype=BF16,
                      name=f"in_proj_gate_{layer}")
    zas, zbs, zcs, outs, row0 = [], [], [], [], 0
    for (batch, t, pos0, states, tag), prev in zip(groups, prevs):
        st_ret, st_k, st_v, st_pool = states if states is not None else (None,) * 4
        pv_ret, pv_win, pv_pool = prev if prev is not None else (None,) * 3
        if (states is None and t % RET_ROWS == 0) or (states is not None and t <= RET_CHUNK):
            za, o_ret, zb, o_pool = _ret_pool(h_ab, row0, batch, t, pos0, st_ret, st_pool, layer,
                                              pv_ret, pv_pool, w["w_pool_map"], w["pool_scale"],
                                              name=f"ret_pool_{tag}{layer}")
        else:
            za, o_ret = _retention(h_ab, row0, batch, t, pos0, st_ret, layer, pv_ret,
                                   name=f"retention_{tag}{layer}")
            zb, o_pool = _pool(h_ab, row0, batch, t, pos0, st_pool, layer, pv_pool,
                               w["w_pool_map"], w["pool_scale"], name=f"pool_{tag}{layer}")
        zc, o_win = _attention(h_att, row0, batch, t, pos0, st_k, st_v, layer, pv_win, w["sinks"],
                               w["rel_bias"], name=f"attention_{tag}{layer}")
        zas.append(za)
        zbs.append(zb)
        zcs.append(zc)
        outs.append((o_ret, o_win, o_pool))
        row0 += batch * t
    last = layer + 1 == DEPTH
    res = _out_stage(xs, zas, zbs, zcs, h_gate, w["w_ret_o"], w["w_pool_o"], w["w_att_o"],
                     w["w_out"], w["ln_g"], w["ln_b"], alpha, last=last, name=f"out_stage_{layer}")
    if last:
        return tuple(res), None, outs
    y, yb = res
    return (y,), (yb,), outs


def _regroup_heads(w, axis):
    shp = w.shape
    w = w.reshape(shp[:axis] + (KV_C, G_C, HD_C) + shp[axis + 1:])
    w = jnp.swapaxes(w, axis, axis + 1)
    return w.reshape(shp)


def _regroup_matrix():
    new = np.arange(W_C)
    g, kv, d = new // KVW, (new // HD_C) % KV_C, new % HD_C
    p = np.zeros((W_C, W_C), np.float32)
    p[(kv * G_C + g) * HD_C + d, new] = 1.0
    return jnp.asarray(p, BF16)


def _w_att_kernel(q_ref, kv_ref, glo_ref, ghi_ref, perm_ref, o_ref):
    perm = perm_ref[...]
    regroup = lambda w: jnp.dot(w.astype(BF16), perm, preferred_element_type=F32).astype(BF16)
    o_ref[:, OFF_QC:OFF_QC + W_C] = regroup(q_ref[...])
    o_ref[:, OFF_GC:OFF_GC + W_C] = regroup(jnp.concatenate([glo_ref[...], ghi_ref[...]], axis=1))
    o_ref[:, OFF_KC:OFF_KC + 2 * KVW] = kv_ref[...].astype(BF16)


def _prep_w_att(w_in, l):
    half = W_C // 2
    assert REF_QC % W_C == 0 and REF_KC % half == 0 and REF_GC % half == 0
    assert REF_VC == REF_KC + KVW and OFF_VC == OFF_KC + KVW
    blk = lambda width, c0: pl.BlockSpec((None, W_PREP_ROWS, width), lambda i: (l, i, c0 // width))
    return pl.pallas_call(
        _w_att_kernel,
        out_shape=jax.ShapeDtypeStruct((D_MODEL, N_ATT), BF16),
        grid=(D_MODEL // W_PREP_ROWS,),
        in_specs=[blk(W_C, REF_QC), blk(half, REF_KC), blk(half, REF_GC), blk(half, REF_GC + half),
                  pl.BlockSpec((W_C, W_C), lambda i: (0, 0))],
        out_specs=pl.BlockSpec((W_PREP_ROWS, N_ATT), lambda i: (i, 0)),
        compiler_params=_params("parallel"),
        name=f"w_att_prep_{l}",
    )(w_in, w_in, w_in, w_in, _regroup_matrix())


def kernel(x_prompt, x_sample, state_ret, cache_win_k, cache_win_v, state_pool, w_in, w_ret_o,
           w_pool_map, pool_scale, w_pool_o, attn_sinks, w_att_o, w_out, ln_g, ln_b, rel_bias):
    alpha = (2.0 * DEPTH) ** 0.25
    bp, tp, _ = x_prompt.shape
    bs, ts, _ = x_sample.shape
    xp = x_prompt.reshape(bp * tp, D_MODEL)
    xs = x_sample.reshape(bs * ts, D_MODEL)
    to_t = lambda c: c.transpose(0, 1, 3, 4, 2).reshape(DEPTH, bs, KVW, WINDOW)
    from_t = lambda c: c.reshape(DEPTH, bs, KV_C, HD_C, WINDOW).transpose(0, 1, 4, 2, 3)
    ck, cv = to_t(cache_win_k), to_t(cache_win_v)
    groups = ((bp, tp, 0, None, "p"), (bs, ts, PAST_LEN, (state_ret, ck, cv, state_pool), "s"))
    xs, xbs, outs = (xp, xs), None, (None, None)
    for l in range(DEPTH):
        w = {
            "w_att": _prep_w_att(w_in, l),
            "w_ret_o": w_ret_o[l].astype(BF16),
            "w_pool_map": w_pool_map[l].astype(BF16),
            "pool_scale": pool_scale[l].reshape(1, W_B),
            "w_pool_o": w_pool_o[l].astype(BF16),
            "sinks": attn_sinks[l],
            "w_att_o": _regroup_heads(w_att_o[l], 0).astype(BF16),
            "w_out": w_out[l].astype(BF16),
            "ln_g": ln_g[l].reshape(1, D_MODEL),
            "ln_b": ln_b[l].reshape(1, D_MODEL),
            "rel_bias": rel_bias,
        }
        xs, xbs, outs = _layer(groups, xs, xbs, l, outs, w, w_in, alpha)
    ((ret_p,), (kp, vp), (pp,)), ((ret_s,), (ks, vs), (ps,)) = outs
    win = lambda a, b: a.reshape(DEPTH, b, WINDOW, KV_C, HD_C)
    return (xs[0].reshape(bp, tp, D_MODEL), xs[1].reshape(bs, ts, D_MODEL), ret_p, ret_s,
            win(kp, bp), from_t(ks), win(vp, bp), from_t(vs), pp, ps)
```
